```python
import math
import jax, jax.numpy as jnp
from jax import lax
import numpy as np

D_MODEL = 1024
BATCH = 8
SEQ = 4096
DEPTH = 1

D_HEAD = 64
ROPE_THETA = 10000.0
Q_BLOCK = 128
H_DIFF = 4
DIFF_V = 2 * D_HEAD
N_Q_SWA = 8
N_KV_SWA = 2
GQA = N_Q_SWA // N_KV_SWA
WINDOW = 128
MEM_LEN = 256
H_CROSS = 4
D_CROSS = D_MODEL // H_CROSS
N_GROUPS = 4
E_PER_GROUP = 8
N_EXPERTS = N_GROUPS * E_PER_GROUP
TOP_K_INNER = 2
D_FF_EXPERT = 512
EXPERT_BLOCK = 128
EPS = 1e-6

DQ_W = H_DIFF * 2 * D_HEAD
DK_W = H_DIFF * 2 * D_HEAD
DV_W = H_DIFF * DIFF_V
SQ_W = N_Q_SWA * D_HEAD
SK_W = N_KV_SWA * D_HEAD
SV_W = N_KV_SWA * D_HEAD
IN_W = DQ_W + DK_W + DV_W + SQ_W + SK_W + SV_W
SPLITS = [DQ_W, DQ_W + DK_W, DQ_W + DK_W + DV_W, DQ_W + DK_W + DV_W + SQ_W, DQ_W + DK_W + DV_W + SQ_W + SK_W]
MIX_OUT = H_DIFF * DIFF_V + N_Q_SWA * D_HEAD

kernel_name = "hybrid_diffattn_swa_sink_hmoe_block"


def rmsnorm(x, w):
    xf = x.astype(jnp.float32)
    y = xf * lax.rsqrt(jnp.mean(xf * xf, axis=-1, keepdims=True) + EPS)
    return (y * w.astype(jnp.float32)).astype(x.dtype)


def rope_tables(positions):
    inv_freq = jnp.exp(-math.log(ROPE_THETA) * jnp.arange(0, D_HEAD, 2, dtype=jnp.float32) / D_HEAD)
    ang = positions.astype(jnp.float32)[..., None] * inv_freq
    return jnp.cos(ang), jnp.sin(ang)


def rope(x, cos, sin):
    shp = cos.shape[:2] + (1,) * (x.ndim - 3) + cos.shape[-1:]
    c, s = cos.reshape(shp), sin.reshape(shp)
    xf = x.astype(jnp.float32)
    x1, x2 = xf[..., : D_HEAD // 2], xf[..., D_HEAD // 2:]
    return jnp.concatenate([x1 * c - x2 * s, x2 * c + x1 * s], axis=-1).astype(x.dtype)


def diff_attention(q, k, v, lam, subln_w, lambda_init):
    B, S = q.shape[0], q.shape[1]
    nb = S // Q_BLOCK
    scale = D_HEAD ** -0.5
    qb = q.reshape(B, nb, Q_BLOCK, H_DIFF, 2, D_HEAD).transpose(1, 0, 2, 3, 4, 5)
    starts = jnp.arange(nb, dtype=jnp.int32) * Q_BLOCK
    kpos = jnp.arange(S, dtype=jnp.int32)
    qoff = jnp.arange(Q_BLOCK, dtype=jnp.int32)

    def block(args):
        qblk, start = args
        s = jnp.einsum('bqhcd,bkhcd->bhcqk', qblk, k).astype(jnp.float32) * scale
        causal = kpos[None, :] <= (start + qoff)[:, None]
        s = jnp.where(causal, s, -jnp.inf)
        p = jax.nn.softmax(s, axis=-1)
        pd = p[:, :, 0] - lam * p[:, :, 1]
        return jnp.einsum('bhqk,bkhe->bqhe', pd.astype(v.dtype), v)

    o = lax.map(block, (qb, starts))
    o = o.transpose(1, 0, 2, 3, 4).reshape(B, S, H_DIFF, DIFF_V)
    o = rmsnorm(o, subln_w) * (1.0 - lambda_init)
    return o.reshape(B, S, H_DIFF * DIFF_V)


def sliding_window_sink_attention(q, k, v, sinks):
    B, S = q.shape[0], q.shape[1]
    nb = S // Q_BLOCK
    scale = D_HEAD ** -0.5
    qb = q.reshape(B, nb, Q_BLOCK, N_KV_SWA, GQA, D_HEAD)
    kb = k.reshape(B, nb, Q_BLOCK, N_KV_SWA, D_HEAD)
    vb = v.reshape(B, nb, Q_BLOCK, N_KV_SWA, D_HEAD)
    pad = ((0, 0), (1, 0), (0, 0), (0, 0), (0, 0))
    kk = jnp.concatenate([jnp.pad(kb, pad)[:, :-1], kb], axis=2)
    vv = jnp.concatenate([jnp.pad(vb, pad)[:, :-1], vb], axis=2)
    s = jnp.einsum('bnqhgd,bnkhd->bnhgqk', qb, kk).astype(jnp.float32) * scale
    qi = jnp.arange(Q_BLOCK)[:, None]
    ki = jnp.arange(2 * Q_BLOCK)[None, :]
    rel = Q_BLOCK + qi - ki
    band = (rel >= 0) & (rel < WINDOW)
    has_prev = jnp.arange(nb)[:, None, None] > 0
    valid = band[None] & (has_prev | (ki >= Q_BLOCK)[None])
    s = jnp.where(valid[None, :, None, None], s, -jnp.inf)
    sink = jnp.broadcast_to(sinks.astype(jnp.float32).reshape(N_KV_SWA, GQA)[None, None, :, :, None, None],
                            s.shape[:-1] + (1,))
    p = jax.nn.softmax(jnp.concatenate([s, sink], axis=-1), axis=-1)[..., :-1]
    o = jnp.einsum('bnhgqk,bnkhd->bnqhgd', p.astype(v.dtype), vv)
    return o.reshape(B, S, N_Q_SWA * D_HEAD)


def memory_cross_attention(hn, memn, wq, wkv, wo):
    B, S = hn.shape[0], hn.shape[1]
    M = memn.shape[1]
    q = (hn @ wq).reshape(B, S, H_CROSS, D_CROSS)
    kv = memn @ wkv
    k = kv[..., :D_MODEL].reshape(B, M, H_CROSS, D_CROSS)
    v = kv[..., D_MODEL:].reshape(B, M, H_CROSS, D_CROSS)
    s = jnp.einsum('bshd,bmhd->bhsm', q, k).astype(jnp.float32) * (D_CROSS ** -0.5)
    p = jax.nn.softmax(s, axis=-1)
    o = jnp.einsum('bhsm,bmhd->bshd', p.astype(v.dtype), v).reshape(B, S, D_MODEL)
    return o @ wo


def hierarchical_moe(t, w_group, b_group, w_expert, b_expert, w_gate, w_up, w_down):
    N, D = t.shape
    g_logits = (t @ w_group).astype(jnp.float32) + b_group.astype(jnp.float32)
    g_prob = jax.nn.softmax(g_logits, axis=-1)
    g_p, g_sel = lax.top_k(g_prob, 1)
    e_logits = ((t @ w_expert).astype(jnp.float32) + b_expert.astype(jnp.float32)).reshape(N, N_GROUPS, E_PER_GROUP)
    e_in_group = jnp.take_along_axis(e_logits, g_sel[:, :, None], axis=1)[:, 0]
    e_prob = jax.nn.softmax(e_in_group, axis=-1)
    e_p, e_idx = lax.top_k(e_prob, TOP_K_INNER)
    gates = g_p * (e_p / jnp.sum(e_p, axis=-1, keepdims=True))

    eid = (g_sel * E_PER_GROUP + e_idx).reshape(-1).astype(jnp.int32)
    gate = gates.reshape(-1)
    tok = jnp.repeat(jnp.arange(N, dtype=jnp.int32), TOP_K_INNER)
    A = N * TOP_K_INNER

    order = jnp.argsort(eid)
    e_s, tok_s, gate_s = eid[order], tok[order], gate[order]
    counts = jnp.zeros((N_EXPERTS,), jnp.int32).at[eid].add(1)
    starts = jnp.cumsum(counts) - counts
    padded = (counts + EXPERT_BLOCK - 1) // EXPERT_BLOCK * EXPERT_BLOCK
    pends = jnp.cumsum(padded)
    pstarts = pends - padded
    dest = pstarts[e_s] + (jnp.arange(A, dtype=jnp.int32) - starts[e_s])
    P = -(-(A + N_EXPERTS * (EXPERT_BLOCK - 1)) // EXPERT_BLOCK) * EXPERT_BLOCK
    nblk = P // EXPERT_BLOCK
    slot_tok = jnp.full((P,), N, jnp.int32).at[dest].set(tok_s)
    t_pad = jnp.concatenate([t, jnp.zeros((1, D), t.dtype)], axis=0)
    xs = t_pad[slot_tok].reshape(nblk, EXPERT_BLOCK, D)
    blk_start = jnp.arange(nblk, dtype=jnp.int32) * EXPERT_BLOCK
    blk_e = jnp.minimum(jnp.searchsorted(pends, blk_start, side='right'), N_EXPERTS - 1)

    def expert_rows(args):
        xb, e = args
        hdn = jax.nn.silu(xb @ w_gate[e]) * (xb @ w_up[e])
        return hdn @ w_down[e]

    ys = lax.map(expert_rows, (xs, blk_e)).reshape(P, D)
    y_assign = ys[dest].astype(jnp.float32) * gate_s[:, None]
    out = jax.ops.segment_sum(y_assign, tok_s, num_segments=N)
    return out.astype(t.dtype)


def setup_inputs(seed: int = 0) -> dict:
    key = jax.random.key(seed)
    ks = jax.random.split(key, 32)
    f32 = jnp.float32
    nrm = lambda k, shape, scale: jax.random.normal(k, shape, f32) * scale
    gain = lambda k, shape: 1.0 + 0.02 * jax.random.normal(k, shape, f32)
    L = DEPTH
    x = jax.random.normal(ks[0], (BATCH, SEQ, D_MODEL), f32)
    mem = jax.random.normal(ks[1], (BATCH, MEM_LEN, D_MODEL), f32)
    offsets = jax.random.randint(ks[2], (BATCH, 1), 0, 1024, dtype=jnp.int32)
    positions = (offsets + jnp.arange(SEQ, dtype=jnp.int32)[None, :]).astype(jnp.int32)
    return {
        "x": x,
        "mem": mem,
        "positions": positions,
        "ln_mix_w": gain(ks[3], (L, D_MODEL)),
        "w_in": nrm(ks[4], (L, D_MODEL, IN_W), D_MODEL ** -0.5),
        "lambda_q1": nrm(ks[5], (L, D_HEAD), 0.1),
        "lambda_k1": nrm(ks[6], (L, D_HEAD), 0.1),
        "lambda_q2": nrm(ks[7], (L, D_HEAD), 0.1),
        "lambda_k2": nrm(ks[8], (L, D_HEAD), 0.1),
        "subln_w": gain(ks[9], (L, DIFF_V)),
        "sinks": nrm(ks[10], (L, N_Q_SWA), 0.5),
        "w_out": nrm(ks[11], (L, MIX_OUT, D_MODEL), MIX_OUT ** -0.5),
        "ln_cross_w": gain(ks[12], (L, D_MODEL)),
        "ln_mem_w": gain(ks[13], (L, D_MODEL)),
        "wq_cross": nrm(ks[14], (L, D_MODEL, D_MODEL), D_MODEL ** -0.5),
        "wkv_cross": nrm(ks[15], (L, D_MODEL, 2 * D_MODEL), D_MODEL ** -0.5),
        "wo_cross": nrm(ks[16], (L, D_MODEL, D_MODEL), D_MODEL ** -0.5),
        "ln_moe_w": gain(ks[17], (L, D_MODEL)),
        "w_group": nrm(ks[18], (L, D_MODEL, N_GROUPS), D_MODEL ** -0.5),
        "b_group": nrm(ks[19], (L, N_GROUPS), 0.01),
        "w_expert": nrm(ks[20], (L, D_MODEL, N_EXPERTS), D_MODEL ** -0.5),
        "b_expert": nrm(ks[21], (L, N_EXPERTS), 0.01),
        "w_gate": nrm(ks[22], (L, N_EXPERTS, D_MODEL, D_FF_EXPERT), D_MODEL ** -0.5),
        "w_up": nrm(ks[23], (L, N_EXPERTS, D_MODEL, D_FF_EXPERT), D_MODEL ** -0.5),
        "w_down": nrm(ks[24], (L, N_EXPERTS, D_FF_EXPERT, D_MODEL), D_FF_EXPERT ** -0.5),
        "ln_final_w": gain(ks[25], (D_MODEL,)),
    }


def reference(x, mem, positions, ln_mix_w, w_in, lambda_q1, lambda_k1, lambda_q2, lambda_k2, subln_w, sinks,
              w_out, ln_cross_w, ln_mem_w, wq_cross, wkv_cross, wo_cross, ln_moe_w, w_group, b_group,
              w_expert, b_expert, w_gate, w_up, w_down, ln_final_w):
    B, S, D = x.shape
    cos, sin = rope_tables(positions)
    h = x
    for l in range(DEPTH):
        lambda_init = 0.8 - 0.6 * math.exp(-0.3 * l)
        a = rmsnorm(h, ln_mix_w[l])
        proj = a @ w_in[l]
        dq, dk, dv, sq, sk, sv = jnp.split(proj, SPLITS, axis=-1)
        dq = rope(dq.reshape(B, S, H_DIFF, 2, D_HEAD), cos, sin)
        dk = rope(dk.reshape(B, S, H_DIFF, 2, D_HEAD), cos, sin)
        dv = dv.reshape(B, S, H_DIFF, DIFF_V)
        lam = (jnp.exp(jnp.sum(lambda_q1[l].astype(jnp.float32) * lambda_k1[l].astype(jnp.float32)))
               - jnp.exp(jnp.sum(lambda_q2[l].astype(jnp.float32) * lambda_k2[l].astype(jnp.float32)))
               + lambda_init)
        o_diff = diff_attention(dq, dk, dv, lam, subln_w[l], lambda_init)
        sq = rope(sq.reshape(B, S, N_Q_SWA, D_HEAD), cos, sin)
        sk = rope(sk.reshape(B, S, N_KV_SWA, D_HEAD), cos, sin)
        sv = sv.reshape(B, S, N_KV_SWA, D_HEAD)
        o_swa = sliding_window_sink_attention(sq, sk, sv, sinks[l])
        h = h + jnp.concatenate([o_diff, o_swa], axis=-1) @ w_out[l]
        c = rmsnorm(h, ln_cross_w[l])
        m = rmsnorm(mem, ln_mem_w[l])
        h = h + memory_cross_attention(c, m, wq_cross[l], wkv_cross[l], wo_cross[l])
        f = rmsnorm(h, ln_moe_w[l]).reshape(B * S, D)
        h = h + hierarchical_moe(f, w_group[l], b_group[l], w_expert[l], b_expert[l],
                                 w_gate[l], w_up[l], w_down[l]).reshape(B, S, D)
    return rmsnorm(h, ln_final_w)
```

```python
import functools
import math

import jax
import jax.numpy as jnp
from jax import lax
from jax.experimental import pallas as pl
from jax.experimental.pallas import tpu as pltpu

F32 = jnp.float32
BF16 = jnp.bfloat16
I32 = jnp.int32

D_HEAD = 64
ROPE_THETA = 10000.0
H_DIFF = 4
N_Q_SWA = 8
N_KV_SWA = 2
WINDOW = 128
H_CROSS = 4
N_GROUPS = 4
E_PER_GROUP = 8
N_EXPERTS = N_GROUPS * E_PER_GROUP
EPS = 1e-6
LANES = 128
IN_W = 2304
NEG = -1e30

VMEM_LIMIT = 56 * 1024 * 1024

NT_DIMS = (((1,), (1,)), ((), ()))


def _rms(x, w):
    ms = jnp.mean(x * x, axis=-1, keepdims=True)
    return x * lax.rsqrt(ms + EPS) * w


def _inproj_kernel(x_ref, lnw_ref, w_ref, pos_ref, inv_ref, o_ref):
    a = _rms(x_ref[...], lnw_ref[...]).astype(BF16)
    ang = pos_ref[...].astype(F32) * inv_ref[...]
    cos = jnp.cos(ang)
    sin = jnp.sin(ang)
    lane = lax.broadcasted_iota(I32, (1, LANES), 1)
    first = (lane % D_HEAD) < (D_HEAD // 2)
    sin_signed = jnp.where(first, -sin, sin)
    n_chunks = IN_W // 256
    for c in range(n_chunks):
        p = jnp.dot(a, w_ref[:, c * 256:(c + 1) * 256], preferred_element_type=F32)
        for hh in range(2):
            g = c * 2 + hh
            xg = p[:, hh * LANES:(hh + 1) * LANES]
            is_v = (8 <= g < 12) or g == 17
            if not is_v:
                partner = jnp.where(first, pltpu.roll(xg, 96, 1), pltpu.roll(xg, 32, 1))
                xg = xg * cos + partner * sin_signed
                if g < 4 or 12 <= g < 16:
                    xg = xg * (D_HEAD ** -0.5)
            o_ref[:, g * LANES:(g + 1) * LANES] = xg.astype(BF16)


def _inproj(x2, ln_w, w_in_bf, pos2, inv128, tm):
    n, d = x2.shape
    return pl.pallas_call(
        _inproj_kernel,
        grid=(n // tm,),
        in_specs=[
            pl.BlockSpec((tm, d), lambda i: (i, 0)),
            pl.BlockSpec((1, d), lambda i: (0, 0)),
            pl.BlockSpec((d, IN_W), lambda i: (0, 0)),
            pl.BlockSpec((tm, 1), lambda i: (i, 0)),
            pl.BlockSpec((1, LANES), lambda i: (0, 0)),
        ],
        out_specs=pl.BlockSpec((tm, IN_W), lambda i: (i, 0)),
        out_shape=jax.ShapeDtypeStruct((n, IN_W), BF16),
        compiler_params=pltpu.CompilerParams(
            dimension_semantics=("arbitrary",), vmem_limit_bytes=VMEM_LIMIT),
        name="inproj",
    )(x2, ln_w, w_in_bf, pos2, inv128)


def _diff_kernel(q_ref, k_ref, v_ref, lq1_ref, lk1_ref, lq2_ref, lk2_ref, sw_ref, o_ref,
                 m_scr, l_scr, acc_scr, *, t, lambda_init):
    i = pl.program_id(2)
    lane = lax.broadcasted_iota(I32, (1, LANES), 1)
    q = q_ref[...]
    zero = jnp.zeros_like(q)
    qc = (jnp.where(lane < D_HEAD, q, zero), jnp.where(lane >= D_HEAD, q, zero))
    m_scr[...] = jnp.full(m_scr.shape, NEG, F32)
    l_scr[...] = jnp.zeros(l_scr.shape, F32)
    acc_scr[...] = jnp.zeros(acc_scr.shape, F32)

    def step(j, masked):
        start = pl.multiple_of(j * t, t)
        k = k_ref[pl.ds(start, t), :]
        v = v_ref[pl.ds(start, t), :]
        for c in range(2):
            s = lax.dot_general(qc[c], k, NT_DIMS, preferred_element_type=F32)
            if masked:
                row = lax.broadcasted_iota(I32, (t, t), 0)
                col = lax.broadcasted_iota(I32, (t, t), 1)
                s = jnp.where(row >= col, s, NEG)
            m_prev = m_scr[c]
            m_new = jnp.maximum(m_prev, jnp.max(s, axis=1, keepdims=True))
            p = jnp.exp(s - m_new[:, 0:1])
            alpha = jnp.exp(m_prev - m_new)
            l_scr[c] = alpha * l_scr[c] + jnp.sum(p, axis=1, keepdims=True)
            acc_scr[c] = alpha * acc_scr[c] + jnp.dot(p.astype(BF16), v, preferred_element_type=F32)
            m_scr[c] = m_new

    def body(j, carry):
        step(j, False)
        return carry

    lax.fori_loop(0, i, body, 0)
    step(i, True)

    lam = (jnp.exp(jnp.sum(lq1_ref[...] * lk1_ref[...], axis=1, keepdims=True))
           - jnp.exp(jnp.sum(lq2_ref[...] * lk2_ref[...], axis=1, keepdims=True))
           + lambda_init)
    o = acc_scr[0] / l_scr[0] - lam * (acc_scr[1] / l_scr[1])
    o = _rms(o, sw_ref[...]) * (1.0 - lambda_init)
    o_ref[...] = o.astype(BF16)


def _diff_attention(qkv, lq1, lk1, lq2, lk2, subln_w, b, s, t, lambda_init):
    n = qkv.shape[0]
    nq = s // t
    small = pl.BlockSpec((1, D_HEAD), lambda bi, h, i: (0, 0))
    return pl.pallas_call(
        functools.partial(_diff_kernel, t=t, lambda_init=lambda_init),
        grid=(b, H_DIFF, nq),
        in_specs=[
            pl.BlockSpec((t, LANES), lambda bi, h, i: (bi * nq + i, h)),
            pl.BlockSpec((s, LANES), lambda bi, h, i: (bi, 4 + h)),
            pl.BlockSpec((s, LANES), lambda bi, h, i: (bi, 8 + h)),
            small, small, small, small,
            pl.BlockSpec((1, LANES), lambda bi, h, i: (0, 0)),
        ],
        out_specs=pl.BlockSpec((t, LANES), lambda bi, h, i: (bi * nq + i, h)),
        out_shape=jax.ShapeDtypeStruct((n, H_DIFF * LANES), BF16),
        scratch_shapes=[pltpu.VMEM((2, t, LANES), F32)] * 3,
        compiler_params=pltpu.CompilerParams(
            dimension_semantics=("arbitrary", "arbitrary", "arbitrary"), vmem_limit_bytes=VMEM_LIMIT),
        name="diffattn",
    )(qkv, qkv, qkv, lq1, lk1, lq2, lk2, subln_w)


def _swa_kernel(sink_ref, q_ref, kc_ref, kp_ref, vc_ref, vp_ref, o_ref, kbuf, vbuf, *, tq):
    i = pl.program_id(1)
    w = WINDOW
    kbuf[0:w, :] = kp_ref[...]
    kbuf[w:w + tq, :] = kc_ref[...]
    vbuf[0:w, :] = vp_ref[...]
    vbuf[w:w + tq, :] = vc_ref[...]
    lane = lax.broadcasted_iota(I32, (1, LANES), 1)
    lo = lane < D_HEAD
    qi = lax.broadcasted_iota(I32, (w, 2 * w), 0)
    ki = lax.broadcasted_iota(I32, (w, 2 * w), 1)
    band = (ki > qi) & (ki <= qi + w)
    band_first = band & (ki >= jnp.where(i > 0, 0, w))
    for r in range(tq // w):
        keys = kbuf[r * w:(r + 2) * w, :]
        vals = vbuf[r * w:(r + 2) * w, :]
        valid = band_first if r == 0 else band
        qs = []
        for half in range(2):
            for p in range(4):
                qg = q_ref[r * w:(r + 1) * w, p * LANES:(p + 1) * LANES]
                qs.append(jnp.where(lo if half == 0 else ~lo, qg, jnp.zeros_like(qg)))
        qstack = jnp.concatenate(qs, axis=0)
        s_all = lax.dot_general(qstack, keys, NT_DIMS, preferred_element_type=F32)
        ps = []
        for hb in range(8):
            sink = sink_ref[hb]
            s = jnp.where(valid, s_all[hb * w:(hb + 1) * w], NEG)
            m = jnp.maximum(jnp.max(s, axis=1, keepdims=True), sink)
            e = jnp.exp(s - m)
            den = jnp.sum(e, axis=1, keepdims=True) + jnp.exp(sink - m)
            ps.append((e / den).astype(BF16))
        pv = jnp.dot(jnp.concatenate(ps, axis=0), vals, preferred_element_type=F32)
        for p in range(4):
            og = jnp.where(lo, pv[p * w:(p + 1) * w], pv[(4 + p) * w:(5 + p) * w])
            o_ref[r * w:(r + 1) * w, p * LANES:(p + 1) * LANES] = og.astype(BF16)


def _swa_attention(qkv, sinks, b, s, tq):
    n = qkv.shape[0]
    nq = s // tq
    per = tq // WINDOW
    prev_map = lambda col: (lambda bi, i, sk: (jnp.maximum(bi * (s // WINDOW) + i * per - 1, 0), col))
    cur_map = lambda col: (lambda bi, i, sk: (bi * nq + i, col))
    return pl.pallas_call(
        functools.partial(_swa_kernel, tq=tq),
        grid_spec=pltpu.PrefetchScalarGridSpec(
            num_scalar_prefetch=1,
            grid=(b, nq),
            in_specs=[
                pl.BlockSpec((tq, 4 * LANES), lambda bi, i, sk: (bi * nq + i, 3)),
                pl.BlockSpec((tq, LANES), cur_map(16)),
                pl.BlockSpec((WINDOW, LANES), prev_map(16)),
                pl.BlockSpec((tq, LANES), cur_map(17)),
                pl.BlockSpec((WINDOW, LANES), prev_map(17)),
            ],
            out_specs=pl.BlockSpec((tq, 4 * LANES), lambda bi, i, sk: (bi * nq + i, 0)),
            scratch_shapes=[pltpu.VMEM((WINDOW + tq, LANES), BF16)] * 2,
        ),
        out_shape=jax.ShapeDtypeStruct((n, 4 * LANES), BF16),
        compiler_params=pltpu.CompilerParams(
            dimension_semantics=("arbitrary", "arbitrary"), vmem_limit_bytes=VMEM_LIMIT),
        name="swa",
    )(sinks, qkv, qkv, qkv, qkv, qkv)


def _memkv_kernel(m_ref, lnw_ref, w_ref, o_ref):
    a = _rms(m_ref[...], lnw_ref[...]).astype(BF16)
    o_ref[...] = jnp.dot(a, w_ref[...], preferred_element_type=F32).astype(BF16)


def _memkv(mem2, ln_w, wkv_bf, m_len):
    n, d = mem2.shape
    return pl.pallas_call(
        _memkv_kernel,
        grid=(n // m_len,),
        in_specs=[
            pl.BlockSpec((m_len, d), lambda i: (i, 0)),
            pl.BlockSpec((1, d), lambda i: (0, 0)),
            pl.BlockSpec((d, 2 * d), lambda i: (0, 0)),
        ],
        out_specs=pl.BlockSpec((m_len, 2 * d), lambda i: (i, 0)),
        out_shape=jax.ShapeDtypeStruct((n, 2 * d), BF16),
        compiler_params=pltpu.CompilerParams(
            dimension_semantics=("arbitrary",), vmem_limit_bytes=VMEM_LIMIT),
        name="memkv",
    )(mem2, ln_w, wkv_bf)


def _split_bf16(x):
    hi = x.astype(BF16)
    lo = (x - hi.astype(F32)).astype(BF16)
    return hi, lo


def _post_kernel(x_ref, od_ref, os_ref, wout_ref, lnc_ref, wq_ref, kv_ref, wo_ref, lnm_ref, wr_ref, br_ref,
                 h_ref, f_ref, code_ref, gate_ref, cnt_ref, base_scr, *, tm, d):
    first_step = (pl.program_id(0) == 0) & (pl.program_id(1) == 0)

    @pl.when(first_step)
    def _():
        base_scr[...] = jnp.zeros(base_scr.shape, F32)

    mix = jnp.concatenate([od_ref[...], os_ref[...]], axis=1)
    h1 = x_ref[...] + jnp.dot(mix, wout_ref[...], preferred_element_type=F32)

    c = _rms(h1, lnc_ref[...]).astype(BF16)
    dc = d // H_CROSS
    q = (jnp.dot(c, wq_ref[...], preferred_element_type=F32) * (dc ** -0.5)).astype(BF16)
    outs = []
    for hd in range(H_CROSS):
        k = kv_ref[:, hd * dc:(hd + 1) * dc]
        v = kv_ref[:, d + hd * dc:d + (hd + 1) * dc]
        s = lax.dot_general(q[:, hd * dc:(hd + 1) * dc], k, NT_DIMS, preferred_element_type=F32)
        e = jnp.exp(s - jnp.max(s, axis=1, keepdims=True))
        den = jnp.sum(e, axis=1, keepdims=True)
        outs.append((jnp.dot(e.astype(BF16), v, preferred_element_type=F32) / den).astype(BF16))
    o = jnp.concatenate(outs, axis=1)
    h2 = h1 + jnp.dot(o, wo_ref[...], preferred_element_type=F32)
    h_ref[...] = h2
    f = _rms(h2, lnm_ref[...])
    f_ref[...] = f

    f_hi, f_lo = _split_bf16(f)
    w_hi, w_lo = _split_bf16(wr_ref[...])
    lg = (lax.dot_general(w_hi, f_hi, NT_DIMS, preferred_element_type=F32)
          + lax.dot_general(w_hi, f_lo, NT_DIMS, preferred_element_type=F32)
          + lax.dot_general(w_lo, f_hi, NT_DIMS, preferred_element_type=F32)) + br_ref[...]

    gl = lg[0:N_GROUPS]
    gmax = jnp.max(gl, axis=0, keepdims=True)
    gidx = lax.broadcasted_iota(I32, gl.shape, 0)
    g_sel = jnp.min(jnp.where(gl == gmax, gidx, N_GROUPS), axis=0, keepdims=True)
    g_p = 1.0 / jnp.sum(jnp.exp(gl - gmax), axis=0, keepdims=True)

    e8 = jnp.zeros((E_PER_GROUP, tm), F32)
    for g in range(N_GROUPS):
        e8 = e8 + jnp.where(g_sel == g, lg[8 + g * E_PER_GROUP:8 + (g + 1) * E_PER_GROUP], 0.0)
    ex = jnp.exp(e8 - jnp.max(e8, axis=0, keepdims=True))
    ep = ex / jnp.sum(ex, axis=0, keepdims=True)
    idx8 = lax.broadcasted_iota(I32, ep.shape, 0)
    p1 = jnp.max(ep, axis=0, keepdims=True)
    i1 = jnp.min(jnp.where(ep == p1, idx8, E_PER_GROUP), axis=0, keepdims=True)
    ep2 = jnp.where(idx8 == i1, -1.0, ep)
    p2 = jnp.max(ep2, axis=0, keepdims=True)
    i2 = jnp.min(jnp.where(ep2 == p2, idx8, E_PER_GROUP), axis=0, keepdims=True)
    psum = p1 + p2
    gate_ref[0:1, :] = g_p * (p1 / psum)
    gate_ref[1:2, :] = g_p * (p2 / psum)
    eid1 = g_sel * E_PER_GROUP + i1
    eid2 = g_sel * E_PER_GROUP + i2

    e32 = lax.broadcasted_iota(I32, (N_EXPERTS, tm), 0)
    oh1 = (e32 == eid1).astype(F32)
    oh2 = (e32 == eid2).astype(F32)
    cnt = oh1 + oh2
    tr = lax.broadcasted_iota(I32, (tm, tm), 0)
    tc = lax.broadcasted_iota(I32, (tm, tm), 1)
    upper = jnp.where(tr < tc, 1.0, 0.0).astype(BF16)
    before = jnp.dot(cnt.astype(BF16), upper, preferred_element_type=F32) + base_scr[:, 0:1]
    rank1 = jnp.sum(oh1 * before, axis=0, keepdims=True).astype(I32)
    rank2 = jnp.sum(oh2 * before, axis=0, keepdims=True).astype(I32)
    code_ref[0:1, :] = eid1 * 65536 + rank1
    code_ref[1:2, :] = eid2 * 65536 + rank2
    base_new = base_scr[...] + jnp.sum(cnt, axis=1, keepdims=True)
    base_scr[...] = base_new
    cnt_ref[...] = base_new


def _post(x2, od, osw, wout_bf, lnc, wq_bf, kv, wo_bf, lnm, wr, br, b, s, tm, m_len):
    n, d = x2.shape
    nt = s // tm
    row = lambda bi, i: (bi * nt + i, 0)
    const = lambda bi, i: (0, 0)
    return pl.pallas_call(
        functools.partial(_post_kernel, tm=tm, d=d),
        grid=(b, nt),
        in_specs=[
            pl.BlockSpec((tm, d), row),
            pl.BlockSpec((tm, d // 2), row),
            pl.BlockSpec((tm, d // 2), row),
            pl.BlockSpec((d, d), const),
            pl.BlockSpec((1, d), const),
            pl.BlockSpec((d, d), const),
            pl.BlockSpec((m_len, 2 * d), lambda bi, i: (bi, 0)),
            pl.BlockSpec((d, d), const),
            pl.BlockSpec((1, d), const),
            pl.BlockSpec((8 + N_EXPERTS, d), const),
            pl.BlockSpec((8 + N_EXPERTS, 1), const),
        ],
        out_specs=[
            pl.BlockSpec((tm, d), row),
            pl.BlockSpec((tm, d), row),
            pl.BlockSpec((2, tm), lambda bi, i: (0, bi * nt + i)),
            pl.BlockSpec((2, tm), lambda bi, i: (0, bi * nt + i)),
            pl.BlockSpec((N_EXPERTS, LANES), const),
        ],
        out_shape=[
            jax.ShapeDtypeStruct((n, d), F32),
            jax.ShapeDtypeStruct((n, d), F32),
            jax.ShapeDtypeStruct((2, n), I32),
            jax.ShapeDtypeStruct((2, n), F32),
            jax.ShapeDtypeStruct((N_EXPERTS, LANES), F32),
        ],
        scratch_shapes=[pltpu.VMEM((N_EXPERTS, LANES), F32)],
        compiler_params=pltpu.CompilerParams(
            dimension_semantics=("arbitrary", "arbitrary"), vmem_limit_bytes=VMEM_LIMIT),
        name="post",
    )(x2, od, osw, wout_bf, lnc, wq_bf, kv, wo_bf, lnm, wr, br)


def _dispatch_kernel(code_ref, cnt_ref, f_hbm, xs_in, xs_hbm, dest_ref, blk_ref, nact_ref,
                     pstart, sem, *, n, ch, bm, nblk):
    del xs_in
    i = pl.program_id(0)

    @pl.when(i == 0)
    def _():
        def per_expert(e, blk0):
            nb = (cnt_ref[e] + (bm - 1)) // bm
            pstart[e] = blk0 * bm

            def fill(j, carry):
                blk_ref[j] = e
                return carry

            lax.fori_loop(blk0, blk0 + nb, fill, 0)
            return blk0 + nb

        nact = lax.fori_loop(0, N_EXPERTS, per_expert, 0)
        nact_ref[0] = nact

        def tail(j, carry):
            blk_ref[j] = N_EXPERTS - 1
            return carry

        lax.fori_loop(nact, nblk, tail, 0)

    def row_copy(t, dst):
        return pltpu.make_async_copy(f_hbm.at[pl.ds(t, 1)], xs_hbm.at[pl.ds(dst, 1)], sem)

    def body(t, carry):
        for k in range(2):
            c = code_ref[k * n + t]
            dst = pstart[c >> 16] + (c & 0xFFFF)
            dest_ref[k * n + t] = dst
            row_copy(t, dst).start()
        return carry

    lax.fori_loop(i * ch, (i + 1) * ch, body, 0)

    def drain(t, carry):
        row_copy(0, 0).wait()
        row_copy(0, 0).wait()
        return carry

    lax.fori_loop(0, ch, drain, 0)


def _dispatch(code_flat, cnt_i, f, xs0, bm, nblk, ch):
    n, d = f.shape
    smem = pl.BlockSpec(memory_space=pltpu.SMEM)
    hbm = pl.BlockSpec(memory_space=pl.ANY)
    return pl.pallas_call(
        functools.partial(_dispatch_kernel, n=n, ch=ch, bm=bm, nblk=nblk),
        grid_spec=pltpu.PrefetchScalarGridSpec(
            num_scalar_prefetch=2,
            grid=(n // ch,),
            in_specs=[hbm, hbm],
            out_specs=[hbm, smem, smem, smem],
            scratch_shapes=[pltpu.SMEM((N_EXPERTS,), I32), pltpu.SemaphoreType.DMA],
        ),
        out_shape=[
            jax.ShapeDtypeStruct(xs0.shape, xs0.dtype),
            jax.ShapeDtypeStruct((2 * n,), I32),
            jax.ShapeDtypeStruct((nblk,), I32),
            jax.ShapeDtypeStruct((1,), I32),
        ],
        input_output_aliases={3: 0},
        compiler_params=pltpu.CompilerParams(dimension_semantics=("arbitrary",)),
        name="dispatch",
    )(code_flat, cnt_i, f, xs0)


def _expert_kernel(blk_ref, nact_ref, x_ref, wg_ref, wu_ref, wd_ref, y_ref):
    j = pl.program_id(0)

    @pl.when(j < nact_ref[0])
    def _():
        x = x_ref[...].astype(BF16)
        g = jnp.dot(x, wg_ref[...], preferred_element_type=F32)
        u = jnp.dot(x, wu_ref[...], preferred_element_type=F32)
        hdn = (g * jax.nn.sigmoid(g) * u).astype(BF16)
        y_ref[...] = jnp.dot(hdn, wd_ref[...], preferred_element_type=F32)


def _experts(blk_e, nact, xs, wg_bf, wu_bf, wd_bf, bm):
    p, d = xs.shape
    dff = wg_bf.shape[-1]
    nblk = p // bm
    rowmap = lambda j, blk, na: (jnp.minimum(j, na[0] - 1), 0)
    wmap = lambda j, blk, na: (blk[jnp.minimum(j, na[0] - 1)], 0, 0)
    return pl.pallas_call(
        _expert_kernel,
        grid_spec=pltpu.PrefetchScalarGridSpec(
            num_scalar_prefetch=2,
            grid=(nblk,),
            in_specs=[
                pl.BlockSpec((bm, d), rowmap),
                pl.BlockSpec((None, d, dff), wmap),
                pl.BlockSpec((None, d, dff), wmap),
                pl.BlockSpec((None, dff, d), wmap),
            ],
            out_specs=pl.BlockSpec((bm, d), rowmap),
        ),
        out_shape=jax.ShapeDtypeStruct((p, d), F32),
        compiler_params=pltpu.CompilerParams(
            dimension_semantics=("arbitrary",), vmem_limit_bytes=VMEM_LIMIT),
        name="experts",
    )(blk_e, nact, xs, wg_bf, wu_bf, wd_bf)


def _combine_kernel(dest_ref, h_ref, gate_ref, lnf_ref, ys_hbm, o_ref, ybuf, sem, *, n, tm):
    i = pl.program_id(0)
    nsteps = pl.num_programs(0)
    slot = i % 2

    def row_copy(src, k, sl, r):
        return pltpu.make_async_copy(ys_hbm.at[pl.ds(src, 1)], ybuf.at[sl, k, pl.ds(r, 1)], sem.at[sl])

    def issue(step, sl):
        def body(r, carry):
            for k in range(2):
                row_copy(dest_ref[k * n + step * tm + r], k, sl, r).start()
            return carry

        lax.fori_loop(0, tm, body, 0)

    @pl.when(i == 0)
    def _():
        issue(0, 0)

    @pl.when(i + 1 < nsteps)
    def _():
        issue(i + 1, 1 - slot)

    def drain(r, carry):
        for k in range(2):
            row_copy(0, k, slot, r).wait()
        return carry

    lax.fori_loop(0, tm, drain, 0)

    gpad = jnp.concatenate([gate_ref[...], jnp.zeros((LANES - 2, tm), F32)], axis=0)
    gt = gpad.T
    h = h_ref[...] + gt[:, 0:1] * ybuf[slot, 0] + gt[:, 1:2] * ybuf[slot, 1]
    o_ref[...] = _rms(h, lnf_ref[...])


def _combine(dest, h2, gates, lnf, ys, tm):
    n, d = h2.shape
    return pl.pallas_call(
        functools.partial(_combine_kernel, n=n, tm=tm),
        grid_spec=pltpu.PrefetchScalarGridSpec(
            num_scalar_prefetch=1,
            grid=(n // tm,),
            in_specs=[
                pl.BlockSpec((tm, d), lambda i, dst: (i, 0)),
                pl.BlockSpec((2, tm), lambda i, dst: (0, i)),
                pl.BlockSpec((1, d), lambda i, dst: (0, 0)),
                pl.BlockSpec(memory_space=pl.ANY),
            ],
            out_specs=pl.BlockSpec((tm, d), lambda i, dst: (i, 0)),
            scratch_shapes=[pltpu.VMEM((2, 2, tm, d), F32), pltpu.SemaphoreType.DMA((2,))],
        ),
        out_shape=jax.ShapeDtypeStruct((n, d), F32),
        compiler_params=pltpu.CompilerParams(
            dimension_semantics=("arbitrary",), vmem_limit_bytes=VMEM_LIMIT),
        name="combine",
    )(dest, h2, gates, lnf, ys)


def _swa_head_perm():
    cols = []
    for p in range(4):
        for half in range(2):
            head = half * 4 + p
            cols.extend(range(head * D_HEAD, (head + 1) * D_HEAD))
    return jnp.asarray(cols, dtype=I32)


def kernel(x, mem, positions, ln_mix_w, w_in, lambda_q1, lambda_k1, lambda_q2, lambda_k2, subln_w, sinks,
           w_out, ln_cross_w, ln_mem_w, wq_cross, wkv_cross, wo_cross, ln_moe_w, w_group, b_group,
           w_expert, b_expert, w_gate, w_up, w_down, ln_final_w):
    b, s, d = x.shape
    m_len = mem.shape[1]
    n = b * s
    assert w_in.shape[0] == 1 and d == 1024 and n <= 65536
    lambda_init = 0.8 - 0.6 * math.exp(-0.3 * 0)

    tm = 512
    t_attn = 256
    bm = 256
    ch = min(2048, n)

    x2 = x.reshape(n, d)
    pos2 = positions.reshape(n, 1).astype(I32)
    inv_freq = jnp.exp(-math.log(ROPE_THETA) * jnp.arange(0, D_HEAD, 2, dtype=F32) / D_HEAD)
    inv128 = jnp.tile(inv_freq, LANES // (D_HEAD // 2)).reshape(1, LANES)

    perm = _swa_head_perm()
    sq0 = 3 * 512
    w_in_l = w_in[0]
    w_in_p = jnp.concatenate([w_in_l[:, :sq0], w_in_l[:, sq0:sq0 + 512][:, perm], w_in_l[:, sq0 + 512:]], axis=1)
    w_out_l = w_out[0]
    w_out_p = jnp.concatenate([w_out_l[:512], w_out_l[512:][perm]], axis=0)
    sinks_p = sinks[0].reshape(2, 4).reshape(-1)

    qkv = _inproj(x2, ln_mix_w[0].reshape(1, d), w_in_p.astype(BF16), pos2, inv128, tm)
    o_diff = _diff_attention(qkv, lambda_q1[0].reshape(1, -1), lambda_k1[0].reshape(1, -1),
                             lambda_q2[0].reshape(1, -1), lambda_k2[0].reshape(1, -1),
                             subln_w[0].reshape(1, -1), b, s, t_attn, lambda_init)
    o_swa = _swa_attention(qkv, sinks_p.astype(F32), b, s, tm)
    kv = _memkv(mem.reshape(b * m_len, d), ln_mem_w[0].reshape(1, d), wkv_cross[0].astype(BF16), m_len)

    wr = jnp.concatenate([w_group[0].T, jnp.zeros((8 - N_GROUPS, d), F32), w_expert[0].T], axis=0)
    br = jnp.concatenate([b_group[0], jnp.zeros((8 - N_GROUPS,), F32), b_expert[0]]).reshape(-1, 1)
    h2, f, code, gates, counts = _post(
        x2, o_diff, o_swa, w_out_p.astype(BF16), ln_cross_w[0].reshape(1, d), wq_cross[0].astype(BF16), kv,
        wo_cross[0].astype(BF16), ln_moe_w[0].reshape(1, d), wr, br, b, s, tm, m_len)

    nblk = (2 * n + N_EXPERTS * (bm - 1) + bm - 1) // bm
    xs0 = jnp.zeros((nblk * bm, d), F32)
    cnt_i = counts[:, 0].astype(I32)
    xs, dest, blk_e, nact = _dispatch(code.reshape(-1), cnt_i, f, xs0, bm, nblk, ch)
    ys = _experts(blk_e, nact, xs, w_gate[0].astype(BF16), w_up[0].astype(BF16), w_down[0].astype(BF16), bm)
    out = _combine(dest, h2, gates, ln_final_w.reshape(1, d), ys, tm)
    return out.reshape(b, s, d)
```

```python
import functools
import math

import jax
import jax.numpy as jnp
from jax import lax
from jax.experimental import pallas as pl
from jax.experimental.pallas import tpu as pltpu

F32 = jnp.float32
BF16 = jnp.bfloat16
I32 = jnp.int32

D_HEAD = 64
ROPE_THETA = 10000.0
H_DIFF = 4
N_Q_SWA = 8
N_KV_SWA = 2
WINDOW = 128
H_CROSS = 4
N_GROUPS = 4
E_PER_GROUP = 8
N_EXPERTS = N_GROUPS * E_PER_GROUP
EPS = 1e-6
LANES = 128
IN_W = 2304
NEG = -1e30

VMEM_LIMIT = 56 * 1024 * 1024

NT_DIMS = (((1,), (1,)), ((), ()))


def _rms(x, w):
    ms = jnp.mean(x * x, axis=-1, keepdims=True)
    return x * lax.rsqrt(ms + EPS) * w


def _inproj_kernel(x_ref, lnw_ref, w_ref, pos_ref, inv_ref, o_ref):
    a = _rms(x_ref[...], lnw_ref[...]).astype(BF16)
    ang = pos_ref[...].astype(F32) * inv_ref[...]
    cos = jnp.cos(ang)
    sin = jnp.sin(ang)
    lane = lax.broadcasted_iota(I32, (1, LANES), 1)
    first = (lane % D_HEAD) < (D_HEAD // 2)
    sin_signed = jnp.where(first, -sin, sin)
    n_chunks = IN_W // 256
    for c in range(n_chunks):
        p = jnp.dot(a, w_ref[:, c * 256:(c + 1) * 256], preferred_element_type=F32)
        for hh in range(2):
            g = c * 2 + hh
            xg = p[:, hh * LANES:(hh + 1) * LANES]
            is_v = (8 <= g < 12) or g == 17
            if not is_v:
                partner = jnp.where(first, pltpu.roll(xg, 96, 1), pltpu.roll(xg, 32, 1))
                xg = xg * cos + partner * sin_signed
                if g < 4:
                    xg = xg * (D_HEAD ** -0.5 * math.log2(math.e))
                elif 12 <= g < 16:
                    xg = xg * (D_HEAD ** -0.5)
            o_ref[:, g * LANES:(g + 1) * LANES] = xg.astype(BF16)


def _inproj(x2, ln_w, w_in_bf, pos2, inv128, tm):
    n, d = x2.shape
    return pl.pallas_call(
        _inproj_kernel,
        grid=(n // tm,),
        in_specs=[
            pl.BlockSpec((tm, d), lambda i: (i, 0)),
            pl.BlockSpec((1, d), lambda i: (0, 0)),
            pl.BlockSpec((d, IN_W), lambda i: (0, 0)),
            pl.BlockSpec((tm, 1), lambda i: (i, 0)),
            pl.BlockSpec((1, LANES), lambda i: (0, 0)),
        ],
        out_specs=pl.BlockSpec((tm, IN_W), lambda i: (i, 0)),
        out_shape=jax.ShapeDtypeStruct((n, IN_W), BF16),
        compiler_params=pltpu.CompilerParams(
            dimension_semantics=("arbitrary",), vmem_limit_bytes=VMEM_LIMIT),
        name="inproj",
    )(x2, ln_w, w_in_bf, pos2, inv128)


def _diff_kernel(q_ref, k_ref, v_ref, lq1_ref, lk1_ref, lq2_ref, lk2_ref, sw_ref, o_ref,
                 vt_scr, acc_scr, *, t, tk, lambda_init):
    i = pl.program_id(2)

    @pl.when(i == 0)
    def _():
        for c in range(vt_scr.shape[0]):
            for r in range(tk // t):
                rows = slice(c * tk + r * t, c * tk + (r + 1) * t)
                vt_scr[c, :, r * t:(r + 1) * t] = v_ref[rows, :].astype(F32).T.astype(BF16)

    lane = lax.broadcasted_iota(I32, (1, LANES), 1)
    q = q_ref[...]
    zero = jnp.zeros_like(q)
    q2 = jnp.concatenate([jnp.where(lane < D_HEAD, q, zero), jnp.where(lane >= D_HEAD, q, zero)], axis=0)
    acc_scr[...] = jnp.zeros(acc_scr.shape, F32)
    n_full = (i * t) // tk

    def step(j, carry, masked):
        m_prev, l_prev = carry
        k = k_ref[pl.ds(pl.multiple_of(j * tk, tk), tk), :]
        st = lax.dot_general(k, q2, NT_DIMS, preferred_element_type=F32)
        if masked:
            key = lax.broadcasted_iota(I32, (tk, 2 * t), 0) + j * tk
            col = lax.broadcasted_iota(I32, (tk, 2 * t), 1)
            qry = jnp.where(col >= t, col - t, col) + i * t
            st = jnp.where(key <= qry, st, NEG)
        m_new = jnp.maximum(m_prev, jnp.max(st, axis=0, keepdims=True))
        p = jnp.exp2(st - m_new)
        alpha = jnp.exp2(m_prev - m_new)
        l_new = alpha * l_prev + jnp.sum(p, axis=0, keepdims=True)
        acc_scr[...] = alpha * acc_scr[...] + jnp.dot(vt_scr[j], p.astype(BF16), preferred_element_type=F32)
        return m_new, l_new

    init = (jnp.full((1, 2 * t), NEG, F32), jnp.zeros((1, 2 * t), F32))
    carry = lax.fori_loop(0, n_full, lambda j, cr: step(j, cr, False), init)
    _, l = step(n_full, carry, True)

    lam = (jnp.exp(jnp.sum(lq1_ref[...] * lk1_ref[...], axis=1, keepdims=True))
           - jnp.exp(jnp.sum(lq2_ref[...] * lk2_ref[...], axis=1, keepdims=True))
           + lambda_init)
    on = acc_scr[...] / l
    o = on[:, :t] - lam * on[:, t:]
    ms = jnp.mean(o * o, axis=0, keepdims=True)
    o = o * lax.rsqrt(ms + EPS) * sw_ref[...] * (1.0 - lambda_init)
    o_ref[...] = o.T.astype(BF16)


def _diff_attention(qkv, lq1, lk1, lq2, lk2, subln_col, b, s, t, tk, lambda_init):
    n = qkv.shape[0]
    nq = s // t
    small = pl.BlockSpec((1, D_HEAD), lambda bi, h, i: (0, 0))
    return pl.pallas_call(
        functools.partial(_diff_kernel, t=t, tk=tk, lambda_init=lambda_init),
        grid=(b, H_DIFF, nq),
        in_specs=[
            pl.BlockSpec((t, LANES), lambda bi, h, i: (bi * nq + i, h)),
            pl.BlockSpec((s, LANES), lambda bi, h, i: (bi, 4 + h)),
            pl.BlockSpec((s, LANES), lambda bi, h, i: (bi, 8 + h)),
            small, small, small, small,
            pl.BlockSpec((LANES, 1), lambda bi, h, i: (0, 0)),
        ],
        out_specs=pl.BlockSpec((t, LANES), lambda bi, h, i: (bi * nq + i, h)),
        out_shape=jax.ShapeDtypeStruct((n, H_DIFF * LANES), BF16),
        scratch_shapes=[pltpu.VMEM((s // tk, LANES, tk), BF16), pltpu.VMEM((LANES, 2 * t), F32)],
        compiler_params=pltpu.CompilerParams(
            dimension_semantics=("arbitrary", "arbitrary", "arbitrary"), vmem_limit_bytes=VMEM_LIMIT),
        name="diffattn",
    )(qkv, qkv, qkv, lq1, lk1, lq2, lk2, subln_col)


def _swa_kernel(sink_ref, q_ref, kc_ref, kp_ref, vc_ref, vp_ref, o_ref, kbuf, vbuf, *, tq):
    i = pl.program_id(1)
    w = WINDOW
    kbuf[0:w, :] = kp_ref[...]
    kbuf[w:w + tq, :] = kc_ref[...]
    vbuf[0:w, :] = vp_ref[...]
    vbuf[w:w + tq, :] = vc_ref[...]
    lane = lax.broadcasted_iota(I32, (1, LANES), 1)
    lo = lane < D_HEAD
    qi = lax.broadcasted_iota(I32, (w, 2 * w), 0)
    ki = lax.broadcasted_iota(I32, (w, 2 * w), 1)
    band = (ki > qi) & (ki <= qi + w)
    band_first = band & (ki >= jnp.where(i > 0, 0, w))
    for r in range(tq // w):
        keys = kbuf[r * w:(r + 2) * w, :]
        vals = vbuf[r * w:(r + 2) * w, :]
        valid = band_first if r == 0 else band
        qs = []
        for half in range(2):
            for p in range(4):
                qg = q_ref[r * w:(r + 1) * w, p * LANES:(p + 1) * LANES]
                qs.append(jnp.where(lo if half == 0 else ~lo, qg, jnp.zeros_like(qg)))
        qstack = jnp.concatenate(qs, axis=0)
        s_all = lax.dot_general(qstack, keys, NT_DIMS, preferred_element_type=F32)
        ps = []
        for hb in range(8):
            sink = sink_ref[hb]
            s = jnp.where(valid, s_all[hb * w:(hb + 1) * w], NEG)
            m = jnp.maximum(jnp.max(s, axis=1, keepdims=True), sink)
            e = jnp.exp(s - m)
            den = jnp.sum(e, axis=1, keepdims=True) + jnp.exp(sink - m)
            ps.append((e / den).astype(BF16))
        pv = jnp.dot(jnp.concatenate(ps, axis=0), vals, preferred_element_type=F32)
        for p in range(4):
            og = jnp.where(lo, pv[p * w:(p + 1) * w], pv[(4 + p) * w:(5 + p) * w])
            o_ref[r * w:(r + 1) * w, p * LANES:(p + 1) * LANES] = og.astype(BF16)


def _swa_attention(qkv, sinks, b, s, tq):
    n = qkv.shape[0]
    nq = s // tq
    per = tq // WINDOW
    prev_map = lambda col: (lambda bi, i, sk: (jnp.maximum(bi * (s // WINDOW) + i * per - 1, 0), col))
    cur_map = lambda col: (lambda bi, i, sk: (bi * nq + i, col))
    return pl.pallas_call(
        functools.partial(_swa_kernel, tq=tq),
        grid_spec=pltpu.PrefetchScalarGridSpec(
            num_scalar_prefetch=1,
            grid=(b, nq),
            in_specs=[
                pl.BlockSpec((tq, 4 * LANES), lambda bi, i, sk: (bi * nq + i, 3)),
                pl.BlockSpec((tq, LANES), cur_map(16)),
                pl.BlockSpec((WINDOW, LANES), prev_map(16)),
                pl.BlockSpec((tq, LANES), cur_map(17)),
                pl.BlockSpec((WINDOW, LANES), prev_map(17)),
            ],
            out_specs=pl.BlockSpec((tq, 4 * LANES), lambda bi, i, sk: (bi * nq + i, 0)),
            scratch_shapes=[pltpu.VMEM((WINDOW + tq, LANES), BF16)] * 2,
        ),
        out_shape=jax.ShapeDtypeStruct((n, 4 * LANES), BF16),
        compiler_params=pltpu.CompilerParams(
            dimension_semantics=("arbitrary", "arbitrary"), vmem_limit_bytes=VMEM_LIMIT),
        name="swa",
    )(sinks, qkv, qkv, qkv, qkv, qkv)


def _memkv_kernel(m_ref, lnw_ref, w_ref, o_ref):
    a = _rms(m_ref[...], lnw_ref[...]).astype(BF16)
    o_ref[...] = jnp.dot(a, w_ref[...], preferred_element_type=F32).astype(BF16)


def _memkv(mem2, ln_w, wkv_bf, m_len):
    n, d = mem2.shape
    return pl.pallas_call(
        _memkv_kernel,
        grid=(n // m_len,),
        in_specs=[
            pl.BlockSpec((m_len, d), lambda i: (i, 0)),
            pl.BlockSpec((1, d), lambda i: (0, 0)),
            pl.BlockSpec((d, 2 * d), lambda i: (0, 0)),
        ],
        out_specs=pl.BlockSpec((m_len, 2 * d), lambda i: (i, 0)),
        out_shape=jax.ShapeDtypeStruct((n, 2 * d), BF16),
        compiler_params=pltpu.CompilerParams(
            dimension_semantics=("arbitrary",), vmem_limit_bytes=VMEM_LIMIT),
        name="memkv",
    )(mem2, ln_w, wkv_bf)


def _split_bf16(x):
    hi = x.astype(BF16)
    lo = (x - hi.astype(F32)).astype(BF16)
    return hi, lo


def _post_kernel(x_ref, od_ref, os_ref, wout_ref, lnc_ref, wq_ref, kv_ref, wo_ref, lnm_ref, wr_ref, br_ref,
                 h_ref, f_ref, code_ref, gate_ref, cnt_ref, base_scr, *, tm, d):
    first_step = (pl.program_id(0) == 0) & (pl.program_id(1) == 0)

    @pl.when(first_step)
    def _():
        base_scr[...] = jnp.zeros(base_scr.shape, F32)

    mix = jnp.concatenate([od_ref[...], os_ref[...]], axis=1)
    h1 = x_ref[...] + jnp.dot(mix, wout_ref[...], preferred_element_type=F32)

    c = _rms(h1, lnc_ref[...]).astype(BF16)
    dc = d // H_CROSS
    q = (jnp.dot(c, wq_ref[...], preferred_element_type=F32) * (dc ** -0.5)).astype(BF16)
    outs = []
    for hd in range(H_CROSS):
        k = kv_ref[:, hd * dc:(hd + 1) * dc]
        v = kv_ref[:, d + hd * dc:d + (hd + 1) * dc]
        s = lax.dot_general(q[:, hd * dc:(hd + 1) * dc], k, NT_DIMS, preferred_element_type=F32)
        e = jnp.exp(s - jnp.max(s, axis=1, keepdims=True))
        den = jnp.sum(e, axis=1, keepdims=True)
        outs.append((jnp.dot(e.astype(BF16), v, preferred_element_type=F32) / den).astype(BF16))
    o = jnp.concatenate(outs, axis=1)
    h2 = h1 + jnp.dot(o, wo_ref[...], preferred_element_type=F32)
    h_ref[...] = h2
    f = _rms(h2, lnm_ref[...])
    f_ref[...] = f

    f_hi, f_lo = _split_bf16(f)
    w_hi, w_lo = _split_bf16(wr_ref[...])
    lg = (lax.dot_general(w_hi, f_hi, NT_DIMS, preferred_element_type=F32)
          + lax.dot_general(w_hi, f_lo, NT_DIMS, preferred_element_type=F32)
          + lax.dot_general(w_lo, f_hi, NT_DIMS, preferred_element_type=F32)) + br_ref[...]

    gl = lg[0:N_GROUPS]
    gmax = jnp.max(gl, axis=0, keepdims=True)
    gidx = lax.broadcasted_iota(I32, gl.shape, 0)
    g_sel = jnp.min(jnp.where(gl == gmax, gidx, N_GROUPS), axis=0, keepdims=True)
    g_p = 1.0 / jnp.sum(jnp.exp(gl - gmax), axis=0, keepdims=True)

    e8 = jnp.zeros((E_PER_GROUP, tm), F32)
    for g in range(N_GROUPS):
        e8 = e8 + jnp.where(g_sel == g, lg[8 + g * E_PER_GROUP:8 + (g + 1) * E_PER_GROUP], 0.0)
    ex = jnp.exp(e8 - jnp.max(e8, axis=0, keepdims=True))
    ep = ex / jnp.sum(ex, axis=0, keepdims=True)
    idx8 = lax.broadcasted_iota(I32, ep.shape, 0)
    p1 = jnp.max(ep, axis=0, keepdims=True)
    i1 = jnp.min(jnp.where(ep == p1, idx8, E_PER_GROUP), axis=0, keepdims=True)
    ep2 = jnp.where(idx8 == i1, -1.0, ep)
    p2 = jnp.max(ep2, axis=0, keepdims=True)
    i2 = jnp.min(jnp.where(ep2 == p2, idx8, E_PER_GROUP), axis=0, keepdims=True)
    psum = p1 + p2
    gate_ref[0:1, :] = g_p * (p1 / psum)
    gate_ref[1:2, :] = g_p * (p2 / psum)
    eid1 = g_sel * E_PER_GROUP + i1
    eid2 = g_sel * E_PER_GROUP + i2

    e32 = lax.broadcasted_iota(I32, (N_EXPERTS, tm), 0)
    oh1 = (e32 == eid1).astype(F32)
    oh2 = (e32 == eid2).astype(F32)
    cnt = oh1 + oh2
    tr = lax.broadcasted_iota(I32, (tm, tm), 0)
    tc = lax.broadcasted_iota(I32, (tm, tm), 1)
    upper = jnp.where(tr < tc, 1.0, 0.0).astype(BF16)
    before = jnp.dot(cnt.astype(BF16), upper, preferred_element_type=F32) + base_scr[:, 0:1]
    rank1 = jnp.sum(oh1 * before, axis=0, keepdims=True).astype(I32)
    rank2 = jnp.sum(oh2 * before, axis=0, keepdims=True).astype(I32)
    code_ref[0:1, :] = eid1 * 65536 + rank1
    code_ref[1:2, :] = eid2 * 65536 + rank2
    base_new = base_scr[...] + jnp.sum(cnt, axis=1, keepdims=True)
    base_scr[...] = base_new
    cnt_ref[...] = base_new


def _post(x2, od, osw, wout_bf, lnc, wq_bf, kv, wo_bf, lnm, wr, br, b, s, tm, m_len):
    n, d = x2.shape
    nt = s // tm
    row = lambda bi, i: (bi * nt + i, 0)
    const = lambda bi, i: (0, 0)
    return pl.pallas_call(
        functools.partial(_post_kernel, tm=tm, d=d),
        grid=(b, nt),
        in_specs=[
            pl.BlockSpec((tm, d), row),
            pl.BlockSpec((tm, d // 2), row),
            pl.BlockSpec((tm, d // 2), row),
            pl.BlockSpec((d, d), const),
            pl.BlockSpec((1, d), const),
            pl.BlockSpec((d, d), const),
            pl.BlockSpec((m_len, 2 * d), lambda bi, i: (bi, 0)),
            pl.BlockSpec((d, d), const),
            pl.BlockSpec((1, d), const),
            pl.BlockSpec((8 + N_EXPERTS, d), const),
            pl.BlockSpec((8 + N_EXPERTS, 1), const),
        ],
        out_specs=[
            pl.BlockSpec((tm, d), row),
            pl.BlockSpec((tm, d), row),
            pl.BlockSpec((2, tm), lambda bi, i: (0, bi * nt + i)),
            pl.BlockSpec((2, tm), lambda bi, i: (0, bi * nt + i)),
            pl.BlockSpec((N_EXPERTS, LANES), const),
        ],
        out_shape=[
            jax.ShapeDtypeStruct((n, d), F32),
            jax.ShapeDtypeStruct((n, d), F32),
            jax.ShapeDtypeStruct((2, n), I32),
            jax.ShapeDtypeStruct((2, n), F32),
            jax.ShapeDtypeStruct((N_EXPERTS, LANES), F32),
        ],
        scratch_shapes=[pltpu.VMEM((N_EXPERTS, LANES), F32)],
        compiler_params=pltpu.CompilerParams(
            dimension_semantics=("arbitrary", "arbitrary"), vmem_limit_bytes=VMEM_LIMIT),
        name="post",
    )(x2, od, osw, wout_bf, lnc, wq_bf, kv, wo_bf, lnm, wr, br)


def _dispatch_kernel(code_ref, cnt_ref, f_ref, xs_in, xs_hbm, dest_ref, blk_ref, nact_ref,
                     pstart, sem, *, n, ch, bm, nblk):
    del xs_in
    i = pl.program_id(0)

    @pl.when(i == 0)
    def _():
        def per_expert(e, blk0):
            nb = (cnt_ref[e] + (bm - 1)) // bm
            pstart[e] = blk0 * bm

            def fill(j, carry):
                blk_ref[j] = e
                return carry

            lax.fori_loop(blk0, blk0 + nb, fill, 0)
            return blk0 + nb

        nact = lax.fori_loop(0, N_EXPERTS, per_expert, 0)
        nact_ref[0] = nact

        def tail(j, carry):
            blk_ref[j] = N_EXPERTS - 1
            return carry

        lax.fori_loop(nact, nblk, tail, 0)

    def row_copy(r, dst):
        return pltpu.make_async_copy(f_ref.at[pl.ds(r, 1)], xs_hbm.at[pl.ds(dst, 1)], sem)

    def body(r, carry):
        t = i * ch + r
        for k in range(2):
            c = code_ref[k * n + t]
            dst = pstart[c >> 16] + (c & 0xFFFF)
            dest_ref[k * n + t] = dst
            row_copy(r, dst).start()
        return carry

    lax.fori_loop(0, ch, body, 0)

    def drain(r, carry):
        row_copy(0, 0).wait()
        row_copy(0, 0).wait()
        return carry

    lax.fori_loop(0, ch, drain, 0)


def _dispatch(code_flat, cnt_i, f, xs0, bm, nblk, ch):
    n, d = f.shape
    smem = pl.BlockSpec(memory_space=pltpu.SMEM)
    hbm = pl.BlockSpec(memory_space=pl.ANY)
    return pl.pallas_call(
        functools.partial(_dispatch_kernel, n=n, ch=ch, bm=bm, nblk=nblk),
        grid_spec=pltpu.PrefetchScalarGridSpec(
            num_scalar_prefetch=2,
            grid=(n // ch,),
            in_specs=[pl.BlockSpec((ch, d), lambda i, code, cnt: (i, 0)), hbm],
            out_specs=[hbm, smem, smem, smem],
            scratch_shapes=[pltpu.SMEM((N_EXPERTS,), I32), pltpu.SemaphoreType.DMA],
        ),
        out_shape=[
            jax.ShapeDtypeStruct(xs0.shape, xs0.dtype),
            jax.ShapeDtypeStruct((2 * n,), I32),
            jax.ShapeDtypeStruct((nblk,), I32),
            jax.ShapeDtypeStruct((1,), I32),
        ],
        input_output_aliases={3: 0},
        compiler_params=pltpu.CompilerParams(dimension_semantics=("arbitrary",)),
        name="dispatch",
    )(code_flat, cnt_i, f, xs0)


def _expert_kernel(blk_ref, nact_ref, x_ref, wg_ref, wu_ref, wd_ref, y_ref):
    j = pl.program_id(0)

    @pl.when(j < nact_ref[0])
    def _():
        x = x_ref[...].astype(BF16)
        g = jnp.dot(x, wg_ref[...], preferred_element_type=F32)
        u = jnp.dot(x, wu_ref[...], preferred_element_type=F32)
        hdn = (g * jax.nn.sigmoid(g) * u).astype(BF16)
        y_ref[...] = jnp.dot(hdn, wd_ref[...], preferred_element_type=F32)

    @pl.when(j >= nact_ref[0])
    def _():
        y_ref[...] = jnp.zeros(y_ref.shape, y_ref.dtype)


def _experts(blk_e, nact, xs, wg_bf, wu_bf, wd_bf, bm):
    p, d = xs.shape
    dff = wg_bf.shape[-1]
    nblk = p // bm
    rowmap = lambda j, blk, na: (jnp.minimum(j, na[0] - 1), 0)
    wmap = lambda j, blk, na: (blk[jnp.minimum(j, na[0] - 1)], 0, 0)
    return pl.pallas_call(
        _expert_kernel,
        grid_spec=pltpu.PrefetchScalarGridSpec(
            num_scalar_prefetch=2,
            grid=(nblk,),
            in_specs=[
                pl.BlockSpec((bm, d), rowmap),
                pl.BlockSpec((None, d, dff), wmap),
                pl.BlockSpec((None, d, dff), wmap),
                pl.BlockSpec((None, dff, d), wmap),
            ],
            out_specs=pl.BlockSpec((bm, d), lambda j, blk, na: (j, 0)),
        ),
        out_shape=jax.ShapeDtypeStruct((p, d), F32),
        compiler_params=pltpu.CompilerParams(
            dimension_semantics=("arbitrary",), vmem_limit_bytes=VMEM_LIMIT),
        name="experts",
    )(blk_e, nact, xs, wg_bf, wu_bf, wd_bf)


def _combine_kernel(dest_ref, h_ref, gate_ref, lnf_ref, ys_hbm, o_ref, ybuf, sem, *, n, tm):
    i = pl.program_id(0)
    nsteps = pl.num_programs(0)
    slot = i % 2

    def row_copy(src, k, sl, r):
        return pltpu.make_async_copy(ys_hbm.at[pl.ds(src, 1)], ybuf.at[sl, k, pl.ds(r, 1)], sem.at[sl])

    def issue(step, sl):
        def body(r, carry):
            for k in range(2):
                row_copy(dest_ref[k * n + step * tm + r], k, sl, r).start()
            return carry

        lax.fori_loop(0, tm, body, 0)

    @pl.when(i == 0)
    def _():
        issue(0, 0)

    @pl.when(i + 1 < nsteps)
    def _():
        issue(i + 1, 1 - slot)

    def drain(r, carry):
        for k in range(2):
            row_copy(0, k, slot, r).wait()
        return carry

    lax.fori_loop(0, tm, drain, 0)

    gpad = jnp.concatenate([gate_ref[...], jnp.zeros((LANES - 2, tm), F32)], axis=0)
    gt = gpad.T
    h = h_ref[...] + gt[:, 0:1] * ybuf[slot, 0] + gt[:, 1:2] * ybuf[slot, 1]
    o_ref[...] = _rms(h, lnf_ref[...])


def _combine(dest, h2, gates, lnf, ys, tm):
    n, d = h2.shape
    return pl.pallas_call(
        functools.partial(_combine_kernel, n=n, tm=tm),
        grid_spec=pltpu.PrefetchScalarGridSpec(
            num_scalar_prefetch=1,
            grid=(n // tm,),
            in_specs=[
                pl.BlockSpec((tm, d), lambda i, dst: (i, 0)),
                pl.BlockSpec((2, tm), lambda i, dst: (0, i)),
                pl.BlockSpec((1, d), lambda i, dst: (0, 0)),
                pl.BlockSpec(memory_space=pl.ANY),
            ],
            out_specs=pl.BlockSpec((tm, d), lambda i, dst: (i, 0)),
            scratch_shapes=[pltpu.VMEM((2, 2, tm, d), F32), pltpu.SemaphoreType.DMA((2,))],
        ),
        out_shape=jax.ShapeDtypeStruct((n, d), F32),
        compiler_params=pltpu.CompilerParams(
            dimension_semantics=("arbitrary",), vmem_limit_bytes=VMEM_LIMIT),
        name="combine",
    )(dest, h2, gates, lnf, ys)


def _swa_head_perm():
    cols = []
    for p in range(4):
        for half in range(2):
            head = half * 4 + p
            cols.extend(range(head * D_HEAD, (head + 1) * D_HEAD))
    return jnp.asarray(cols, dtype=I32)


def kernel(x, mem, positions, ln_mix_w, w_in, lambda_q1, lambda_k1, lambda_q2, lambda_k2, subln_w, sinks,
           w_out, ln_cross_w, ln_mem_w, wq_cross, wkv_cross, wo_cross, ln_moe_w, w_group, b_group,
           w_expert, b_expert, w_gate, w_up, w_down, ln_final_w):
    b, s, d = x.shape
    m_len = mem.shape[1]
    n = b * s
    assert w_in.shape[0] == 1 and d == 1024 and n <= 65536
    lambda_init = 0.8 - 0.6 * math.exp(-0.3 * 0)

    tm = 512
    t_attn = 256
    tk_attn = 1024
    bm = 256
    ch = tm

    x2 = x.reshape(n, d)
    pos2 = positions.reshape(n, 1).astype(I32)
    inv_freq = jnp.exp(-math.log(ROPE_THETA) * jnp.arange(0, D_HEAD, 2, dtype=F32) / D_HEAD)
    inv128 = jnp.tile(inv_freq, LANES // (D_HEAD // 2)).reshape(1, LANES)

    perm = _swa_head_perm()
    sq0 = 3 * 512
    w_in_l = w_in[0]
    w_in_p = jnp.concatenate([w_in_l[:, :sq0], w_in_l[:, sq0:sq0 + 512][:, perm], w_in_l[:, sq0 + 512:]], axis=1)
    w_out_l = w_out[0]
    w_out_p = jnp.concatenate([w_out_l[:512], w_out_l[512:][perm]], axis=0)
    sinks_p = sinks[0].reshape(2, 4).reshape(-1)

    qkv = _inproj(x2, ln_mix_w[0].reshape(1, d), w_in_p.astype(BF16), pos2, inv128, tm)
    o_diff = _diff_attention(qkv, lambda_q1[0].reshape(1, -1), lambda_k1[0].reshape(1, -1),
                             lambda_q2[0].reshape(1, -1), lambda_k2[0].reshape(1, -1),
                             subln_w[0].reshape(-1, 1), b, s, t_attn, min(tk_attn, s), lambda_init)
    o_swa = _swa_attention(qkv, sinks_p.astype(F32), b, s, tm)
    kv = _memkv(mem.reshape(b * m_len, d), ln_mem_w[0].reshape(1, d), wkv_cross[0].astype(BF16), m_len)

    wr = jnp.concatenate([w_group[0].T, jnp.zeros((8 - N_GROUPS, d), F32), w_expert[0].T], axis=0)
    br = jnp.concatenate([b_group[0], jnp.zeros((8 - N_GROUPS,), F32), b_expert[0]]).reshape(-1, 1)
    h2, f, code, gates, counts = _post(
        x2, o_diff, o_swa, w_out_p.astype(BF16), ln_cross_w[0].reshape(1, d), wq_cross[0].astype(BF16), kv,
        wo_cross[0].astype(BF16), ln_moe_w[0].reshape(1, d), wr, br, b, s, tm, m_len)

    nblk = (2 * n + N_EXPERTS * (bm - 1) + bm - 1) // bm
    xs0 = jnp.zeros((nblk * bm, d), F32)
    cnt_i = counts[:, 0].astype(I32)
    xs, dest, blk_e, nact = _dispatch(code.reshape(-1), cnt_i, f, xs0, bm, nblk, ch)
    ys = _experts(blk_e, nact, xs, w_gate[0].astype(BF16), w_up[0].astype(BF16), w_down[0].astype(BF16), bm)
    out = _combine(dest, h2, gates, ln_final_w.reshape(1, d), ys, tm)
    return out.reshape(b, s, d)
```

```python
import functools
import math

import jax
import jax.numpy as jnp
from jax import lax
from jax.experimental import pallas as pl
from jax.experimental.pallas import tpu as pltpu

F32 = jnp.float32
BF16 = jnp.bfloat16
I32 = jnp.int32

D_HEAD = 64
ROPE_THETA = 10000.0
H_DIFF = 4
N_Q_SWA = 8
N_KV_SWA = 2
WINDOW = 128
H_CROSS = 4
N_GROUPS = 4
E_PER_GROUP = 8
N_EXPERTS = N_GROUPS * E_PER_GROUP
EPS = 1e-6
LANES = 128
SUBLANES = 8
IN_W = 2304
NEG = -1e30

VMEM_LIMIT = 56 * 1024 * 1024

NT_DIMS = (((1,), (1,)), ((), ()))


def _rms(x, w):
    ms = jnp.mean(x * x, axis=-1, keepdims=True)
    return x * lax.rsqrt(ms + EPS) * w


def _inproj_kernel(x_ref, lnw_ref, w_ref, pos_ref, inv_ref, o_ref):
    a = _rms(x_ref[...], lnw_ref[...]).astype(BF16)
    ang = pos_ref[...].astype(F32) * inv_ref[...]
    cos = jnp.cos(ang)
    sin = jnp.sin(ang)
    lane = lax.broadcasted_iota(I32, (1, LANES), 1)
    first = (lane % D_HEAD) < (D_HEAD // 2)
    sin_signed = jnp.where(first, -sin, sin)
    n_chunks = IN_W // 256
    for c in range(n_chunks):
        p = jnp.dot(a, w_ref[:, c * 256:(c + 1) * 256], preferred_element_type=F32)
        for hh in range(2):
            g = c * 2 + hh
            xg = p[:, hh * LANES:(hh + 1) * LANES]
            is_v = (8 <= g < 12) or g == 17
            if not is_v:
                partner = jnp.where(first, pltpu.roll(xg, 96, 1), pltpu.roll(xg, 32, 1))
                xg = xg * cos + partner * sin_signed
                if g < 4:
                    xg = xg * (D_HEAD ** -0.5 * math.log2(math.e))
                elif 12 <= g < 16:
                    xg = xg * (D_HEAD ** -0.5)
            o_ref[:, g * LANES:(g + 1) * LANES] = xg.astype(BF16)


def _inproj(x2, ln_w, w_in_bf, pos2, inv128, tm):
    n, d = x2.shape
    return pl.pallas_call(
        _inproj_kernel,
        grid=(n // tm,),
        in_specs=[
            pl.BlockSpec((tm, d), lambda i: (i, 0)),
            pl.BlockSpec((1, d), lambda i: (0, 0)),
            pl.BlockSpec((d, IN_W), lambda i: (0, 0)),
            pl.BlockSpec((tm, 1), lambda i: (i, 0)),
            pl.BlockSpec((1, LANES), lambda i: (0, 0)),
        ],
        out_specs=pl.BlockSpec((tm, IN_W), lambda i: (i, 0)),
        out_shape=jax.ShapeDtypeStruct((n, IN_W), BF16),
        compiler_params=pltpu.CompilerParams(
            dimension_semantics=("arbitrary",), vmem_limit_bytes=VMEM_LIMIT),
        name="inproj",
    )(x2, ln_w, w_in_bf, pos2, inv128)


def _diff_kernel(q_ref, k_ref, v_ref, lq1_ref, lk1_ref, lq2_ref, lk2_ref, sw_ref, o_ref,
                 vt_scr, acc_scr, *, t, tk, lambda_init):
    i = pl.program_id(2)

    @pl.when(i == 0)
    def _():
        for c in range(vt_scr.shape[0]):
            for r in range(tk // t):
                rows = slice(c * tk + r * t, c * tk + (r + 1) * t)
                vt_scr[c, :, r * t:(r + 1) * t] = v_ref[rows, :].astype(F32).T.astype(BF16)

    lane = lax.broadcasted_iota(I32, (1, LANES), 1)
    q = q_ref[...]
    zero = jnp.zeros_like(q)
    q2 = jnp.concatenate([jnp.where(lane < D_HEAD, q, zero), jnp.where(lane >= D_HEAD, q, zero)], axis=0)
    acc_scr[...] = jnp.zeros(acc_scr.shape, F32)
    n_full = (i * t) // tk

    def step(j, carry, masked):
        m_prev, l_prev = carry
        k = k_ref[pl.ds(pl.multiple_of(j * tk, tk), tk), :]
        st = lax.dot_general(k, q2, NT_DIMS, preferred_element_type=F32)
        if masked:
            key = lax.broadcasted_iota(I32, (tk, 2 * t), 0) + j * tk
            col = lax.broadcasted_iota(I32, (tk, 2 * t), 1)
            qry = jnp.where(col >= t, col - t, col) + i * t
            st = jnp.where(key <= qry, st, NEG)
        m_new = jnp.maximum(m_prev, jnp.max(st, axis=0, keepdims=True))
        p = jnp.exp2(st - m_new)
        alpha = jnp.exp2(m_prev - m_new)
        l_new = alpha * l_prev + jnp.sum(p, axis=0, keepdims=True)
        acc_scr[...] = alpha * acc_scr[...] + jnp.dot(vt_scr[j], p.astype(BF16), preferred_element_type=F32)
        return m_new, l_new

    init = (jnp.full((1, 2 * t), NEG, F32), jnp.zeros((1, 2 * t), F32))
    carry = lax.fori_loop(0, n_full, lambda j, cr: step(j, cr, False), init)
    _, l = step(n_full, carry, True)

    lam = (jnp.exp(jnp.sum(lq1_ref[...] * lk1_ref[...], axis=1, keepdims=True))
           - jnp.exp(jnp.sum(lq2_ref[...] * lk2_ref[...], axis=1, keepdims=True))
           + lambda_init)
    on = acc_scr[...] / l
    o = on[:, :t] - lam * on[:, t:]
    ms = jnp.mean(o * o, axis=0, keepdims=True)
    o = o * lax.rsqrt(ms + EPS) * sw_ref[...] * (1.0 - lambda_init)
    o_ref[...] = o.T.astype(BF16)


def _diff_attention(qkv, lq1, lk1, lq2, lk2, subln_col, b, s, t, tk, lambda_init):
    n = qkv.shape[0]
    nq = s // t
    small = pl.BlockSpec((1, D_HEAD), lambda bi, h, i: (0, 0))
    return pl.pallas_call(
        functools.partial(_diff_kernel, t=t, tk=tk, lambda_init=lambda_init),
        grid=(b, H_DIFF, nq),
        in_specs=[
            pl.BlockSpec((t, LANES), lambda bi, h, i: (bi * nq + i, h)),
            pl.BlockSpec((s, LANES), lambda bi, h, i: (bi, 4 + h)),
            pl.BlockSpec((s, LANES), lambda bi, h, i: (bi, 8 + h)),
            small, small, small, small,
            pl.BlockSpec((LANES, 1), lambda bi, h, i: (0, 0)),
        ],
        out_specs=pl.BlockSpec((t, LANES), lambda bi, h, i: (bi * nq + i, h)),
        out_shape=jax.ShapeDtypeStruct((n, H_DIFF * LANES), BF16),
        scratch_shapes=[pltpu.VMEM((s // tk, LANES, tk), BF16), pltpu.VMEM((LANES, 2 * t), F32)],
        compiler_params=pltpu.CompilerParams(
            dimension_semantics=("arbitrary", "arbitrary", "arbitrary"), vmem_limit_bytes=VMEM_LIMIT),
        name="diffattn",
    )(qkv, qkv, qkv, lq1, lk1, lq2, lk2, subln_col)


def _swa_kernel(sink_ref, q_ref, kc_ref, kp_ref, vc_ref, vp_ref, o_ref, kbuf, vbuf, *, tq):
    i = pl.program_id(1)
    w = WINDOW
    kbuf[0:w, :] = kp_ref[...]
    kbuf[w:w + tq, :] = kc_ref[...]
    vbuf[0:w, :] = vp_ref[...]
    vbuf[w:w + tq, :] = vc_ref[...]
    lane = lax.broadcasted_iota(I32, (1, LANES), 1)
    lo = lane < D_HEAD
    qi = lax.broadcasted_iota(I32, (w, 2 * w), 0)
    ki = lax.broadcasted_iota(I32, (w, 2 * w), 1)
    band = (ki > qi) & (ki <= qi + w)
    band_first = band & (ki >= jnp.where(i > 0, 0, w))
    for r in range(tq // w):
        keys = kbuf[r * w:(r + 2) * w, :]
        vals = vbuf[r * w:(r + 2) * w, :]
        valid = band_first if r == 0 else band
        qs = []
        for half in range(2):
            for p in range(4):
                qg = q_ref[r * w:(r + 1) * w, p * LANES:(p + 1) * LANES]
                qs.append(jnp.where(lo if half == 0 else ~lo, qg, jnp.zeros_like(qg)))
        qstack = jnp.concatenate(qs, axis=0)
        s_all = lax.dot_general(qstack, keys, NT_DIMS, preferred_element_type=F32)
        ps = []
        for hb in range(8):
            sink = sink_ref[hb]
            s = jnp.where(valid, s_all[hb * w:(hb + 1) * w], NEG)
            m = jnp.maximum(jnp.max(s, axis=1, keepdims=True), sink)
            e = jnp.exp(s - m)
            den = jnp.sum(e, axis=1, keepdims=True) + jnp.exp(sink - m)
            ps.append((e / den).astype(BF16))
        pv = jnp.dot(jnp.concatenate(ps, axis=0), vals, preferred_element_type=F32)
        for p in range(4):
            og = jnp.where(lo, pv[p * w:(p + 1) * w], pv[(4 + p) * w:(5 + p) * w])
            o_ref[r * w:(r + 1) * w, p * LANES:(p + 1) * LANES] = og.astype(BF16)


def _swa_attention(qkv, sinks, b, s, tq):
    n = qkv.shape[0]
    nq = s // tq
    per = tq // WINDOW
    prev_map = lambda col: (lambda bi, i, sk: (jnp.maximum(bi * (s // WINDOW) + i * per - 1, 0), col))
    cur_map = lambda col: (lambda bi, i, sk: (bi * nq + i, col))
    return pl.pallas_call(
        functools.partial(_swa_kernel, tq=tq),
        grid_spec=pltpu.PrefetchScalarGridSpec(
            num_scalar_prefetch=1,
            grid=(b, nq),
            in_specs=[
                pl.BlockSpec((tq, 4 * LANES), lambda bi, i, sk: (bi * nq + i, 3)),
                pl.BlockSpec((tq, LANES), cur_map(16)),
                pl.BlockSpec((WINDOW, LANES), prev_map(16)),
                pl.BlockSpec((tq, LANES), cur_map(17)),
                pl.BlockSpec((WINDOW, LANES), prev_map(17)),
            ],
            out_specs=pl.BlockSpec((tq, 4 * LANES), lambda bi, i, sk: (bi * nq + i, 0)),
            scratch_shapes=[pltpu.VMEM((WINDOW + tq, LANES), BF16)] * 2,
        ),
        out_shape=jax.ShapeDtypeStruct((n, 4 * LANES), BF16),
        compiler_params=pltpu.CompilerParams(
            dimension_semantics=("arbitrary", "arbitrary"), vmem_limit_bytes=VMEM_LIMIT),
        name="swa",
    )(sinks, qkv, qkv, qkv, qkv, qkv)


def _memkv_kernel(m_ref, lnw_ref, w_ref, o_ref):
    a = _rms(m_ref[...], lnw_ref[...]).astype(BF16)
    o_ref[...] = jnp.dot(a, w_ref[...], preferred_element_type=F32).astype(BF16)


def _memkv(mem2, ln_w, wkv_bf, m_len):
    n, d = mem2.shape
    return pl.pallas_call(
        _memkv_kernel,
        grid=(n // m_len,),
        in_specs=[
            pl.BlockSpec((m_len, d), lambda i: (i, 0)),
            pl.BlockSpec((1, d), lambda i: (0, 0)),
            pl.BlockSpec((d, 2 * d), lambda i: (0, 0)),
        ],
        out_specs=pl.BlockSpec((m_len, 2 * d), lambda i: (i, 0)),
        out_shape=jax.ShapeDtypeStruct((n, 2 * d), BF16),
        compiler_params=pltpu.CompilerParams(
            dimension_semantics=("arbitrary",), vmem_limit_bytes=VMEM_LIMIT),
        name="memkv",
    )(mem2, ln_w, wkv_bf)


def _split_bf16(x):
    hi = x.astype(BF16)
    lo = (x - hi.astype(F32)).astype(BF16)
    return hi, lo


def _post_kernel(x_ref, od_ref, os_ref, wout_ref, lnc_ref, wq_ref, kv_ref, wo_ref, lnm_ref, wr_ref, br_ref,
                 h_ref, f_ref, lpos_ref, gate_ref, tcnt_ref, tbase_ref, base_scr, *, tm, d):
    first_step = (pl.program_id(0) == 0) & (pl.program_id(1) == 0)

    @pl.when(first_step)
    def _():
        base_scr[...] = jnp.zeros(base_scr.shape, F32)

    mix = jnp.concatenate([od_ref[...], os_ref[...]], axis=1)
    h1 = x_ref[...] + jnp.dot(mix, wout_ref[...], preferred_element_type=F32)

    c = _rms(h1, lnc_ref[...]).astype(BF16)
    dc = d // H_CROSS
    q = (jnp.dot(c, wq_ref[...], preferred_element_type=F32) * (dc ** -0.5)).astype(BF16)
    outs = []
    for hd in range(H_CROSS):
        k = kv_ref[:, hd * dc:(hd + 1) * dc]
        v = kv_ref[:, d + hd * dc:d + (hd + 1) * dc]
        s = lax.dot_general(q[:, hd * dc:(hd + 1) * dc], k, NT_DIMS, preferred_element_type=F32)
        e = jnp.exp(s - jnp.max(s, axis=1, keepdims=True))
        den = jnp.sum(e, axis=1, keepdims=True)
        outs.append((jnp.dot(e.astype(BF16), v, preferred_element_type=F32) / den).astype(BF16))
    o = jnp.concatenate(outs, axis=1)
    h2 = h1 + jnp.dot(o, wo_ref[...], preferred_element_type=F32)
    h_ref[...] = h2
    f = _rms(h2, lnm_ref[...])
    f_ref[...] = f.astype(BF16)

    f_hi, f_lo = _split_bf16(f)
    w_hi, w_lo = _split_bf16(wr_ref[...])
    lg = (lax.dot_general(w_hi, f_hi, NT_DIMS, preferred_element_type=F32)
          + lax.dot_general(w_hi, f_lo, NT_DIMS, preferred_element_type=F32)
          + lax.dot_general(w_lo, f_hi, NT_DIMS, preferred_element_type=F32)) + br_ref[...]

    gl = lg[0:N_GROUPS]
    gmax = jnp.max(gl, axis=0, keepdims=True)
    gidx = lax.broadcasted_iota(I32, gl.shape, 0)
    g_sel = jnp.min(jnp.where(gl == gmax, gidx, N_GROUPS), axis=0, keepdims=True)
    g_p = 1.0 / jnp.sum(jnp.exp(gl - gmax), axis=0, keepdims=True)

    e8 = jnp.zeros((E_PER_GROUP, tm), F32)
    for g in range(N_GROUPS):
        e8 = e8 + jnp.where(g_sel == g, lg[8 + g * E_PER_GROUP:8 + (g + 1) * E_PER_GROUP], 0.0)
    ex = jnp.exp(e8 - jnp.max(e8, axis=0, keepdims=True))
    ep = ex / jnp.sum(ex, axis=0, keepdims=True)
    idx8 = lax.broadcasted_iota(I32, ep.shape, 0)
    p1 = jnp.max(ep, axis=0, keepdims=True)
    i1 = jnp.min(jnp.where(ep == p1, idx8, E_PER_GROUP), axis=0, keepdims=True)
    ep2 = jnp.where(idx8 == i1, -1.0, ep)
    p2 = jnp.max(ep2, axis=0, keepdims=True)
    i2 = jnp.min(jnp.where(ep2 == p2, idx8, E_PER_GROUP), axis=0, keepdims=True)
    psum = p1 + p2
    gate_ref[0:1, :] = g_p * (p1 / psum)
    gate_ref[1:2, :] = g_p * (p2 / psum)
    eid1 = g_sel * E_PER_GROUP + i1
    eid2 = g_sel * E_PER_GROUP + i2

    e32 = lax.broadcasted_iota(I32, (N_EXPERTS, tm), 0)
    oh1 = (e32 == eid1).astype(F32)
    oh2 = (e32 == eid2).astype(F32)
    cnt = oh1 + oh2
    tr = lax.broadcasted_iota(I32, (tm, tm), 0)
    tc = lax.broadcasted_iota(I32, (tm, tm), 1)
    upper = jnp.where(tr < tc, 1.0, 0.0).astype(BF16)
    earlier = jnp.dot(cnt.astype(BF16), upper, preferred_element_type=F32)
    seg = jnp.sum(cnt, axis=1, keepdims=True)
    seg = jnp.floor((seg + (SUBLANES - 1)) * (1.0 / SUBLANES)) * SUBLANES
    seg = jnp.broadcast_to(seg, (N_EXPERTS, LANES))
    er = lax.broadcasted_iota(I32, (N_EXPERTS, N_EXPERTS), 0)
    ec = lax.broadcasted_iota(I32, (N_EXPERTS, N_EXPERTS), 1)
    lower = jnp.where(ec < er, 1.0, 0.0).astype(BF16)
    start = jnp.dot(lower, seg.astype(BF16), preferred_element_type=F32)[:, 0:1]
    where = earlier + start
    lpos_ref[0:1, :] = jnp.sum(oh1 * where, axis=0, keepdims=True).astype(I32)
    lpos_ref[1:2, :] = jnp.sum(oh2 * where, axis=0, keepdims=True).astype(I32)
    tcnt_ref[...] = seg
    tbase_ref[...] = base_scr[...]
    base_scr[...] = base_scr[...] + seg


def _post(x2, od, osw, wout_bf, lnc, wq_bf, kv, wo_bf, lnm, wr, br, b, s, tm, m_len):
    n, d = x2.shape
    nt = s // tm
    row = lambda bi, i: (bi * nt + i, 0)
    const = lambda bi, i: (0, 0)
    return pl.pallas_call(
        functools.partial(_post_kernel, tm=tm, d=d),
        grid=(b, nt),
        in_specs=[
            pl.BlockSpec((tm, d), row),
            pl.BlockSpec((tm, d // 2), row),
            pl.BlockSpec((tm, d // 2), row),
            pl.BlockSpec((d, d), const),
            pl.BlockSpec((1, d), const),
            pl.BlockSpec((d, d), const),
            pl.BlockSpec((m_len, 2 * d), lambda bi, i: (bi, 0)),
            pl.BlockSpec((d, d), const),
            pl.BlockSpec((1, d), const),
            pl.BlockSpec((8 + N_EXPERTS, d), const),
            pl.BlockSpec((8 + N_EXPERTS, 1), const),
        ],
        out_specs=[
            pl.BlockSpec((tm, d), row),
            pl.BlockSpec((tm, d), row),
            pl.BlockSpec((2, tm), lambda bi, i: (0, bi * nt + i)),
            pl.BlockSpec((2, tm), lambda bi, i: (0, bi * nt + i)),
            pl.BlockSpec((None, N_EXPERTS, LANES), lambda bi, i: (bi * nt + i, 0, 0)),
            pl.BlockSpec((None, N_EXPERTS, LANES), lambda bi, i: (bi * nt + i, 0, 0)),
        ],
        out_shape=[
            jax.ShapeDtypeStruct((n, d), F32),
            jax.ShapeDtypeStruct((n, d), BF16),
            jax.ShapeDtypeStruct((2, n), I32),
            jax.ShapeDtypeStruct((2, n), F32),
            jax.ShapeDtypeStruct((n // tm, N_EXPERTS, LANES), F32),
            jax.ShapeDtypeStruct((n // tm, N_EXPERTS, LANES), F32),
        ],
        scratch_shapes=[pltpu.VMEM((N_EXPERTS, LANES), F32)],
        compiler_params=pltpu.CompilerParams(
            dimension_semantics=("arbitrary", "arbitrary"), vmem_limit_bytes=VMEM_LIMIT),
        name="post",
    )(x2, od, osw, wout_bf, lnc, wq_bf, kv, wo_bf, lnm, wr, br)


def _local_rows(tm):
    return 2 * tm + SUBLANES * N_EXPERTS


def _segment_copies(tile, tcnt_ref, tbase_ref, pstart_ref, make, act, max_len):
    def per_expert(e, local0):
        ln = tcnt_ref[tile * N_EXPERTS + e]
        glob0 = pstart_ref[e] + tbase_ref[tile * N_EXPERTS + e]
        for k in reversed(range((max_len // SUBLANES).bit_length())):
            size = SUBLANES << k

            @pl.when((ln & size) != 0)
            def _():
                off = (ln // (2 * size)) * (2 * size)
                copy = make(pl.multiple_of(local0 + off, SUBLANES), pl.multiple_of(glob0 + off, SUBLANES), size)
                if act == "start":
                    copy.start()
                else:
                    copy.wait()

        return local0 + ln

    return lax.fori_loop(0, N_EXPERTS, per_expert, 0)


def _dispatch_kernel(tcnt_ref, tbase_ref, tot_ref, f_ref, lpos_ref, xs_in, xs_hbm, pstart_ref, blk_ref, nact_ref,
                     loc_scr, sem, *, tm, bm, nblk):
    del xs_in
    i = pl.program_id(0)
    nt = pl.num_programs(0)
    slot = i % 2

    @pl.when(i == 0)
    def _():
        def per_expert(e, blk0):
            nb = (tot_ref[e] + (bm - 1)) // bm
            pstart_ref[e] = blk0 * bm

            def fill(j, carry):
                blk_ref[j] = e
                return carry

            lax.fori_loop(blk0, blk0 + nb, fill, 0)
            return blk0 + nb

        nact = lax.fori_loop(0, N_EXPERTS, per_expert, 0)
        nact_ref[0] = nact

        def tail(j, carry):
            blk_ref[j] = N_EXPERTS - 1
            return carry

        lax.fori_loop(nact, nblk, tail, 0)

    def copies(tile, sl, act):
        def make(lrow, grow, size):
            return pltpu.make_async_copy(loc_scr.at[sl, pl.ds(lrow, size)], xs_hbm.at[pl.ds(grow, size)], sem.at[sl])

        _segment_copies(tile, tcnt_ref, tbase_ref, pstart_ref, make, act, _local_rows(tm))

    @pl.when(i >= 2)
    def _():
        copies(i - 2, slot, "wait")

    pos = lax.broadcasted_iota(I32, (_local_rows(tm), tm), 0)
    onehot = jnp.where(pos == lpos_ref[0:1, :], 1.0, jnp.where(pos == lpos_ref[1:2, :], 1.0, 0.0)).astype(BF16)
    loc_scr[slot] = jnp.dot(onehot, f_ref[...], preferred_element_type=F32)
    copies(i, slot, "start")

    @pl.when(i == nt - 1)
    def _():
        @pl.when(i >= 1)
        def _():
            copies(i - 1, 1 - slot, "wait")

        copies(i, slot, "wait")


def _dispatch(tcnt_i, tbase_i, tot_i, f, lpos, xs0, tm, bm, nblk):
    n, d = f.shape
    smem = pl.BlockSpec(memory_space=pltpu.SMEM)
    hbm = pl.BlockSpec(memory_space=pl.ANY)
    return pl.pallas_call(
        functools.partial(_dispatch_kernel, tm=tm, bm=bm, nblk=nblk),
        grid_spec=pltpu.PrefetchScalarGridSpec(
            num_scalar_prefetch=3,
            grid=(n // tm,),
            in_specs=[
                pl.BlockSpec((tm, d), lambda i, *_: (i, 0)),
                pl.BlockSpec((2, tm), lambda i, *_: (0, i)),
                hbm,
            ],
            out_specs=[hbm, smem, smem, smem],
            scratch_shapes=[pltpu.VMEM((2, _local_rows(tm), d), F32), pltpu.SemaphoreType.DMA((2,))],
        ),
        out_shape=[
            jax.ShapeDtypeStruct(xs0.shape, xs0.dtype),
            jax.ShapeDtypeStruct((N_EXPERTS,), I32),
            jax.ShapeDtypeStruct((nblk,), I32),
            jax.ShapeDtypeStruct((1,), I32),
        ],
        input_output_aliases={5: 0},
        compiler_params=pltpu.CompilerParams(
            dimension_semantics=("arbitrary",), vmem_limit_bytes=VMEM_LIMIT),
        name="dispatch",
    )(tcnt_i, tbase_i, tot_i, f, lpos, xs0)


def _expert_kernel(blk_ref, nact_ref, x_ref, wg_ref, wu_ref, wd_ref, y_ref):
    j = pl.program_id(0)

    @pl.when(j < nact_ref[0])
    def _():
        x = x_ref[...].astype(BF16)
        g = jnp.dot(x, wg_ref[...], preferred_element_type=F32)
        u = jnp.dot(x, wu_ref[...], preferred_element_type=F32)
        hdn = (g * jax.nn.sigmoid(g) * u).astype(BF16)
        y_ref[...] = jnp.dot(hdn, wd_ref[...], preferred_element_type=F32)

    @pl.when(j >= nact_ref[0])
    def _():
        y_ref[...] = jnp.zeros(y_ref.shape, y_ref.dtype)


def _experts(blk_e, nact, xs, wg_bf, wu_bf, wd_bf, bm):
    p, d = xs.shape
    dff = wg_bf.shape[-1]
    nblk = p // bm
    rowmap = lambda j, blk, na: (jnp.minimum(j, na[0] - 1), 0)
    wmap = lambda j, blk, na: (blk[jnp.minimum(j, na[0] - 1)], 0, 0)
    return pl.pallas_call(
        _expert_kernel,
        grid_spec=pltpu.PrefetchScalarGridSpec(
            num_scalar_prefetch=2,
            grid=(nblk,),
            in_specs=[
                pl.BlockSpec((bm, d), rowmap),
                pl.BlockSpec((None, d, dff), wmap),
                pl.BlockSpec((None, d, dff), wmap),
                pl.BlockSpec((None, dff, d), wmap),
            ],
            out_specs=pl.BlockSpec((bm, d), lambda j, blk, na: (j, 0)),
        ),
        out_shape=jax.ShapeDtypeStruct((p, d), F32),
        compiler_params=pltpu.CompilerParams(
            dimension_semantics=("arbitrary",), vmem_limit_bytes=VMEM_LIMIT),
        name="experts",
    )(blk_e, nact, xs, wg_bf, wu_bf, wd_bf)


def _combine_kernel(tcnt_ref, tbase_ref, pstart_ref, h_ref, lpos_ref, gate_ref, lnf_ref, ys_hbm, o_ref,
                    loc_scr, sem, *, tm):
    i = pl.program_id(0)
    nt = pl.num_programs(0)
    slot = i % 2

    def copies(tile, sl, act):
        def make(lrow, grow, size):
            return pltpu.make_async_copy(ys_hbm.at[pl.ds(grow, size)], loc_scr.at[sl, pl.ds(lrow, size)], sem.at[sl])

        return _segment_copies(tile, tcnt_ref, tbase_ref, pstart_ref, make, act, _local_rows(tm))

    @pl.when(i == 0)
    def _():
        copies(0, 0, "start")

    @pl.when(i + 1 < nt)
    def _():
        copies(i + 1, 1 - slot, "start")

    used = copies(i, slot, "wait")
    row = lax.broadcasted_iota(I32, (_local_rows(tm), 1), 0)
    ys = jnp.where(row < used, loc_scr[slot], 0.0).astype(BF16)

    r8 = lax.broadcasted_iota(I32, (8, tm), 0)
    lp = lpos_ref[...].astype(F32)
    top = jnp.where(r8 == 0, lp[0:1], jnp.where(r8 == 1, lp[1:2], jnp.where(
        r8 == 2, gate_ref[0:1, :], jnp.where(r8 == 3, gate_ref[1:2, :], 0.0))))
    cols = jnp.concatenate([top, jnp.zeros((LANES - 8, tm), F32)], axis=0).T
    pos = lax.broadcasted_iota(I32, (tm, _local_rows(tm)), 1)
    acc = h_ref[...]
    for k in range(2):
        sel = jnp.where(pos == cols[:, k:k + 1].astype(I32), 1.0, 0.0).astype(BF16)
        acc = acc + cols[:, 2 + k:3 + k] * jnp.dot(sel, ys, preferred_element_type=F32)
    o_ref[...] = _rms(acc, lnf_ref[...])


def _combine(tcnt_i, tbase_i, pstart, h2, lpos, gates, lnf, ys, tm):
    n, d = h2.shape
    return pl.pallas_call(
        functools.partial(_combine_kernel, tm=tm),
        grid_spec=pltpu.PrefetchScalarGridSpec(
            num_scalar_prefetch=3,
            grid=(n // tm,),
            in_specs=[
                pl.BlockSpec((tm, d), lambda i, *_: (i, 0)),
                pl.BlockSpec((2, tm), lambda i, *_: (0, i)),
                pl.BlockSpec((2, tm), lambda i, *_: (0, i)),
                pl.BlockSpec((1, d), lambda i, *_: (0, 0)),
                pl.BlockSpec(memory_space=pl.ANY),
            ],
            out_specs=pl.BlockSpec((tm, d), lambda i, *_: (i, 0)),
            scratch_shapes=[pltpu.VMEM((2, _local_rows(tm), d), F32), pltpu.SemaphoreType.DMA((2,))],
        ),
        out_shape=jax.ShapeDtypeStruct((n, d), F32),
        compiler_params=pltpu.CompilerParams(
            dimension_semantics=("arbitrary",), vmem_limit_bytes=VMEM_LIMIT),
        name="combine",
    )(tcnt_i, tbase_i, pstart, h2, lpos, gates, lnf, ys)


def _swa_head_perm():
    cols = []
    for p in range(4):
        for half in range(2):
            head = half * 4 + p
            cols.extend(range(head * D_HEAD, (head + 1) * D_HEAD))
    return jnp.asarray(cols, dtype=I32)


def kernel(x, mem, positions, ln_mix_w, w_in, lambda_q1, lambda_k1, lambda_q2, lambda_k2, subln_w, sinks,
           w_out, ln_cross_w, ln_mem_w, wq_cross, wkv_cross, wo_cross, ln_moe_w, w_group, b_group,
           w_expert, b_expert, w_gate, w_up, w_down, ln_final_w):
    b, s, d = x.shape
    m_len = mem.shape[1]
    n = b * s
    assert w_in.shape[0] == 1 and d == 1024 and n <= 65536
    lambda_init = 0.8 - 0.6 * math.exp(-0.3 * 0)

    tm = 512
    t_attn = 256
    tk_attn = 1024
    bm = 256

    x2 = x.reshape(n, d)
    pos2 = positions.reshape(n, 1).astype(I32)
    inv_freq = jnp.exp(-math.log(ROPE_THETA) * jnp.arange(0, D_HEAD, 2, dtype=F32) / D_HEAD)
    inv128 = jnp.tile(inv_freq, LANES // (D_HEAD // 2)).reshape(1, LANES)

    perm = _swa_head_perm()
    sq0 = 3 * 512
    w_in_l = w_in[0]
    w_in_p = jnp.concatenate([w_in_l[:, :sq0], w_in_l[:, sq0:sq0 + 512][:, perm], w_in_l[:, sq0 + 512:]], axis=1)
    w_out_l = w_out[0]
    w_out_p = jnp.concatenate([w_out_l[:512], w_out_l[512:][perm]], axis=0)
    sinks_p = sinks[0].reshape(2, 4).reshape(-1)

    qkv = _inproj(x2, ln_mix_w[0].reshape(1, d), w_in_p.astype(BF16), pos2, inv128, tm)
    o_diff = _diff_attention(qkv, lambda_q1[0].reshape(1, -1), lambda_k1[0].reshape(1, -1),
                             lambda_q2[0].reshape(1, -1), lambda_k2[0].reshape(1, -1),
                             subln_w[0].reshape(-1, 1), b, s, t_attn, min(tk_attn, s), lambda_init)
    o_swa = _swa_attention(qkv, sinks_p.astype(F32), b, s, tm)
    kv = _memkv(mem.reshape(b * m_len, d), ln_mem_w[0].reshape(1, d), wkv_cross[0].astype(BF16), m_len)

    wr = jnp.concatenate([w_group[0].T, jnp.zeros((8 - N_GROUPS, d), F32), w_expert[0].T], axis=0)
    br = jnp.concatenate([b_group[0], jnp.zeros((8 - N_GROUPS,), F32), b_expert[0]]).reshape(-1, 1)
    h2, f, lpos, gates, tcnt, tbase = _post(
        x2, o_diff, o_swa, w_out_p.astype(BF16), ln_cross_w[0].reshape(1, d), wq_cross[0].astype(BF16), kv,
        wo_cross[0].astype(BF16), ln_moe_w[0].reshape(1, d), wr, br, b, s, tm, m_len)

    tcnt_i = tcnt[:, :, 0].astype(I32).reshape(-1)
    tbase_i = tbase[:, :, 0].astype(I32).reshape(-1)
    tot_i = tbase_i[-N_EXPERTS:] + tcnt_i[-N_EXPERTS:]
    rows_max = 2 * n + (n // tm) * N_EXPERTS * (SUBLANES - 1)
    nblk = (rows_max + N_EXPERTS * (bm - 1) + bm - 1) // bm
    xs0 = jnp.zeros((nblk * bm, d), F32)
    xs, pstart, blk_e, nact = _dispatch(tcnt_i, tbase_i, tot_i, f, lpos, xs0, tm, bm, nblk)
    ys = _experts(blk_e, nact, xs, w_gate[0].astype(BF16), w_up[0].astype(BF16), w_down[0].astype(BF16), bm)
    out = _combine(tcnt_i, tbase_i, pstart, h2, lpos, gates, ln_final_w.reshape(1, d), ys, tm)
    return out.reshape(b, s, d)
```

```python
import functools
import math

import jax
import jax.numpy as jnp
from jax import lax
from jax.experimental import pallas as pl
from jax.experimental.pallas import tpu as pltpu

F32 = jnp.float32
BF16 = jnp.bfloat16
I32 = jnp.int32

D_HEAD = 64
ROPE_THETA = 10000.0
H_DIFF = 4
N_Q_SWA = 8
N_KV_SWA = 2
WINDOW = 128
H_CROSS = 4
N_GROUPS = 4
E_PER_GROUP = 8
N_EXPERTS = N_GROUPS * E_PER_GROUP
EPS = 1e-6
LANES = 128
SUBLANES = 8
LONG_SEGMENT_ROWS = 128
IN_W = 2304
NEG = -1e30

VMEM_LIMIT = 56 * 1024 * 1024

NT_DIMS = (((1,), (1,)), ((), ()))


def _rms(x, w):
    ms = jnp.mean(x * x, axis=-1, keepdims=True)
    return x * lax.rsqrt(ms + EPS) * w


def _inproj_kernel(x_ref, lnw_ref, w_ref, pos_ref, inv_ref, o_ref):
    a = _rms(x_ref[...], lnw_ref[...]).astype(BF16)
    ang = pos_ref[...].astype(F32) * inv_ref[...]
    cos = jnp.cos(ang)
    sin = jnp.sin(ang)
    lane = lax.broadcasted_iota(I32, (1, LANES), 1)
    first = (lane % D_HEAD) < (D_HEAD // 2)
    sin_signed = jnp.where(first, -sin, sin)
    n_chunks = IN_W // 256
    for c in range(n_chunks):
        p = jnp.dot(a, w_ref[:, c * 256:(c + 1) * 256], preferred_element_type=F32)
        for hh in range(2):
            g = c * 2 + hh
            xg = p[:, hh * LANES:(hh + 1) * LANES]
            is_v = (8 <= g < 12) or g == 17
            if not is_v:
                partner = jnp.where(first, pltpu.roll(xg, 96, 1), pltpu.roll(xg, 32, 1))
                xg = xg * cos + partner * sin_signed
                if g < 4:
                    xg = xg * (D_HEAD ** -0.5 * math.log2(math.e))
                elif 12 <= g < 16:
                    xg = xg * (D_HEAD ** -0.5)
            o_ref[:, g * LANES:(g + 1) * LANES] = xg.astype(BF16)


def _inproj(x2, ln_w, w_in_bf, pos2, inv128, tm):
    n, d = x2.shape
    return pl.pallas_call(
        _inproj_kernel,
        grid=(n // tm,),
        in_specs=[
            pl.BlockSpec((tm, d), lambda i: (i, 0)),
            pl.BlockSpec((1, d), lambda i: (0, 0)),
            pl.BlockSpec((d, IN_W), lambda i: (0, 0)),
            pl.BlockSpec((tm, 1), lambda i: (i, 0)),
            pl.BlockSpec((1, LANES), lambda i: (0, 0)),
        ],
        out_specs=pl.BlockSpec((tm, IN_W), lambda i: (i, 0)),
        out_shape=jax.ShapeDtypeStruct((n, IN_W), BF16),
        compiler_params=pltpu.CompilerParams(
            dimension_semantics=("arbitrary",), vmem_limit_bytes=VMEM_LIMIT),
        name="inproj",
    )(x2, ln_w, w_in_bf, pos2, inv128)


def _diff_kernel(q_ref, k_ref, v_ref, lq1_ref, lk1_ref, lq2_ref, lk2_ref, sw_ref, o_ref,
                 vt_scr, acc_scr, m_scr, l_scr, sa_scr, sb_scr, *, t, tk, lambda_init):
    i = pl.program_id(2)

    @pl.when(i == 0)
    def _():
        for c in range(vt_scr.shape[0]):
            for r in range(tk // t):
                rows = slice(c * tk + r * t, c * tk + (r + 1) * t)
                vt_scr[c, :, r * t:(r + 1) * t] = v_ref[rows, :].astype(F32).T.astype(BF16)

    lane = lax.broadcasted_iota(I32, (1, LANES), 1)
    q = q_ref[...]
    zero = jnp.zeros_like(q)
    q2 = jnp.concatenate([jnp.where(lane < D_HEAD, q, zero), jnp.where(lane >= D_HEAD, q, zero)], axis=0)
    acc_scr[...] = jnp.zeros(acc_scr.shape, F32)
    m_scr[...] = jnp.full(m_scr.shape, NEG, F32)
    l_scr[...] = jnp.zeros(l_scr.shape, F32)
    n_full = (i * t) // tk

    def scores(j, buf):
        k = k_ref[pl.ds(pl.multiple_of(j * tk, tk), tk), :]
        buf[...] = lax.dot_general(k, q2, NT_DIMS, preferred_element_type=F32)

    def softmax_pv(j, buf, masked):
        st = buf[...]
        if masked:
            key = lax.broadcasted_iota(I32, (tk, 2 * t), 0) + j * tk
            col = lax.broadcasted_iota(I32, (tk, 2 * t), 1)
            qry = jnp.where(col >= t, col - t, col) + i * t
            st = jnp.where(key <= qry, st, NEG)
        m_prev = m_scr[...]
        m_new = jnp.maximum(m_prev, jnp.max(st, axis=0, keepdims=True))
        p = jnp.exp2(st - m_new)
        alpha = jnp.exp2(m_prev - m_new)
        l_scr[...] = alpha * l_scr[...] + jnp.sum(p, axis=0, keepdims=True)
        acc_scr[...] = alpha * acc_scr[...] + jnp.dot(vt_scr[j], p.astype(BF16), preferred_element_type=F32)
        m_scr[...] = m_new

    scores(0, sa_scr)

    def pair(jj, carry):
        scores(2 * jj + 1, sb_scr)
        softmax_pv(2 * jj, sa_scr, False)
        scores(2 * jj + 2, sa_scr)
        softmax_pv(2 * jj + 1, sb_scr, False)
        return carry

    lax.fori_loop(0, n_full // 2, pair, 0)
    last_even = 2 * (n_full // 2)

    @pl.when(n_full % 2 == 1)
    def _():
        scores(last_even + 1, sb_scr)
        softmax_pv(last_even, sa_scr, False)
        softmax_pv(last_even + 1, sb_scr, True)

    @pl.when(n_full % 2 == 0)
    def _():
        softmax_pv(last_even, sa_scr, True)

    lam = (jnp.exp(jnp.sum(lq1_ref[...] * lk1_ref[...], axis=1, keepdims=True))
           - jnp.exp(jnp.sum(lq2_ref[...] * lk2_ref[...], axis=1, keepdims=True))
           + lambda_init)
    on = acc_scr[...] / l_scr[...]
    o = on[:, :t] - lam * on[:, t:]
    ms = jnp.mean(o * o, axis=0, keepdims=True)
    o = o * lax.rsqrt(ms + EPS) * sw_ref[...] * (1.0 - lambda_init)
    o_ref[...] = o.T.astype(BF16)


def _diff_attention(qkv, lq1, lk1, lq2, lk2, subln_col, b, s, t, tk, lambda_init):
    n = qkv.shape[0]
    nq = s // t
    small = pl.BlockSpec((1, D_HEAD), lambda bi, h, i: (0, 0))
    return pl.pallas_call(
        functools.partial(_diff_kernel, t=t, tk=tk, lambda_init=lambda_init),
        grid=(b, H_DIFF, nq),
        in_specs=[
            pl.BlockSpec((t, LANES), lambda bi, h, i: (bi * nq + i, h)),
            pl.BlockSpec((s, LANES), lambda bi, h, i: (bi, 4 + h)),
            pl.BlockSpec((s, LANES), lambda bi, h, i: (bi, 8 + h)),
            small, small, small, small,
            pl.BlockSpec((LANES, 1), lambda bi, h, i: (0, 0)),
        ],
        out_specs=pl.BlockSpec((t, LANES), lambda bi, h, i: (bi * nq + i, h)),
        out_shape=jax.ShapeDtypeStruct((n, H_DIFF * LANES), BF16),
        scratch_shapes=[
            pltpu.VMEM((s // tk, LANES, tk), BF16),
            pltpu.VMEM((LANES, 2 * t), F32),
            pltpu.VMEM((1, 2 * t), F32),
            pltpu.VMEM((1, 2 * t), F32),
            pltpu.VMEM((tk, 2 * t), F32),
            pltpu.VMEM((tk, 2 * t), F32),
        ],
        compiler_params=pltpu.CompilerParams(
            dimension_semantics=("arbitrary", "arbitrary", "arbitrary"), vmem_limit_bytes=VMEM_LIMIT),
        name="diffattn",
    )(qkv, qkv, qkv, lq1, lk1, lq2, lk2, subln_col)


def _swa_kernel(sink_ref, q_ref, kc_ref, kp_ref, vc_ref, vp_ref, o_ref, kbuf, vbuf, *, tq):
    i = pl.program_id(1)
    w = WINDOW
    kbuf[0:w, :] = kp_ref[...]
    kbuf[w:w + tq, :] = kc_ref[...]
    vbuf[0:w, :] = vp_ref[...]
    vbuf[w:w + tq, :] = vc_ref[...]
    lane = lax.broadcasted_iota(I32, (1, LANES), 1)
    lo = lane < D_HEAD
    qi = lax.broadcasted_iota(I32, (w, 2 * w), 0)
    ki = lax.broadcasted_iota(I32, (w, 2 * w), 1)
    band = (ki > qi) & (ki <= qi + w)
    band_first = band & (ki >= jnp.where(i > 0, 0, w))
    for r in range(tq // w):
        keys = kbuf[r * w:(r + 2) * w, :]
        vals = vbuf[r * w:(r + 2) * w, :]
        valid = band_first if r == 0 else band
        qs = []
        for half in range(2):
            for p in range(4):
                qg = q_ref[r * w:(r + 1) * w, p * LANES:(p + 1) * LANES]
                qs.append(jnp.where(lo if half == 0 else ~lo, qg, jnp.zeros_like(qg)))
        qstack = jnp.concatenate(qs, axis=0)
        s_all = lax.dot_general(qstack, keys, NT_DIMS, preferred_element_type=F32)
        ps = []
        for hb in range(8):
            sink = sink_ref[hb]
            s = jnp.where(valid, s_all[hb * w:(hb + 1) * w], NEG)
            m = jnp.maximum(jnp.max(s, axis=1, keepdims=True), sink)
            e = jnp.exp(s - m)
            den = jnp.sum(e, axis=1, keepdims=True) + jnp.exp(sink - m)
            ps.append((e / den).astype(BF16))
        pv = jnp.dot(jnp.concatenate(ps, axis=0), vals, preferred_element_type=F32)
        for p in range(4):
            og = jnp.where(lo, pv[p * w:(p + 1) * w], pv[(4 + p) * w:(5 + p) * w])
            o_ref[r * w:(r + 1) * w, p * LANES:(p + 1) * LANES] = og.astype(BF16)


def _swa_attention(qkv, sinks, b, s, tq):
    n = qkv.shape[0]
    nq = s // tq
    per = tq // WINDOW
    prev_map = lambda col: (lambda bi, i, sk: (jnp.maximum(bi * (s // WINDOW) + i * per - 1, 0), col))
    cur_map = lambda col: (lambda bi, i, sk: (bi * nq + i, col))
    return pl.pallas_call(
        functools.partial(_swa_kernel, tq=tq),
        grid_spec=pltpu.PrefetchScalarGridSpec(
            num_scalar_prefetch=1,
            grid=(b, nq),
            in_specs=[
                pl.BlockSpec((tq, 4 * LANES), lambda bi, i, sk: (bi * nq + i, 3)),
                pl.BlockSpec((tq, LANES), cur_map(16)),
                pl.BlockSpec((WINDOW, LANES), prev_map(16)),
                pl.BlockSpec((tq, LANES), cur_map(17)),
                pl.BlockSpec((WINDOW, LANES), prev_map(17)),
            ],
            out_specs=pl.BlockSpec((tq, 4 * LANES), lambda bi, i, sk: (bi * nq + i, 0)),
            scratch_shapes=[pltpu.VMEM((WINDOW + tq, LANES), BF16)] * 2,
        ),
        out_shape=jax.ShapeDtypeStruct((n, 4 * LANES), BF16),
        compiler_params=pltpu.CompilerParams(
            dimension_semantics=("arbitrary", "arbitrary"), vmem_limit_bytes=VMEM_LIMIT),
        name="swa",
    )(sinks, qkv, qkv, qkv, qkv, qkv)


def _memkv_kernel(m_ref, lnw_ref, w_ref, o_ref):
    a = _rms(m_ref[...], lnw_ref[...]).astype(BF16)
    o_ref[...] = jnp.dot(a, w_ref[...], preferred_element_type=F32).astype(BF16)


def _memkv(mem2, ln_w, wkv_bf, m_len):
    n, d = mem2.shape
    return pl.pallas_call(
        _memkv_kernel,
        grid=(n // m_len,),
        in_specs=[
            pl.BlockSpec((m_len, d), lambda i: (i, 0)),
            pl.BlockSpec((1, d), lambda i: (0, 0)),
            pl.BlockSpec((d, 2 * d), lambda i: (0, 0)),
        ],
        out_specs=pl.BlockSpec((m_len, 2 * d), lambda i: (i, 0)),
        out_shape=jax.ShapeDtypeStruct((n, 2 * d), BF16),
        compiler_params=pltpu.CompilerParams(
            dimension_semantics=("arbitrary",), vmem_limit_bytes=VMEM_LIMIT),
        name="memkv",
    )(mem2, ln_w, wkv_bf)


def _split_bf16(x):
    hi = x.astype(BF16)
    lo = (x - hi.astype(F32)).astype(BF16)
    return hi, lo


def _post_kernel(x_ref, od_ref, os_ref, wout_ref, lnc_ref, wq_ref, kv_ref, wo_ref, lnm_ref, wr_ref, br_ref,
                 h_ref, f_ref, lpos_ref, gate_ref, tcnt_ref, tbase_ref, base_scr, *, tm, d):
    first_step = (pl.program_id(0) == 0) & (pl.program_id(1) == 0)

    @pl.when(first_step)
    def _():
        base_scr[...] = jnp.zeros(base_scr.shape, F32)

    mix = jnp.concatenate([od_ref[...], os_ref[...]], axis=1)
    h1 = x_ref[...] + jnp.dot(mix, wout_ref[...], preferred_element_type=F32)

    c = _rms(h1, lnc_ref[...]).astype(BF16)
    dc = d // H_CROSS
    q = (jnp.dot(c, wq_ref[...], preferred_element_type=F32) * (dc ** -0.5)).astype(BF16)
    outs = []
    for hd in range(H_CROSS):
        k = kv_ref[:, hd * dc:(hd + 1) * dc]
        v = kv_ref[:, d + hd * dc:d + (hd + 1) * dc]
        s = lax.dot_general(q[:, hd * dc:(hd + 1) * dc], k, NT_DIMS, preferred_element_type=F32)
        e = jnp.exp(s - jnp.max(s, axis=1, keepdims=True))
        den = jnp.sum(e, axis=1, keepdims=True)
        outs.append((jnp.dot(e.astype(BF16), v, preferred_element_type=F32) / den).astype(BF16))
    o = jnp.concatenate(outs, axis=1)
    h2 = h1 + jnp.dot(o, wo_ref[...], preferred_element_type=F32)
    h_ref[...] = h2
    f = _rms(h2, lnm_ref[...])
    f_ref[...] = f.astype(BF16)

    f_hi, f_lo = _split_bf16(f)
    w_hi, w_lo = _split_bf16(wr_ref[...])
    lg = (lax.dot_general(w_hi, f_hi, NT_DIMS, preferred_element_type=F32)
          + lax.dot_general(w_hi, f_lo, NT_DIMS, preferred_element_type=F32)
          + lax.dot_general(w_lo, f_hi, NT_DIMS, preferred_element_type=F32)) + br_ref[...]

    gl = lg[0:N_GROUPS]
    gmax = jnp.max(gl, axis=0, keepdims=True)
    gidx = lax.broadcasted_iota(I32, gl.shape, 0)
    g_sel = jnp.min(jnp.where(gl == gmax, gidx, N_GROUPS), axis=0, keepdims=True)
    g_p = 1.0 / jnp.sum(jnp.exp(gl - gmax), axis=0, keepdims=True)

    e8 = jnp.zeros((E_PER_GROUP, tm), F32)
    for g in range(N_GROUPS):
        e8 = e8 + jnp.where(g_sel == g, lg[8 + g * E_PER_GROUP:8 + (g + 1) * E_PER_GROUP], 0.0)
    ex = jnp.exp(e8 - jnp.max(e8, axis=0, keepdims=True))
    ep = ex / jnp.sum(ex, axis=0, keepdims=True)
    idx8 = lax.broadcasted_iota(I32, ep.shape, 0)
    p1 = jnp.max(ep, axis=0, keepdims=True)
    i1 = jnp.min(jnp.where(ep == p1, idx8, E_PER_GROUP), axis=0, keepdims=True)
    ep2 = jnp.where(idx8 == i1, -1.0, ep)
    p2 = jnp.max(ep2, axis=0, keepdims=True)
    i2 = jnp.min(jnp.where(ep2 == p2, idx8, E_PER_GROUP), axis=0, keepdims=True)
    psum = p1 + p2
    gate_ref[0:1, :] = g_p * (p1 / psum)
    gate_ref[1:2, :] = g_p * (p2 / psum)
    eid1 = g_sel * E_PER_GROUP + i1
    eid2 = g_sel * E_PER_GROUP + i2

    e32 = lax.broadcasted_iota(I32, (N_EXPERTS, tm), 0)
    oh1 = (e32 == eid1).astype(F32)
    oh2 = (e32 == eid2).astype(F32)
    cnt = oh1 + oh2
    tr = lax.broadcasted_iota(I32, (tm, tm), 0)
    tc = lax.broadcasted_iota(I32, (tm, tm), 1)
    upper = jnp.where(tr < tc, 1.0, 0.0).astype(BF16)
    earlier = jnp.dot(cnt.astype(BF16), upper, preferred_element_type=F32)
    seg = jnp.sum(cnt, axis=1, keepdims=True)
    seg = jnp.floor((seg + (SUBLANES - 1)) * (1.0 / SUBLANES)) * SUBLANES
    seg = jnp.broadcast_to(seg, (N_EXPERTS, LANES))
    er = lax.broadcasted_iota(I32, (N_EXPERTS, N_EXPERTS), 0)
    ec = lax.broadcasted_iota(I32, (N_EXPERTS, N_EXPERTS), 1)
    lower = jnp.where(ec < er, 1.0, 0.0).astype(BF16)
    start = jnp.dot(lower, seg.astype(BF16), preferred_element_type=F32)[:, 0:1]
    where = earlier + start
    lpos_ref[0:1, :] = jnp.sum(oh1 * where, axis=0, keepdims=True).astype(I32)
    lpos_ref[1:2, :] = jnp.sum(oh2 * where, axis=0, keepdims=True).astype(I32)
    tcnt_ref[...] = seg
    tbase_ref[...] = base_scr[...]
    base_scr[...] = base_scr[...] + seg


def _post(x2, od, osw, wout_bf, lnc, wq_bf, kv, wo_bf, lnm, wr, br, b, s, tm, m_len):
    n, d = x2.shape
    nt = s // tm
    row = lambda bi, i: (bi * nt + i, 0)
    const = lambda bi, i: (0, 0)
    return pl.pallas_call(
        functools.partial(_post_kernel, tm=tm, d=d),
        grid=(b, nt),
        in_specs=[
            pl.BlockSpec((tm, d), row),
            pl.BlockSpec((tm, d // 2), row),
            pl.BlockSpec((tm, d // 2), row),
            pl.BlockSpec((d, d), const),
            pl.BlockSpec((1, d), const),
            pl.BlockSpec((d, d), const),
            pl.BlockSpec((m_len, 2 * d), lambda bi, i: (bi, 0)),
            pl.BlockSpec((d, d), const),
            pl.BlockSpec((1, d), const),
            pl.BlockSpec((8 + N_EXPERTS, d), const),
            pl.BlockSpec((8 + N_EXPERTS, 1), const),
        ],
        out_specs=[
            pl.BlockSpec((tm, d), row),
            pl.BlockSpec((tm, d), row),
            pl.BlockSpec((2, tm), lambda bi, i: (0, bi * nt + i)),
            pl.BlockSpec((2, tm), lambda bi, i: (0, bi * nt + i)),
            pl.BlockSpec((None, N_EXPERTS, LANES), lambda bi, i: (bi * nt + i, 0, 0)),
            pl.BlockSpec((None, N_EXPERTS, LANES), lambda bi, i: (bi * nt + i, 0, 0)),
        ],
        out_shape=[
            jax.ShapeDtypeStruct((n, d), F32),
            jax.ShapeDtypeStruct((n, d), BF16),
            jax.ShapeDtypeStruct((2, n), I32),
            jax.ShapeDtypeStruct((2, n), F32),
            jax.ShapeDtypeStruct((n // tm, N_EXPERTS, LANES), F32),
            jax.ShapeDtypeStruct((n // tm, N_EXPERTS, LANES), F32),
        ],
        scratch_shapes=[pltpu.VMEM((N_EXPERTS, LANES), F32)],
        compiler_params=pltpu.CompilerParams(
            dimension_semantics=("arbitrary", "arbitrary"), vmem_limit_bytes=VMEM_LIMIT),
        name="post",
    )(x2, od, osw, wout_bf, lnc, wq_bf, kv, wo_bf, lnm, wr, br)


def _local_rows(tm):
    return 2 * tm + SUBLANES * N_EXPERTS


def _segment_copies(tile, tcnt_ref, tbase_ref, pstart_ref, make, act, max_len):
    sizes = [SUBLANES << k for k in reversed(range((max_len // SUBLANES).bit_length()))]

    if act == "wait":
        total = lax.fori_loop(0, N_EXPERTS, lambda e, acc: acc + tcnt_ref[tile * N_EXPERTS + e], 0)
        for size in sizes:
            @pl.when((total & size) != 0)
            def _():
                make(0, 0, size).wait()

        return total

    def pieces(ln, local0, glob0, some_sizes):
        for size in some_sizes:
            @pl.when((ln & size) != 0)
            def _():
                off = (ln // (2 * size)) * (2 * size)
                make(pl.multiple_of(local0 + off, SUBLANES), pl.multiple_of(glob0 + off, SUBLANES), size).start()

    big = [s for s in sizes if s >= LONG_SEGMENT_ROWS]
    small = [s for s in sizes if s < LONG_SEGMENT_ROWS]

    def per_expert(e, local0):
        ln = tcnt_ref[tile * N_EXPERTS + e]
        glob0 = pstart_ref[e] + tbase_ref[tile * N_EXPERTS + e]

        @pl.when(ln >= LONG_SEGMENT_ROWS)
        def _():
            pieces(ln, local0, glob0, big)

        pieces(ln, local0, glob0, small)
        return local0 + ln

    return lax.fori_loop(0, N_EXPERTS, per_expert, 0)


def _dispatch_kernel(tcnt_ref, tbase_ref, tot_ref, f_ref, lpos_ref, xs_hbm, pstart_ref, blk_ref, nact_ref,
                     loc_scr, zero_scr, sem, zsem, *, tm, bm, nblk):
    i = pl.program_id(0)
    nt = pl.num_programs(0)
    slot = i % 2

    @pl.when(i == 0)
    def _():
        def per_expert(e, blk0):
            nb = (tot_ref[e] + (bm - 1)) // bm
            pstart_ref[e] = blk0 * bm

            def fill(j, carry):
                blk_ref[j] = e
                return carry

            lax.fori_loop(blk0, blk0 + nb, fill, 0)
            return blk0 + nb

        nact = lax.fori_loop(0, N_EXPERTS, per_expert, 0)
        nact_ref[0] = nact

        def tail(j, carry):
            blk_ref[j] = N_EXPERTS - 1
            return carry

        lax.fori_loop(nact, nblk, tail, 0)

        zero_scr[...] = jnp.zeros(zero_scr.shape, F32)

        def zero_fill(act):
            def do(copy):
                if act == "start":
                    copy.start()
                else:
                    copy.wait()

            def per_expert_pad(e, carry):
                tot = tot_ref[e]
                pad = (tot + (bm - 1)) // bm * bm - tot
                row0 = pstart_ref[e] + tot
                for k in reversed(range((bm // SUBLANES).bit_length())):
                    size = SUBLANES << k

                    @pl.when((pad & size) != 0)
                    def _():
                        row = pl.multiple_of(row0 + (pad // (2 * size)) * (2 * size), SUBLANES)
                        do(pltpu.make_async_copy(zero_scr.at[pl.ds(0, size)], xs_hbm.at[pl.ds(row, size)], zsem))

                return carry

            lax.fori_loop(0, N_EXPERTS, per_expert_pad, 0)

            def per_unused_block(j, carry):
                do(pltpu.make_async_copy(zero_scr, xs_hbm.at[pl.ds(pl.multiple_of(j * bm, bm), bm)], zsem))
                return carry

            lax.fori_loop(nact, nblk, per_unused_block, 0)

        zero_fill("start")
        zero_fill("wait")

    def copies(tile, sl, act):
        def make(lrow, grow, size):
            return pltpu.make_async_copy(loc_scr.at[sl, pl.ds(lrow, size)], xs_hbm.at[pl.ds(grow, size)], sem.at[sl])

        _segment_copies(tile, tcnt_ref, tbase_ref, pstart_ref, make, act, _local_rows(tm))

    @pl.when(i >= 2)
    def _():
        copies(i - 2, slot, "wait")

    pos = lax.broadcasted_iota(I32, (_local_rows(tm), tm), 0)
    onehot = jnp.where(pos == lpos_ref[0:1, :], 1.0, jnp.where(pos == lpos_ref[1:2, :], 1.0, 0.0)).astype(BF16)
    loc_scr[slot] = jnp.dot(onehot, f_ref[...], preferred_element_type=F32)
    copies(i, slot, "start")

    @pl.when(i == nt - 1)
    def _():
        @pl.when(i >= 1)
        def _():
            copies(i - 1, 1 - slot, "wait")

        copies(i, slot, "wait")


def _dispatch(tcnt_i, tbase_i, tot_i, f, lpos, tm, bm, nblk):
    n, d = f.shape
    smem = pl.BlockSpec(memory_space=pltpu.SMEM)
    hbm = pl.BlockSpec(memory_space=pl.ANY)
    return pl.pallas_call(
        functools.partial(_dispatch_kernel, tm=tm, bm=bm, nblk=nblk),
        grid_spec=pltpu.PrefetchScalarGridSpec(
            num_scalar_prefetch=3,
            grid=(n // tm,),
            in_specs=[
                pl.BlockSpec((tm, d), lambda i, *_: (i, 0)),
                pl.BlockSpec((2, tm), lambda i, *_: (0, i)),
            ],
            out_specs=[hbm, smem, smem, smem],
            scratch_shapes=[
                pltpu.VMEM((2, _local_rows(tm), d), F32),
                pltpu.VMEM((bm, d), F32),
                pltpu.SemaphoreType.DMA((2,)),
                pltpu.SemaphoreType.DMA,
            ],
        ),
        out_shape=[
            jax.ShapeDtypeStruct((nblk * bm, d), F32),
            jax.ShapeDtypeStruct((N_EXPERTS,), I32),
            jax.ShapeDtypeStruct((nblk,), I32),
            jax.ShapeDtypeStruct((1,), I32),
        ],
        compiler_params=pltpu.CompilerParams(
            dimension_semantics=("arbitrary",), vmem_limit_bytes=VMEM_LIMIT),
        name="dispatch",
    )(tcnt_i, tbase_i, tot_i, f, lpos)


def _expert_kernel(blk_ref, nact_ref, x_ref, wg_ref, wu_ref, wd_ref, y_ref, wg_bf, wu_bf, wd_bf):
    j = pl.program_id(0)
    active = j < nact_ref[0]

    @pl.when(active & ((j == 0) | (blk_ref[j] != blk_ref[jnp.maximum(j - 1, 0)])))
    def _():
        wg_bf[...] = wg_ref[...].astype(BF16)
        wu_bf[...] = wu_ref[...].astype(BF16)
        wd_bf[...] = wd_ref[...].astype(BF16)

    @pl.when(active)
    def _():
        x = x_ref[...].astype(BF16)
        g = jnp.dot(x, wg_bf[...], preferred_element_type=F32)
        u = jnp.dot(x, wu_bf[...], preferred_element_type=F32)
        hdn = (g * jax.nn.sigmoid(g) * u).astype(BF16)
        y_ref[...] = jnp.dot(hdn, wd_bf[...], preferred_element_type=F32)

    @pl.when(j >= nact_ref[0])
    def _():
        y_ref[...] = jnp.zeros(y_ref.shape, y_ref.dtype)


def _experts(blk_e, nact, xs, wg, wu, wd, bm):
    p, d = xs.shape
    dff = wg.shape[-1]
    nblk = p // bm
    rowmap = lambda j, blk, na: (jnp.minimum(j, na[0] - 1), 0)
    wmap = lambda j, blk, na: (blk[jnp.minimum(j, na[0] - 1)], 0, 0)
    return pl.pallas_call(
        _expert_kernel,
        grid_spec=pltpu.PrefetchScalarGridSpec(
            num_scalar_prefetch=2,
            grid=(nblk,),
            in_specs=[
                pl.BlockSpec((bm, d), rowmap),
                pl.BlockSpec((None, d, dff), wmap),
                pl.BlockSpec((None, d, dff), wmap),
                pl.BlockSpec((None, dff, d), wmap),
            ],
            out_specs=pl.BlockSpec((bm, d), lambda j, blk, na: (j, 0)),
            scratch_shapes=[pltpu.VMEM((d, dff), BF16), pltpu.VMEM((d, dff), BF16), pltpu.VMEM((dff, d), BF16)],
        ),
        out_shape=jax.ShapeDtypeStruct((p, d), F32),
        compiler_params=pltpu.CompilerParams(
            dimension_semantics=("arbitrary",), vmem_limit_bytes=VMEM_LIMIT),
        name="experts",
    )(blk_e, nact, xs, wg, wu, wd)


def _combine_kernel(tcnt_ref, tbase_ref, pstart_ref, h_ref, lpos_ref, gate_ref, lnf_ref, ys_hbm, o_ref,
                    loc_scr, sem, *, tm):
    i = pl.program_id(0)
    nt = pl.num_programs(0)
    slot = i % 2

    def copies(tile, sl, act):
        def make(lrow, grow, size):
            return pltpu.make_async_copy(ys_hbm.at[pl.ds(grow, size)], loc_scr.at[sl, pl.ds(lrow, size)], sem.at[sl])

        return _segment_copies(tile, tcnt_ref, tbase_ref, pstart_ref, make, act, _local_rows(tm))

    @pl.when(i == 0)
    def _():
        copies(0, 0, "start")

    @pl.when(i + 1 < nt)
    def _():
        copies(i + 1, 1 - slot, "start")

    used = copies(i, slot, "wait")
    row = lax.broadcasted_iota(I32, (_local_rows(tm), 1), 0)
    ys = jnp.where(row < used, loc_scr[slot], 0.0).astype(BF16)

    r8 = lax.broadcasted_iota(I32, (8, tm), 0)
    lp = lpos_ref[...].astype(F32)
    top = jnp.where(r8 == 0, lp[0:1], jnp.where(r8 == 1, lp[1:2], jnp.where(
        r8 == 2, gate_ref[0:1, :], jnp.where(r8 == 3, gate_ref[1:2, :], 0.0))))
    cols = jnp.concatenate([top, jnp.zeros((LANES - 8, tm), F32)], axis=0).T
    pos = lax.broadcasted_iota(I32, (tm, _local_rows(tm)), 1)
    acc = h_ref[...]
    for k in range(2):
        sel = jnp.where(pos == cols[:, k:k + 1].astype(I32), 1.0, 0.0).astype(BF16)
        acc = acc + cols[:, 2 + k:3 + k] * jnp.dot(sel, ys, preferred_element_type=F32)
    o_ref[...] = _rms(acc, lnf_ref[...])


def _combine(tcnt_i, tbase_i, pstart, h2, lpos, gates, lnf, ys, tm):
    n, d = h2.shape
    return pl.pallas_call(
        functools.partial(_combine_kernel, tm=tm),
        grid_spec=pltpu.PrefetchScalarGridSpec(
            num_scalar_prefetch=3,
            grid=(n // tm,),
            in_specs=[
                pl.BlockSpec((tm, d), lambda i, *_: (i, 0)),
                pl.BlockSpec((2, tm), lambda i, *_: (0, i)),
                pl.BlockSpec((2, tm), lambda i, *_: (0, i)),
                pl.BlockSpec((1, d), lambda i, *_: (0, 0)),
                pl.BlockSpec(memory_space=pl.ANY),
            ],
            out_specs=pl.BlockSpec((tm, d), lambda i, *_: (i, 0)),
            scratch_shapes=[pltpu.VMEM((2, _local_rows(tm), d), F32), pltpu.SemaphoreType.DMA((2,))],
        ),
        out_shape=jax.ShapeDtypeStruct((n, d), F32),
        compiler_params=pltpu.CompilerParams(
            dimension_semantics=("arbitrary",), vmem_limit_bytes=VMEM_LIMIT),
        name="combine",
    )(tcnt_i, tbase_i, pstart, h2, lpos, gates, lnf, ys)


def _swa_head_perm():
    cols = []
    for p in range(4):
        for half in range(2):
            head = half * 4 + p
            cols.extend(range(head * D_HEAD, (head + 1) * D_HEAD))
    return jnp.asarray(cols, dtype=I32)


def kernel(x, mem, positions, ln_mix_w, w_in, lambda_q1, lambda_k1, lambda_q2, lambda_k2, subln_w, sinks,
           w_out, ln_cross_w, ln_mem_w, wq_cross, wkv_cross, wo_cross, ln_moe_w, w_group, b_group,
           w_expert, b_expert, w_gate, w_up, w_down, ln_final_w):
    b, s, d = x.shape
    m_len = mem.shape[1]
    n = b * s
    assert w_in.shape[0] == 1 and d == 1024 and n <= 65536
    lambda_init = 0.8 - 0.6 * math.exp(-0.3 * 0)

    tm = 512
    t_attn = 256
    tk_attn = 1024
    bm = 256

    x2 = x.reshape(n, d)
    pos2 = positions.reshape(n, 1).astype(I32)
    inv_freq = jnp.exp(-math.log(ROPE_THETA) * jnp.arange(0, D_HEAD, 2, dtype=F32) / D_HEAD)
    inv128 = jnp.tile(inv_freq, LANES // (D_HEAD // 2)).reshape(1, LANES)

    perm = _swa_head_perm()
    sq0 = 3 * 512
    w_in_l = w_in[0]
    w_in_p = jnp.concatenate([w_in_l[:, :sq0], w_in_l[:, sq0:sq0 + 512][:, perm], w_in_l[:, sq0 + 512:]], axis=1)
    w_out_l = w_out[0]
    w_out_p = jnp.concatenate([w_out_l[:512], w_out_l[512:][perm]], axis=0)
    sinks_p = sinks[0].reshape(2, 4).reshape(-1)

    qkv = _inproj(x2, ln_mix_w[0].reshape(1, d), w_in_p.astype(BF16), pos2, inv128, tm)
    o_diff = _diff_attention(qkv, lambda_q1[0].reshape(1, -1), lambda_k1[0].reshape(1, -1),
                             lambda_q2[0].reshape(1, -1), lambda_k2[0].reshape(1, -1),
                             subln_w[0].reshape(-1, 1), b, s, t_attn, min(tk_attn, s), lambda_init)
    o_swa = _swa_attention(qkv, sinks_p.astype(F32), b, s, tm)
    kv = _memkv(mem.reshape(b * m_len, d), ln_mem_w[0].reshape(1, d), wkv_cross[0].astype(BF16), m_len)

    wr = jnp.concatenate([w_group[0].T, jnp.zeros((8 - N_GROUPS, d), F32), w_expert[0].T], axis=0)
    br = jnp.concatenate([b_group[0], jnp.zeros((8 - N_GROUPS,), F32), b_expert[0]]).reshape(-1, 1)
    h2, f, lpos, gates, tcnt, tbase = _post(
        x2, o_diff, o_swa, w_out_p.astype(BF16), ln_cross_w[0].reshape(1, d), wq_cross[0].astype(BF16), kv,
        wo_cross[0].astype(BF16), ln_moe_w[0].reshape(1, d), wr, br, b, s, tm, m_len)

    tcnt_i = tcnt[:, :, 0].astype(I32).reshape(-1)
    tbase_i = tbase[:, :, 0].astype(I32).reshape(-1)
    tot_i = tbase_i[-N_EXPERTS:] + tcnt_i[-N_EXPERTS:]
    rows_max = 2 * n + (n // tm) * N_EXPERTS * (SUBLANES - 1)
    nblk = (rows_max + N_EXPERTS * (bm - 1) + bm - 1) // bm
    xs, pstart, blk_e, nact = _dispatch(tcnt_i, tbase_i, tot_i, f, lpos, tm, bm, nblk)
    ys = _experts(blk_e, nact, xs, w_gate[0], w_up[0], w_down[0], bm)
    out = _combine(tcnt_i, tbase_i, pstart, h2, lpos, gates, ln_final_w.reshape(1, d), ys, tm)
    return out.reshape(b, s, d)
```

```python
import functools
import math

import jax
import jax.numpy as jnp
from jax import lax
from jax.experimental import pallas as pl
from jax.experimental.pallas import tpu as pltpu

F32 = jnp.float32
BF16 = jnp.bfloat16
I32 = jnp.int32

D_HEAD = 64
ROPE_THETA = 10000.0
H_DIFF = 4
N_Q_SWA = 8
N_KV_SWA = 2
WINDOW = 128
H_CROSS = 4
N_GROUPS = 4
E_PER_GROUP = 8
N_EXPERTS = N_GROUPS * E_PER_GROUP
EPS = 1e-6
LANES = 128
SUBLANES = 8
LONG_SEGMENT_ROWS = 128
IN_W = 2304
NEG = -1e30

VMEM_LIMIT = 56 * 1024 * 1024

NT_DIMS = (((1,), (1,)), ((), ()))


U32 = jnp.uint32


def _pack_pairs(x):
    h = x.shape[1] // 2
    hi = lax.bitcast_convert_type(x[:, :h], U32)
    lo = lax.bitcast_convert_type(x[:, h:], U32)
    return hi | (lo >> 16)


def _unpack_pairs(p):
    a = lax.bitcast_convert_type(p & U32(0xFFFF0000), F32)
    b = lax.bitcast_convert_type(p << 16, F32)
    return jnp.concatenate([a, b], axis=1).astype(BF16)


def _rms(x, w):
    ms = jnp.mean(x * x, axis=-1, keepdims=True)
    return x * lax.rsqrt(ms + EPS) * w


def _inproj_kernel(x_ref, lnw_ref, w_ref, pos_ref, inv_ref, o_ref):
    a = _rms(x_ref[...], lnw_ref[...]).astype(BF16)
    ang = pos_ref[...].astype(F32) * inv_ref[...]
    cos = jnp.cos(ang)
    sin = jnp.sin(ang)
    lane = lax.broadcasted_iota(I32, (1, LANES), 1)
    first = (lane % D_HEAD) < (D_HEAD // 2)
    sin_signed = jnp.where(first, -sin, sin)
    n_chunks = IN_W // 256
    for c in range(n_chunks):
        p = jnp.dot(a, w_ref[:, c * 256:(c + 1) * 256], preferred_element_type=F32)
        for hh in range(2):
            g = c * 2 + hh
            xg = p[:, hh * LANES:(hh + 1) * LANES]
            is_v = (8 <= g < 12) or g == 17
            if not is_v:
                partner = jnp.where(first, pltpu.roll(xg, 96, 1), pltpu.roll(xg, 32, 1))
                xg = xg * cos + partner * sin_signed
                if g < 4:
                    xg = xg * (D_HEAD ** -0.5 * math.log2(math.e))
                elif 12 <= g < 16:
                    xg = xg * (D_HEAD ** -0.5)
            o_ref[:, g * LANES:(g + 1) * LANES] = xg.astype(BF16)


def _inproj(x2, ln_w, w_in_bf, pos2, inv128, tm):
    n, d = x2.shape
    return pl.pallas_call(
        _inproj_kernel,
        grid=(n // tm,),
        in_specs=[
            pl.BlockSpec((tm, d), lambda i: (i, 0)),
            pl.BlockSpec((1, d), lambda i: (0, 0)),
            pl.BlockSpec((d, IN_W), lambda i: (0, 0)),
            pl.BlockSpec((tm, 1), lambda i: (i, 0)),
            pl.BlockSpec((1, LANES), lambda i: (0, 0)),
        ],
        out_specs=pl.BlockSpec((tm, IN_W), lambda i: (i, 0)),
        out_shape=jax.ShapeDtypeStruct((n, IN_W), BF16),
        compiler_params=pltpu.CompilerParams(
            dimension_semantics=("arbitrary",), vmem_limit_bytes=VMEM_LIMIT),
        name="inproj",
    )(x2, ln_w, w_in_bf, pos2, inv128)


def _diff_kernel(q_ref, k_ref, v_ref, lq1_ref, lk1_ref, lq2_ref, lk2_ref, sw_ref, o_ref,
                 vt_scr, acc_scr, m_scr, l_scr, sa_scr, sb_scr, sc_scr, *, t, tk, lambda_init):
    s = k_ref.shape[0]
    per = tk // t
    for c in range(s // tk):
        for r in range(per):
            rows = slice(c * tk + r * t, c * tk + (r + 1) * t)
            vt_scr[c, :, r * t:(r + 1) * t] = v_ref[rows, :].astype(F32).T.astype(BF16)

    lane = lax.broadcasted_iota(I32, (1, LANES), 1)
    lam = (jnp.exp(jnp.sum(lq1_ref[...] * lk1_ref[...], axis=1, keepdims=True))
           - jnp.exp(jnp.sum(lq2_ref[...] * lk2_ref[...], axis=1, keepdims=True))
           + lambda_init)

    def tile(i, carry):
        q = q_ref[pl.ds(pl.multiple_of(i * t, t), t), :]
        zero = jnp.zeros_like(q)
        q2 = jnp.concatenate([jnp.where(lane < D_HEAD, q, zero), jnp.where(lane >= D_HEAD, q, zero)], axis=0)
        acc_scr[...] = jnp.zeros(acc_scr.shape, F32)
        m_scr[...] = jnp.full(m_scr.shape, NEG, F32)
        l_scr[...] = jnp.zeros(l_scr.shape, F32)
        n_full = i // per

        def scores(j, buf):
            k = k_ref[pl.ds(pl.multiple_of(j * tk, tk), tk), :]
            buf[...] = lax.dot_general(k, q2, NT_DIMS, preferred_element_type=F32)

        def accumulate(st_parts, vt):
            m_prev = m_scr[...]
            m_new = m_prev
            for st in st_parts:
                m_new = jnp.maximum(m_new, jnp.max(st, axis=0, keepdims=True))
            ps = [jnp.exp2(st - m_new) for st in st_parts]
            alpha = jnp.exp2(m_prev - m_new)
            l_new = alpha * l_scr[...]
            for p in ps:
                l_new = l_new + jnp.sum(p, axis=0, keepdims=True)
            l_scr[...] = l_new
            pb = [p.astype(BF16) for p in ps]
            pb = pb[0] if len(pb) == 1 else jnp.concatenate(pb, axis=0)
            acc_scr[...] = alpha * acc_scr[...] + jnp.dot(vt, pb, preferred_element_type=F32)
            m_scr[...] = m_new

        def full_block(j, buf):
            accumulate([buf[...]], vt_scr[j])

        def diagonal_block(v):
            rows = (v + 1) * t
            k = k_ref[pl.ds(pl.multiple_of(n_full * tk, tk), rows), :]
            sc_scr[0:rows, :] = lax.dot_general(k, q2, NT_DIMS, preferred_element_type=F32)
            scores(0, sa_scr)
            key = lax.broadcasted_iota(I32, (t, 2 * t), 0)
            col = lax.broadcasted_iota(I32, (t, 2 * t), 1)
            diag = jnp.where(key <= jnp.where(col >= t, col - t, col), sc_scr[v * t:rows, :], NEG)
            parts = [diag] if v == 0 else [sc_scr[0:v * t, :], diag]
            accumulate(parts, vt_scr[n_full, :, 0:rows])

        for v in range(per):
            pl.when(i % per == v)(functools.partial(diagonal_block, v))

        def pair(jj, c2):
            scores(2 * jj + 1, sb_scr)
            full_block(2 * jj, sa_scr)
            scores(2 * jj + 2, sa_scr)
            full_block(2 * jj + 1, sb_scr)
            return c2

        lax.fori_loop(0, n_full // 2, pair, 0)

        @pl.when(n_full % 2 == 1)
        def _():
            full_block(n_full - 1, sa_scr)

        on = acc_scr[...] / l_scr[...]
        o = on[:, :t] - lam * on[:, t:]
        ms = jnp.mean(o * o, axis=0, keepdims=True)
        o = o * lax.rsqrt(ms + EPS) * sw_ref[...] * (1.0 - lambda_init)
        o_ref[pl.ds(pl.multiple_of(i * t, t), t), :] = o.T.astype(BF16)
        return carry

    lax.fori_loop(0, s // t, tile, 0)


def _diff_attention(qkv, lq1, lk1, lq2, lk2, subln_col, b, s, t, tk, lambda_init):
    n = qkv.shape[0]
    small = pl.BlockSpec((1, D_HEAD), lambda bi, h: (0, 0))
    return pl.pallas_call(
        functools.partial(_diff_kernel, t=t, tk=tk, lambda_init=lambda_init),
        grid=(b, H_DIFF),
        in_specs=[
            pl.BlockSpec((s, LANES), lambda bi, h: (bi, h)),
            pl.BlockSpec((s, LANES), lambda bi, h: (bi, 4 + h)),
            pl.BlockSpec((s, LANES), lambda bi, h: (bi, 8 + h)),
            small, small, small, small,
            pl.BlockSpec((LANES, 1), lambda bi, h: (0, 0)),
        ],
        out_specs=pl.BlockSpec((s, LANES), lambda bi, h: (bi, h)),
        out_shape=jax.ShapeDtypeStruct((n, H_DIFF * LANES), BF16),
        scratch_shapes=[
            pltpu.VMEM((s // tk, LANES, tk), BF16),
            pltpu.VMEM((LANES, 2 * t), F32),
            pltpu.VMEM((1, 2 * t), F32),
            pltpu.VMEM((1, 2 * t), F32),
            pltpu.VMEM((tk, 2 * t), F32),
            pltpu.VMEM((tk, 2 * t), F32),
            pltpu.VMEM((tk, 2 * t), F32),
        ],
        compiler_params=pltpu.CompilerParams(
            dimension_semantics=("arbitrary", "arbitrary"), vmem_limit_bytes=VMEM_LIMIT),
        name="diffattn",
    )(qkv, qkv, qkv, lq1, lk1, lq2, lk2, subln_col)


def _swa_kernel(sink_ref, q_ref, kc_ref, kp_ref, vc_ref, vp_ref, o_ref, kbuf, vbuf, *, tq):
    i = pl.program_id(1)
    w = WINDOW
    kbuf[0:w, :] = kp_ref[...]
    kbuf[w:w + tq, :] = kc_ref[...]
    vbuf[0:w, :] = vp_ref[...]
    vbuf[w:w + tq, :] = vc_ref[...]
    lane = lax.broadcasted_iota(I32, (1, LANES), 1)
    lo = lane < D_HEAD
    qi = lax.broadcasted_iota(I32, (w, 2 * w), 0)
    ki = lax.broadcasted_iota(I32, (w, 2 * w), 1)
    band = (ki > qi) & (ki <= qi + w)
    band_first = band & (ki >= jnp.where(i > 0, 0, w))
    for r in range(tq // w):
        keys = kbuf[r * w:(r + 2) * w, :]
        vals = vbuf[r * w:(r + 2) * w, :]
        valid = band_first if r == 0 else band
        qs = []
        for half in range(2):
            for p in range(4):
                qg = q_ref[r * w:(r + 1) * w, p * LANES:(p + 1) * LANES]
                qs.append(jnp.where(lo if half == 0 else ~lo, qg, jnp.zeros_like(qg)))
        qstack = jnp.concatenate(qs, axis=0)
        s_all = lax.dot_general(qstack, keys, NT_DIMS, preferred_element_type=F32)
        ps = []
        for hb in range(8):
            sink = sink_ref[hb]
            s = jnp.where(valid, s_all[hb * w:(hb + 1) * w], NEG)
            m = jnp.maximum(jnp.max(s, axis=1, keepdims=True), sink)
            e = jnp.exp(s - m)
            den = jnp.sum(e, axis=1, keepdims=True) + jnp.exp(sink - m)
            ps.append((e / den).astype(BF16))
        pv = jnp.dot(jnp.concatenate(ps, axis=0), vals, preferred_element_type=F32)
        for p in range(4):
            og = jnp.where(lo, pv[p * w:(p + 1) * w], pv[(4 + p) * w:(5 + p) * w])
            o_ref[r * w:(r + 1) * w, p * LANES:(p + 1) * LANES] = og.astype(BF16)


def _swa_attention(qkv, sinks, b, s, tq):
    n = qkv.shape[0]
    nq = s // tq
    per = tq // WINDOW
    prev_map = lambda col: (lambda bi, i, sk: (jnp.maximum(bi * (s // WINDOW) + i * per - 1, 0), col))
    cur_map = lambda col: (lambda bi, i, sk: (bi * nq + i, col))
    return pl.pallas_call(
        functools.partial(_swa_kernel, tq=tq),
        grid_spec=pltpu.PrefetchScalarGridSpec(
            num_scalar_prefetch=1,
            grid=(b, nq),
            in_specs=[
                pl.BlockSpec((tq, 4 * LANES), lambda bi, i, sk: (bi * nq + i, 3)),
                pl.BlockSpec((tq, LANES), cur_map(16)),
                pl.BlockSpec((WINDOW, LANES), prev_map(16)),
                pl.BlockSpec((tq, LANES), cur_map(17)),
                pl.BlockSpec((WINDOW, LANES), prev_map(17)),
            ],
            out_specs=pl.BlockSpec((tq, 4 * LANES), lambda bi, i, sk: (bi * nq + i, 0)),
            scratch_shapes=[pltpu.VMEM((WINDOW + tq, LANES), BF16)] * 2,
        ),
        out_shape=jax.ShapeDtypeStruct((n, 4 * LANES), BF16),
        compiler_params=pltpu.CompilerParams(
            dimension_semantics=("arbitrary", "arbitrary"), vmem_limit_bytes=VMEM_LIMIT),
        name="swa",
    )(sinks, qkv, qkv, qkv, qkv, qkv)


def _memkv_kernel(m_ref, lnw_ref, w_ref, o_ref):
    a = _rms(m_ref[...], lnw_ref[...]).astype(BF16)
    o_ref[...] = jnp.dot(a, w_ref[...], preferred_element_type=F32).astype(BF16)


def _memkv(mem2, ln_w, wkv_bf, m_len):
    n, d = mem2.shape
    return pl.pallas_call(
        _memkv_kernel,
        grid=(n // m_len,),
        in_specs=[
            pl.BlockSpec((m_len, d), lambda i: (i, 0)),
            pl.BlockSpec((1, d), lambda i: (0, 0)),
            pl.BlockSpec((d, 2 * d), lambda i: (0, 0)),
        ],
        out_specs=pl.BlockSpec((m_len, 2 * d), lambda i: (i, 0)),
        out_shape=jax.ShapeDtypeStruct((n, 2 * d), BF16),
        compiler_params=pltpu.CompilerParams(
            dimension_semantics=("arbitrary",), vmem_limit_bytes=VMEM_LIMIT),
        name="memkv",
    )(mem2, ln_w, wkv_bf)


def _split_bf16(x):
    hi = x.astype(BF16)
    lo = (x - hi.astype(F32)).astype(BF16)
    return hi, lo


def _post_kernel(x_ref, od_ref, os_ref, wout_ref, lnc_ref, wq_ref, kv_ref, wo_ref, lnm_ref, wr_ref, br_ref,
                 h_ref, f_ref, lpos_ref, gate_ref, tcnt_ref, tbase_ref, base_scr, *, tm, d):
    first_step = (pl.program_id(0) == 0) & (pl.program_id(1) == 0)

    @pl.when(first_step)
    def _():
        base_scr[...] = jnp.zeros(base_scr.shape, F32)

    mix = jnp.concatenate([od_ref[...], os_ref[...]], axis=1)
    h1 = x_ref[...] + jnp.dot(mix, wout_ref[...], preferred_element_type=F32)

    c = _rms(h1, lnc_ref[...]).astype(BF16)
    dc = d // H_CROSS
    q = (jnp.dot(c, wq_ref[...], preferred_element_type=F32) * (dc ** -0.5)).astype(BF16)
    outs = []
    for hd in range(H_CROSS):
        k = kv_ref[:, hd * dc:(hd + 1) * dc]
        v = kv_ref[:, d + hd * dc:d + (hd + 1) * dc]
        s = lax.dot_general(q[:, hd * dc:(hd + 1) * dc], k, NT_DIMS, preferred_element_type=F32)
        e = jnp.exp(s - jnp.max(s, axis=1, keepdims=True))
        den = jnp.sum(e, axis=1, keepdims=True)
        outs.append((jnp.dot(e.astype(BF16), v, preferred_element_type=F32) / den).astype(BF16))
    o = jnp.concatenate(outs, axis=1)
    h2 = h1 + jnp.dot(o, wo_ref[...], preferred_element_type=F32)
    h_ref[...] = h2
    f = _rms(h2, lnm_ref[...])
    f_ref[...] = f.astype(BF16)

    f_hi, f_lo = _split_bf16(f)
    w_hi, w_lo = _split_bf16(wr_ref[...])
    lg = (lax.dot_general(w_hi, f_hi, NT_DIMS, preferred_element_type=F32)
          + lax.dot_general(w_hi, f_lo, NT_DIMS, preferred_element_type=F32)
          + lax.dot_general(w_lo, f_hi, NT_DIMS, preferred_element_type=F32)) + br_ref[...]

    gl = lg[0:N_GROUPS]
    gmax = jnp.max(gl, axis=0, keepdims=True)
    gidx = lax.broadcasted_iota(I32, gl.shape, 0)
    g_sel = jnp.min(jnp.where(gl == gmax, gidx, N_GROUPS), axis=0, keepdims=True)
    g_p = 1.0 / jnp.sum(jnp.exp(gl - gmax), axis=0, keepdims=True)

    e8 = jnp.zeros((E_PER_GROUP, tm), F32)
    for g in range(N_GROUPS):
        e8 = e8 + jnp.where(g_sel == g, lg[8 + g * E_PER_GROUP:8 + (g + 1) * E_PER_GROUP], 0.0)
    ex = jnp.exp(e8 - jnp.max(e8, axis=0, keepdims=True))
    ep = ex / jnp.sum(ex, axis=0, keepdims=True)
    idx8 = lax.broadcasted_iota(I32, ep.shape, 0)
    p1 = jnp.max(ep, axis=0, keepdims=True)
    i1 = jnp.min(jnp.where(ep == p1, idx8, E_PER_GROUP), axis=0, keepdims=True)
    ep2 = jnp.where(idx8 == i1, -1.0, ep)
    p2 = jnp.max(ep2, axis=0, keepdims=True)
    i2 = jnp.min(jnp.where(ep2 == p2, idx8, E_PER_GROUP), axis=0, keepdims=True)
    psum = p1 + p2
    gate_ref[0:1, :] = g_p * (p1 / psum)
    gate_ref[1:2, :] = g_p * (p2 / psum)
    eid1 = g_sel * E_PER_GROUP + i1
    eid2 = g_sel * E_PER_GROUP + i2

    e32 = lax.broadcasted_iota(I32, (N_EXPERTS, tm), 0)
    oh1 = (e32 == eid1).astype(F32)
    oh2 = (e32 == eid2).astype(F32)
    cnt = oh1 + oh2
    tr = lax.broadcasted_iota(I32, (tm, tm), 0)
    tc = lax.broadcasted_iota(I32, (tm, tm), 1)
    upper = jnp.where(tr < tc, 1.0, 0.0).astype(BF16)
    earlier = jnp.dot(cnt.astype(BF16), upper, preferred_element_type=F32)
    seg = jnp.sum(cnt, axis=1, keepdims=True)
    seg = jnp.floor((seg + (SUBLANES - 1)) * (1.0 / SUBLANES)) * SUBLANES
    seg = jnp.broadcast_to(seg, (N_EXPERTS, LANES))
    er = lax.broadcasted_iota(I32, (N_EXPERTS, N_EXPERTS), 0)
    ec = lax.broadcasted_iota(I32, (N_EXPERTS, N_EXPERTS), 1)
    lower = jnp.where(ec < er, 1.0, 0.0).astype(BF16)
    start = jnp.dot(lower, seg.astype(BF16), preferred_element_type=F32)[:, 0:1]
    where = earlier + start
    lpos_ref[0:1, :] = jnp.sum(oh1 * where, axis=0, keepdims=True).astype(I32)
    lpos_ref[1:2, :] = jnp.sum(oh2 * where, axis=0, keepdims=True).astype(I32)
    tcnt_ref[...] = seg
    tbase_ref[...] = base_scr[...]
    base_scr[...] = base_scr[...] + seg


def _post(x2, od, osw, wout_bf, lnc, wq_bf, kv, wo_bf, lnm, wr, br, b, s, tm, m_len):
    n, d = x2.shape
    nt = s // tm
    row = lambda bi, i: (bi * nt + i, 0)
    const = lambda bi, i: (0, 0)
    return pl.pallas_call(
        functools.partial(_post_kernel, tm=tm, d=d),
        grid=(b, nt),
        in_specs=[
            pl.BlockSpec((tm, d), row),
            pl.BlockSpec((tm, d // 2), row),
            pl.BlockSpec((tm, d // 2), row),
            pl.BlockSpec((d, d), const),
            pl.BlockSpec((1, d), const),
            pl.BlockSpec((d, d), const),
            pl.BlockSpec((m_len, 2 * d), lambda bi, i: (bi, 0)),
            pl.BlockSpec((d, d), const),
            pl.BlockSpec((1, d), const),
            pl.BlockSpec((8 + N_EXPERTS, d), const),
            pl.BlockSpec((8 + N_EXPERTS, 1), const),
        ],
        out_specs=[
            pl.BlockSpec((tm, d), row),
            pl.BlockSpec((tm, d), row),
            pl.BlockSpec((2, tm), lambda bi, i: (0, bi * nt + i)),
            pl.BlockSpec((2, tm), lambda bi, i: (0, bi * nt + i)),
            pl.BlockSpec((None, N_EXPERTS, LANES), lambda bi, i: (bi * nt + i, 0, 0)),
            pl.BlockSpec((None, N_EXPERTS, LANES), lambda bi, i: (bi * nt + i, 0, 0)),
        ],
        out_shape=[
            jax.ShapeDtypeStruct((n, d), F32),
            jax.ShapeDtypeStruct((n, d), BF16),
            jax.ShapeDtypeStruct((2, n), I32),
            jax.ShapeDtypeStruct((2, n), F32),
            jax.ShapeDtypeStruct((n // tm, N_EXPERTS, LANES), F32),
            jax.ShapeDtypeStruct((n // tm, N_EXPERTS, LANES), F32),
        ],
        scratch_shapes=[pltpu.VMEM((N_EXPERTS, LANES), F32)],
        compiler_params=pltpu.CompilerParams(
            dimension_semantics=("arbitrary", "arbitrary"), vmem_limit_bytes=VMEM_LIMIT),
        name="post",
    )(x2, od, osw, wout_bf, lnc, wq_bf, kv, wo_bf, lnm, wr, br)


def _local_rows(tm):
    return 2 * tm + SUBLANES * N_EXPERTS


def _segment_copies(tile, tcnt_ref, tbase_ref, pstart_ref, make, act, max_len):
    sizes = [SUBLANES << k for k in reversed(range((max_len // SUBLANES).bit_length()))]

    if act == "wait":
        total = lax.fori_loop(0, N_EXPERTS, lambda e, acc: acc + tcnt_ref[tile * N_EXPERTS + e], 0)
        for size in sizes:
            @pl.when((total & size) != 0)
            def _():
                make(0, 0, size).wait()

        return total

    def pieces(ln, local0, glob0, some_sizes):
        for size in some_sizes:
            @pl.when((ln & size) != 0)
            def _():
                off = (ln // (2 * size)) * (2 * size)
                make(pl.multiple_of(local0 + off, SUBLANES), pl.multiple_of(glob0 + off, SUBLANES), size).start()

    big = [s for s in sizes if s >= LONG_SEGMENT_ROWS]
    small = [s for s in sizes if s < LONG_SEGMENT_ROWS]

    def per_expert(e, local0):
        ln = tcnt_ref[tile * N_EXPERTS + e]
        glob0 = pstart_ref[e] + tbase_ref[tile * N_EXPERTS + e]

        @pl.when(ln >= LONG_SEGMENT_ROWS)
        def _():
            pieces(ln, local0, glob0, big)

        pieces(ln, local0, glob0, small)
        return local0 + ln

    return lax.fori_loop(0, N_EXPERTS, per_expert, 0)


def _dispatch_kernel(tcnt_ref, tbase_ref, tot_ref, f_ref, lpos_ref, xs_hbm, pstart_ref, blk_ref, nact_ref,
                     loc_scr, zero_scr, sem, zsem, *, tm, bm, nblk):
    i = pl.program_id(0)
    nt = pl.num_programs(0)
    slot = i % 2

    @pl.when(i == 0)
    def _():
        def per_expert(e, blk0):
            nb = (tot_ref[e] + (bm - 1)) // bm
            pstart_ref[e] = blk0 * bm

            def fill(j, carry):
                blk_ref[j] = e
                return carry

            lax.fori_loop(blk0, blk0 + nb, fill, 0)
            return blk0 + nb

        nact = lax.fori_loop(0, N_EXPERTS, per_expert, 0)
        nact_ref[0] = nact

        def tail(j, carry):
            blk_ref[j] = N_EXPERTS - 1
            return carry

        lax.fori_loop(nact, nblk, tail, 0)

        zero_scr[...] = jnp.zeros(zero_scr.shape, U32)

        def zero_fill(act):
            def do(copy):
                if act == "start":
                    copy.start()
                else:
                    copy.wait()

            def per_expert_pad(e, carry):
                tot = tot_ref[e]
                pad = (tot + (bm - 1)) // bm * bm - tot
                row0 = pstart_ref[e] + tot
                for k in reversed(range((bm // SUBLANES).bit_length())):
                    size = SUBLANES << k

                    @pl.when((pad & size) != 0)
                    def _():
                        row = pl.multiple_of(row0 + (pad // (2 * size)) * (2 * size), SUBLANES)
                        do(pltpu.make_async_copy(zero_scr.at[pl.ds(0, size)], xs_hbm.at[pl.ds(row, size)], zsem))

                return carry

            lax.fori_loop(0, N_EXPERTS, per_expert_pad, 0)

            def per_unused_block(j, carry):
                do(pltpu.make_async_copy(zero_scr, xs_hbm.at[pl.ds(pl.multiple_of(j * bm, bm), bm)], zsem))
                return carry

            lax.fori_loop(nact, nblk, per_unused_block, 0)

        zero_fill("start")
        zero_fill("wait")

    def copies(tile, sl, act):
        def make(lrow, grow, size):
            return pltpu.make_async_copy(loc_scr.at[sl, pl.ds(lrow, size)], xs_hbm.at[pl.ds(grow, size)], sem.at[sl])

        _segment_copies(tile, tcnt_ref, tbase_ref, pstart_ref, make, act, _local_rows(tm))

    @pl.when(i >= 2)
    def _():
        copies(i - 2, slot, "wait")

    pos = lax.broadcasted_iota(I32, (_local_rows(tm), tm), 0)
    onehot = jnp.where(pos == lpos_ref[0:1, :], 1.0, jnp.where(pos == lpos_ref[1:2, :], 1.0, 0.0)).astype(BF16)
    loc_scr[slot] = _pack_pairs(jnp.dot(onehot, f_ref[...], preferred_element_type=F32))
    copies(i, slot, "start")

    @pl.when(i == nt - 1)
    def _():
        @pl.when(i >= 1)
        def _():
            copies(i - 1, 1 - slot, "wait")

        copies(i, slot, "wait")


def _dispatch(tcnt_i, tbase_i, tot_i, f, lpos, tm, bm, nblk):
    n, d = f.shape
    smem = pl.BlockSpec(memory_space=pltpu.SMEM)
    hbm = pl.BlockSpec(memory_space=pl.ANY)
    return pl.pallas_call(
        functools.partial(_dispatch_kernel, tm=tm, bm=bm, nblk=nblk),
        grid_spec=pltpu.PrefetchScalarGridSpec(
            num_scalar_prefetch=3,
            grid=(n // tm,),
            in_specs=[
                pl.BlockSpec((tm, d), lambda i, *_: (i, 0)),
                pl.BlockSpec((2, tm), lambda i, *_: (0, i)),
            ],
            out_specs=[hbm, smem, smem, smem],
            scratch_shapes=[
                pltpu.VMEM((2, _local_rows(tm), d // 2), U32),
                pltpu.VMEM((bm, d // 2), U32),
                pltpu.SemaphoreType.DMA((2,)),
                pltpu.SemaphoreType.DMA,
            ],
        ),
        out_shape=[
            jax.ShapeDtypeStruct((nblk * bm, d // 2), U32),
            jax.ShapeDtypeStruct((N_EXPERTS,), I32),
            jax.ShapeDtypeStruct((nblk,), I32),
            jax.ShapeDtypeStruct((1,), I32),
        ],
        compiler_params=pltpu.CompilerParams(
            dimension_semantics=("arbitrary",), vmem_limit_bytes=VMEM_LIMIT),
        name="dispatch",
    )(tcnt_i, tbase_i, tot_i, f, lpos)


def _expert_kernel(blk_ref, nact_ref, x_ref, wg_ref, wu_ref, wd_ref, y_ref, wg_bf, wu_bf, wd_bf):
    j = pl.program_id(0)
    active = j < nact_ref[0]

    @pl.when(active & ((j == 0) | (blk_ref[j] != blk_ref[jnp.maximum(j - 1, 0)])))
    def _():
        wg_bf[...] = wg_ref[...].astype(BF16)
        wu_bf[...] = wu_ref[...].astype(BF16)
        wd_bf[...] = wd_ref[...].astype(BF16)

    @pl.when(active)
    def _():
        x = _unpack_pairs(x_ref[...])
        g = jnp.dot(x, wg_bf[...], preferred_element_type=F32)
        u = jnp.dot(x, wu_bf[...], preferred_element_type=F32)
        hdn = (g * jax.nn.sigmoid(g) * u).astype(BF16)
        y = jnp.dot(hdn, wd_bf[...], preferred_element_type=F32)
        y_ref[...] = _pack_pairs(y.astype(BF16).astype(F32))

    @pl.when(j >= nact_ref[0])
    def _():
        y_ref[...] = jnp.zeros(y_ref.shape, y_ref.dtype)


def _experts(blk_e, nact, xs, wg, wu, wd, bm):
    p, dh = xs.shape
    d, dff = wg.shape[-2:]
    nblk = p // bm
    rowmap = lambda j, blk, na: (jnp.minimum(j, na[0] - 1), 0)
    wmap = lambda j, blk, na: (blk[jnp.minimum(j, na[0] - 1)], 0, 0)
    return pl.pallas_call(
        _expert_kernel,
        grid_spec=pltpu.PrefetchScalarGridSpec(
            num_scalar_prefetch=2,
            grid=(nblk,),
            in_specs=[
                pl.BlockSpec((bm, dh), rowmap),
                pl.BlockSpec((None, d, dff), wmap),
                pl.BlockSpec((None, d, dff), wmap),
                pl.BlockSpec((None, dff, d), wmap),
            ],
            out_specs=pl.BlockSpec((bm, dh), lambda j, blk, na: (j, 0)),
            scratch_shapes=[pltpu.VMEM((d, dff), BF16), pltpu.VMEM((d, dff), BF16), pltpu.VMEM((dff, d), BF16)],
        ),
        out_shape=jax.ShapeDtypeStruct((p, dh), U32),
        compiler_params=pltpu.CompilerParams(
            dimension_semantics=("arbitrary",), vmem_limit_bytes=VMEM_LIMIT),
        name="experts",
    )(blk_e, nact, xs, wg, wu, wd)


def _combine_kernel(tcnt_ref, tbase_ref, pstart_ref, h_ref, lpos_ref, gate_ref, lnf_ref, ys_hbm, o_ref,
                    loc_scr, sem, *, tm):
    i = pl.program_id(0)
    nt = pl.num_programs(0)
    slot = i % 2

    def copies(tile, sl, act):
        def make(lrow, grow, size):
            return pltpu.make_async_copy(ys_hbm.at[pl.ds(grow, size)], loc_scr.at[sl, pl.ds(lrow, size)], sem.at[sl])

        return _segment_copies(tile, tcnt_ref, tbase_ref, pstart_ref, make, act, _local_rows(tm))

    @pl.when(i == 0)
    def _():
        copies(0, 0, "start")

    @pl.when(i + 1 < nt)
    def _():
        copies(i + 1, 1 - slot, "start")

    used = copies(i, slot, "wait")
    row = lax.broadcasted_iota(I32, (_local_rows(tm), 1), 0)
    ys = _unpack_pairs(jnp.where(row < used, loc_scr[slot], U32(0)))

    r8 = lax.broadcasted_iota(I32, (8, tm), 0)
    lp = lpos_ref[...].astype(F32)
    top = jnp.where(r8 == 0, lp[0:1], jnp.where(r8 == 1, lp[1:2], jnp.where(
        r8 == 2, gate_ref[0:1, :], jnp.where(r8 == 3, gate_ref[1:2, :], 0.0))))
    cols = jnp.concatenate([top, jnp.zeros((LANES - 8, tm), F32)], axis=0).T
    pos = lax.broadcasted_iota(I32, (tm, _local_rows(tm)), 1)
    acc = h_ref[...]
    for k in range(2):
        sel = jnp.where(pos == cols[:, k:k + 1].astype(I32), 1.0, 0.0).astype(BF16)
        acc = acc + cols[:, 2 + k:3 + k] * jnp.dot(sel, ys, preferred_element_type=F32)
    o_ref[...] = _rms(acc, lnf_ref[...])


def _combine(tcnt_i, tbase_i, pstart, h2, lpos, gates, lnf, ys, tm):
    n, d = h2.shape
    return pl.pallas_call(
        functools.partial(_combine_kernel, tm=tm),
        grid_spec=pltpu.PrefetchScalarGridSpec(
            num_scalar_prefetch=3,
            grid=(n // tm,),
            in_specs=[
                pl.BlockSpec((tm, d), lambda i, *_: (i, 0)),
                pl.BlockSpec((2, tm), lambda i, *_: (0, i)),
                pl.BlockSpec((2, tm), lambda i, *_: (0, i)),
                pl.BlockSpec((1, d), lambda i, *_: (0, 0)),
                pl.BlockSpec(memory_space=pl.ANY),
            ],
            out_specs=pl.BlockSpec((tm, d), lambda i, *_: (i, 0)),
            scratch_shapes=[pltpu.VMEM((2, _local_rows(tm), d // 2), U32), pltpu.SemaphoreType.DMA((2,))],
        ),
        out_shape=jax.ShapeDtypeStruct((n, d), F32),
        compiler_params=pltpu.CompilerParams(
            dimension_semantics=("arbitrary",), vmem_limit_bytes=VMEM_LIMIT),
        name="combine",
    )(tcnt_i, tbase_i, pstart, h2, lpos, gates, lnf, ys)


def _swa_head_perm():
    cols = []
    for p in range(4):
        for half in range(2):
            head = half * 4 + p
            cols.extend(range(head * D_HEAD, (head + 1) * D_HEAD))
    return jnp.asarray(cols, dtype=I32)


def kernel(x, mem, positions, ln_mix_w, w_in, lambda_q1, lambda_k1, lambda_q2, lambda_k2, subln_w, sinks,
           w_out, ln_cross_w, ln_mem_w, wq_cross, wkv_cross, wo_cross, ln_moe_w, w_group, b_group,
           w_expert, b_expert, w_gate, w_up, w_down, ln_final_w):
    b, s, d = x.shape
    m_len = mem.shape[1]
    n = b * s
    assert w_in.shape[0] == 1 and d == 1024 and n <= 65536
    lambda_init = 0.8 - 0.6 * math.exp(-0.3 * 0)

    tm = 512
    t_attn = 256
    tk_attn = 1024
    bm = 256

    x2 = x.reshape(n, d)
    pos2 = positions.reshape(n, 1).astype(I32)
    inv_freq = jnp.exp(-math.log(ROPE_THETA) * jnp.arange(0, D_HEAD, 2, dtype=F32) / D_HEAD)
    inv128 = jnp.tile(inv_freq, LANES // (D_HEAD // 2)).reshape(1, LANES)

    perm = _swa_head_perm()
    sq0 = 3 * 512
    w_in_l = w_in[0]
    w_in_p = jnp.concatenate([w_in_l[:, :sq0], w_in_l[:, sq0:sq0 + 512][:, perm], w_in_l[:, sq0 + 512:]], axis=1)
    w_out_l = w_out[0]
    w_out_p = jnp.concatenate([w_out_l[:512], w_out_l[512:][perm]], axis=0)
    sinks_p = sinks[0].reshape(2, 4).reshape(-1)

    qkv = _inproj(x2, ln_mix_w[0].reshape(1, d), w_in_p.astype(BF16), pos2, inv128, tm)
    o_diff = _diff_attention(qkv, lambda_q1[0].reshape(1, -1), lambda_k1[0].reshape(1, -1),
                             lambda_q2[0].reshape(1, -1), lambda_k2[0].reshape(1, -1),
                             subln_w[0].reshape(-1, 1), b, s, t_attn, min(tk_attn, s), lambda_init)
    o_swa = _swa_attention(qkv, sinks_p.astype(F32), b, s, tm)
    kv = _memkv(mem.reshape(b * m_len, d), ln_mem_w[0].reshape(1, d), wkv_cross[0].astype(BF16), m_len)

    wr = jnp.concatenate([w_group[0].T, jnp.zeros((8 - N_GROUPS, d), F32), w_expert[0].T], axis=0)
    br = jnp.concatenate([b_group[0], jnp.zeros((8 - N_GROUPS,), F32), b_expert[0]]).reshape(-1, 1)
    h2, f, lpos, gates, tcnt, tbase = _post(
        x2, o_diff, o_swa, w_out_p.astype(BF16), ln_cross_w[0].reshape(1, d), wq_cross[0].astype(BF16), kv,
        wo_cross[0].astype(BF16), ln_moe_w[0].reshape(1, d), wr, br, b, s, tm, m_len)

    tcnt_i = tcnt[:, :, 0].astype(I32).reshape(-1)
    tbase_i = tbase[:, :, 0].astype(I32).reshape(-1)
    tot_i = tbase_i[-N_EXPERTS:] + tcnt_i[-N_EXPERTS:]
    rows_max = 2 * n + (n // tm) * N_EXPERTS * (SUBLANES - 1)
    nblk = (rows_max + N_EXPERTS * (bm - 1) + bm - 1) // bm
    xs, pstart, blk_e, nact = _dispatch(tcnt_i, tbase_i, tot_i, f, lpos, tm, bm, nblk)
    ys = _experts(blk_e, nact, xs, w_gate[0], w_up[0], w_down[0], bm)
    out = _combine(tcnt_i, tbase_i, pstart, h2, lpos, gates, ln_final_w.reshape(1, d), ys, tm)
    return out.reshape(b, s, d)
```

```python
import functools
import math

import jax
import jax.numpy as jnp
from jax import lax
from jax.experimental import pallas as pl
from jax.experimental.pallas import tpu as pltpu

F32 = jnp.float32
BF16 = jnp.bfloat16
I32 = jnp.int32

D_HEAD = 64
ROPE_THETA = 10000.0
H_DIFF = 4
N_Q_SWA = 8
N_KV_SWA = 2
WINDOW = 128
H_CROSS = 4
N_GROUPS = 4
E_PER_GROUP = 8
N_EXPERTS = N_GROUPS * E_PER_GROUP
EPS = 1e-6
LANES = 128
SUBLANES = 8
LONG_SEGMENT_ROWS = 128
IN_W = 2304
NEG = -1e30

VMEM_LIMIT = 56 * 1024 * 1024

NT_DIMS = (((1,), (1,)), ((), ()))


U32 = jnp.uint32


def _pack_pairs(x):
    h = x.shape[1] // 2
    hi = lax.bitcast_convert_type(x[:, :h], U32)
    lo = lax.bitcast_convert_type(x[:, h:], U32)
    return hi | (lo >> 16)


def _unpack_pairs(p):
    a = lax.bitcast_convert_type(p & U32(0xFFFF0000), F32)
    b = lax.bitcast_convert_type(p << 16, F32)
    return jnp.concatenate([a, b], axis=1).astype(BF16)


def _rms(x, w):
    ms = jnp.mean(x * x, axis=-1, keepdims=True)
    return x * lax.rsqrt(ms + EPS) * w


def _rope_table_kernel(pos_ref, inv_ref, cos_ref, sin_ref):
    ang = pos_ref[...].astype(F32) * inv_ref[...]
    cos_ref[...] = jnp.cos(ang)
    sin_ref[...] = jnp.sin(ang)


def _rope_tables(pos_rep, inv128, rows):
    n4 = pos_rep.shape[0]
    spec = pl.BlockSpec((rows, LANES), lambda i: (i, 0))
    return pl.pallas_call(
        _rope_table_kernel,
        grid=(n4 // rows,),
        in_specs=[spec, pl.BlockSpec((1, LANES), lambda i: (0, 0))],
        out_specs=[spec, spec],
        out_shape=[jax.ShapeDtypeStruct((n4, LANES), F32)] * 2,
        compiler_params=pltpu.CompilerParams(dimension_semantics=("arbitrary",)),
        name="ropetab",
    )(pos_rep, inv128)


def _inproj_kernel(x_ref, lnw_ref, w_ref, cos_ref, sin_ref, o_ref):
    a = _rms(x_ref[...], lnw_ref[...]).astype(BF16)
    reps = LANES // (D_HEAD // 2)
    cos = jnp.concatenate([cos_ref[...]] * reps, axis=1)
    sin = jnp.concatenate([sin_ref[...]] * reps, axis=1)
    lane = lax.broadcasted_iota(I32, (1, LANES), 1)
    first = (lane % D_HEAD) < (D_HEAD // 2)
    sin_signed = jnp.where(first, -sin, sin)
    n_chunks = IN_W // 256
    value_chunks = (4, 5)
    for c in [c for c in range(n_chunks) if c not in value_chunks] + list(value_chunks):
        p = jnp.dot(a, w_ref[:, c * 256:(c + 1) * 256], preferred_element_type=F32)
        for hh in range(2):
            g = c * 2 + hh
            xg = p[:, hh * LANES:(hh + 1) * LANES]
            is_v = (8 <= g < 12) or g == 17
            if not is_v:
                partner = jnp.where(first, pltpu.roll(xg, 96, 1), pltpu.roll(xg, 32, 1))
                xg = xg * cos + partner * sin_signed
                if g < 4 or 12 <= g < 16:
                    xg = xg * (D_HEAD ** -0.5 * math.log2(math.e))
            o_ref[:, g * LANES:(g + 1) * LANES] = xg.astype(BF16)


def _inproj(x2, ln_w, w_in_bf, cos, sin, tm):
    n, d = x2.shape
    half = D_HEAD // 2
    return pl.pallas_call(
        _inproj_kernel,
        grid=(n // tm,),
        in_specs=[
            pl.BlockSpec((tm, d), lambda i: (i, 0)),
            pl.BlockSpec((1, d), lambda i: (0, 0)),
            pl.BlockSpec((d, IN_W), lambda i: (0, 0)),
            pl.BlockSpec((tm, half), lambda i: (i, 0)),
            pl.BlockSpec((tm, half), lambda i: (i, 0)),
        ],
        out_specs=pl.BlockSpec((tm, IN_W), lambda i: (i, 0)),
        out_shape=jax.ShapeDtypeStruct((n, IN_W), BF16),
        compiler_params=pltpu.CompilerParams(
            dimension_semantics=("arbitrary",), vmem_limit_bytes=VMEM_LIMIT),
        name="inproj",
    )(x2, ln_w, w_in_bf, cos, sin)


def _diff_kernel(q_ref, k_ref, v_ref, lq1_ref, lk1_ref, lq2_ref, lk2_ref, sw_ref, o_ref,
                 vt_scr, acc_scr, m_scr, l_scr, sa_scr, sb_scr, sc_scr, *, t, tk, lambda_init):
    s = k_ref.shape[0]
    per = tk // t
    for c in range(s // tk):
        for r in range(per):
            rows = slice(c * tk + r * t, c * tk + (r + 1) * t)
            vt_scr[c, :, r * t:(r + 1) * t] = v_ref[rows, :].astype(F32).T.astype(BF16)

    lane = lax.broadcasted_iota(I32, (1, LANES), 1)
    lam = (jnp.exp(jnp.sum(lq1_ref[...] * lk1_ref[...], axis=1, keepdims=True))
           - jnp.exp(jnp.sum(lq2_ref[...] * lk2_ref[...], axis=1, keepdims=True))
           + lambda_init)

    def tile_queries(i):
        q = q_ref[pl.ds(pl.multiple_of(i * t, t), t), :]
        zero = jnp.zeros_like(q)
        return jnp.concatenate([jnp.where(lane < D_HEAD, q, zero), jnp.where(lane >= D_HEAD, q, zero)], axis=0)

    def diagonal_scores(i, v):
        rows = (v + 1) * t
        k = k_ref[pl.ds(pl.multiple_of((i // per) * tk, tk), rows), :]
        sc_scr[0:rows, :] = lax.dot_general(k, tile_queries(i), NT_DIMS, preferred_element_type=F32)

    diagonal_scores(0, 0)

    def tile(i, carry):
        q2 = tile_queries(i)
        acc_scr[...] = jnp.zeros(acc_scr.shape, F32)
        m_scr[...] = jnp.full(m_scr.shape, NEG, F32)
        l_scr[...] = jnp.zeros(l_scr.shape, F32)
        n_full = i // per

        def scores(j, buf):
            k = k_ref[pl.ds(pl.multiple_of(j * tk, tk), tk), :]
            buf[...] = lax.dot_general(k, q2, NT_DIMS, preferred_element_type=F32)

        def accumulate(st_parts, vt):
            m_prev = m_scr[...]
            m_new = m_prev
            for st in st_parts:
                m_new = jnp.maximum(m_new, jnp.max(st, axis=0, keepdims=True))
            ps = [jnp.exp2(st - m_new) for st in st_parts]
            alpha = jnp.exp2(m_prev - m_new)
            l_new = alpha * l_scr[...]
            for p in ps:
                l_new = l_new + jnp.sum(p, axis=0, keepdims=True)
            l_scr[...] = l_new
            pb = [p.astype(BF16) for p in ps]
            pb = pb[0] if len(pb) == 1 else jnp.concatenate(pb, axis=0)
            acc_scr[...] = alpha * acc_scr[...] + jnp.dot(vt, pb, preferred_element_type=F32)
            m_scr[...] = m_new

        def full_block(j, buf):
            accumulate([buf[...]], vt_scr[j])

        def diagonal_block(v):
            rows = (v + 1) * t
            scores(0, sa_scr)
            key = lax.broadcasted_iota(I32, (t, 2 * t), 0)
            col = lax.broadcasted_iota(I32, (t, 2 * t), 1)
            diag = jnp.where(key <= jnp.where(col >= t, col - t, col), sc_scr[v * t:rows, :], NEG)
            parts = [diag] if v == 0 else [sc_scr[0:v * t, :], diag]
            accumulate(parts, vt_scr[n_full, :, 0:rows])

        for v in range(per):
            pl.when(i % per == v)(functools.partial(diagonal_block, v))

        def pair(jj, c2):
            scores(2 * jj + 1, sb_scr)
            full_block(2 * jj, sa_scr)
            scores(2 * jj + 2, sa_scr)
            full_block(2 * jj + 1, sb_scr)
            return c2

        lax.fori_loop(0, n_full // 2, pair, 0)

        @pl.when(n_full % 2 == 1)
        def _():
            full_block(n_full - 1, sa_scr)

        def finish(next_variant):
            if next_variant is not None:
                diagonal_scores(i + 1, next_variant)
            on = acc_scr[...] / l_scr[...]
            o = on[:, :t] - lam * on[:, t:]
            ms = jnp.mean(o * o, axis=0, keepdims=True)
            o = o * lax.rsqrt(ms + EPS) * sw_ref[...] * (1.0 - lambda_init)
            o_ref[pl.ds(pl.multiple_of(i * t, t), t), :] = o.T.astype(BF16)

        is_last = i + 1 == s // t
        for v in range(per):
            pl.when(jnp.logical_and(jnp.logical_not(is_last), (i + 1) % per == v))(functools.partial(finish, v))
        pl.when(is_last)(functools.partial(finish, None))
        return carry

    lax.fori_loop(0, s // t, tile, 0)


def _diff_attention(qkv, lq1, lk1, lq2, lk2, subln_col, b, s, t, tk, lambda_init):
    n = qkv.shape[0]
    small = pl.BlockSpec((1, D_HEAD), lambda bi, h: (0, 0))
    return pl.pallas_call(
        functools.partial(_diff_kernel, t=t, tk=tk, lambda_init=lambda_init),
        grid=(b, H_DIFF),
        in_specs=[
            pl.BlockSpec((s, LANES), lambda bi, h: (bi, h)),
            pl.BlockSpec((s, LANES), lambda bi, h: (bi, 4 + h)),
            pl.BlockSpec((s, LANES), lambda bi, h: (bi, 8 + h)),
            small, small, small, small,
            pl.BlockSpec((LANES, 1), lambda bi, h: (0, 0)),
        ],
        out_specs=pl.BlockSpec((s, LANES), lambda bi, h: (bi, h)),
        out_shape=jax.ShapeDtypeStruct((n, H_DIFF * LANES), BF16),
        scratch_shapes=[
            pltpu.VMEM((s // tk, LANES, tk), BF16),
            pltpu.VMEM((LANES, 2 * t), F32),
            pltpu.VMEM((1, 2 * t), F32),
            pltpu.VMEM((1, 2 * t), F32),
            pltpu.VMEM((tk, 2 * t), F32),
            pltpu.VMEM((tk, 2 * t), F32),
            pltpu.VMEM((tk, 2 * t), F32),
        ],
        compiler_params=pltpu.CompilerParams(
            dimension_semantics=("arbitrary", "arbitrary"), vmem_limit_bytes=VMEM_LIMIT),
        name="diffattn",
    )(qkv, qkv, qkv, lq1, lk1, lq2, lk2, subln_col)


def _swa_kernel(sink_ref, q_ref, kc_ref, kp_ref, vc_ref, vp_ref, o_ref, kbuf, vtbuf, *, tq):
    i = pl.program_id(1)
    w = WINDOW
    nh = N_Q_SWA
    kbuf[0:w, :] = kp_ref[...]
    kbuf[w:w + tq, :] = kc_ref[...]
    vtbuf[:, 0:w] = vp_ref[...].astype(F32).T.astype(BF16)
    for r in range(tq // w):
        vtbuf[:, (r + 1) * w:(r + 2) * w] = vc_ref[r * w:(r + 1) * w, :].astype(F32).T.astype(BF16)
    lane = lax.broadcasted_iota(I32, (1, LANES), 1)
    lo = lane < D_HEAD
    ki = lax.broadcasted_iota(I32, (2 * w, nh * w), 0)
    qi = lax.broadcasted_iota(I32, (2 * w, nh * w), 1) % w
    band = (ki > qi) & (ki <= qi + w)
    band_first = band & (ki >= jnp.where(i > 0, 0, w))
    sink = jnp.concatenate([jnp.full((1, w), sink_ref[hb], F32) for hb in range(nh)], axis=1) * math.log2(math.e)
    for r in range(tq // w):
        keys = kbuf[r * w:(r + 2) * w, :]
        vt = vtbuf[:, r * w:(r + 2) * w]
        valid = band_first if r == 0 else band
        qs = []
        for half in range(2):
            for p in range(4):
                qg = q_ref[r * w:(r + 1) * w, p * LANES:(p + 1) * LANES]
                qs.append(jnp.where(lo if half == 0 else ~lo, qg, jnp.zeros_like(qg)))
        qstack = jnp.concatenate(qs, axis=0)
        st = lax.dot_general(keys, qstack, NT_DIMS, preferred_element_type=F32)
        st = jnp.where(valid, st, NEG)
        m = jnp.maximum(jnp.max(st, axis=0, keepdims=True), sink)
        e = jnp.exp2(st - m)
        den = jnp.sum(e, axis=0, keepdims=True) + jnp.exp2(sink - m)
        ot = jnp.dot(vt, e.astype(BF16), preferred_element_type=F32) / den
        for p in range(4):
            both = jnp.concatenate([ot[0:D_HEAD, p * w:(p + 1) * w], ot[D_HEAD:, (4 + p) * w:(5 + p) * w]], axis=0)
            o_ref[r * w:(r + 1) * w, p * LANES:(p + 1) * LANES] = both.T.astype(BF16)


def _swa_attention(qkv, sinks, b, s, tq):
    n = qkv.shape[0]
    nq = s // tq
    per = tq // WINDOW
    prev_map = lambda col: (lambda bi, i, sk: (jnp.maximum(bi * (s // WINDOW) + i * per - 1, 0), col))
    cur_map = lambda col: (lambda bi, i, sk: (bi * nq + i, col))
    return pl.pallas_call(
        functools.partial(_swa_kernel, tq=tq),
        grid_spec=pltpu.PrefetchScalarGridSpec(
            num_scalar_prefetch=1,
            grid=(b, nq),
            in_specs=[
                pl.BlockSpec((tq, 4 * LANES), lambda bi, i, sk: (bi * nq + i, 3)),
                pl.BlockSpec((tq, LANES), cur_map(16)),
                pl.BlockSpec((WINDOW, LANES), prev_map(16)),
                pl.BlockSpec((tq, LANES), cur_map(17)),
                pl.BlockSpec((WINDOW, LANES), prev_map(17)),
            ],
            out_specs=pl.BlockSpec((tq, 4 * LANES), lambda bi, i, sk: (bi * nq + i, 0)),
            scratch_shapes=[pltpu.VMEM((WINDOW + tq, LANES), BF16), pltpu.VMEM((LANES, WINDOW + tq), BF16)],
        ),
        out_shape=jax.ShapeDtypeStruct((n, 4 * LANES), BF16),
        compiler_params=pltpu.CompilerParams(
            dimension_semantics=("arbitrary", "arbitrary"), vmem_limit_bytes=VMEM_LIMIT),
        name="swa",
    )(sinks, qkv, qkv, qkv, qkv, qkv)


def _memkv_kernel(m_ref, lnw_ref, w_ref, o_ref):
    a = _rms(m_ref[...], lnw_ref[...]).astype(BF16)
    o_ref[...] = jnp.dot(a, w_ref[...], preferred_element_type=F32).astype(BF16)


def _memkv(mem2, ln_w, wkv_bf, m_len):
    n, d = mem2.shape
    return pl.pallas_call(
        _memkv_kernel,
        grid=(n // m_len,),
        in_specs=[
            pl.BlockSpec((m_len, d), lambda i: (i, 0)),
            pl.BlockSpec((1, d), lambda i: (0, 0)),
            pl.BlockSpec((d, 2 * d), lambda i: (0, 0)),
        ],
        out_specs=pl.BlockSpec((m_len, 2 * d), lambda i: (i, 0)),
        out_shape=jax.ShapeDtypeStruct((n, 2 * d), BF16),
        compiler_params=pltpu.CompilerParams(
            dimension_semantics=("arbitrary",), vmem_limit_bytes=VMEM_LIMIT),
        name="memkv",
    )(mem2, ln_w, wkv_bf)


def _split_bf16(x):
    hi = x.astype(BF16)
    lo = (x - hi.astype(F32)).astype(BF16)
    return hi, lo


def _post_kernel(x_ref, od_ref, os_ref, wout_ref, lnc_ref, wq_ref, kv_ref, wo_ref, lnm_ref, wr_ref, br_ref,
                 h_ref, f_ref, lpos_ref, gate_ref, tcnt_ref, tbase_ref, base_scr, *, tm, d):
    first_step = (pl.program_id(0) == 0) & (pl.program_id(1) == 0)

    @pl.when(first_step)
    def _():
        base_scr[...] = jnp.zeros(base_scr.shape, F32)

    mix = jnp.concatenate([od_ref[...], os_ref[...]], axis=1)
    h1 = x_ref[...] + jnp.dot(mix, wout_ref[...], preferred_element_type=F32)

    c = _rms(h1, lnc_ref[...]).astype(BF16)
    dc = d // H_CROSS
    q = (jnp.dot(c, wq_ref[...], preferred_element_type=F32) * (dc ** -0.5)).astype(BF16)
    outs = []
    for hd in range(H_CROSS):
        k = kv_ref[:, hd * dc:(hd + 1) * dc]
        v = kv_ref[:, d + hd * dc:d + (hd + 1) * dc]
        s = lax.dot_general(q[:, hd * dc:(hd + 1) * dc], k, NT_DIMS, preferred_element_type=F32)
        e = jnp.exp(s - jnp.max(s, axis=1, keepdims=True))
        den = jnp.sum(e, axis=1, keepdims=True)
        outs.append((jnp.dot(e.astype(BF16), v, preferred_element_type=F32) / den).astype(BF16))
    o = jnp.concatenate(outs, axis=1)
    h2 = h1 + jnp.dot(o, wo_ref[...], preferred_element_type=F32)
    h_ref[...] = h2
    f = _rms(h2, lnm_ref[...])
    f_ref[...] = f.astype(BF16)

    f_hi, f_lo = _split_bf16(f)
    w_hi, w_lo = _split_bf16(wr_ref[...])
    lg = (lax.dot_general(w_hi, f_hi, NT_DIMS, preferred_element_type=F32)
          + lax.dot_general(w_hi, f_lo, NT_DIMS, preferred_element_type=F32)
          + lax.dot_general(w_lo, f_hi, NT_DIMS, preferred_element_type=F32)) + br_ref[...]

    gl = lg[0:N_GROUPS]
    gmax = jnp.max(gl, axis=0, keepdims=True)
    gidx = lax.broadcasted_iota(I32, gl.shape, 0)
    g_sel = jnp.min(jnp.where(gl == gmax, gidx, N_GROUPS), axis=0, keepdims=True)
    g_p = 1.0 / jnp.sum(jnp.exp(gl - gmax), axis=0, keepdims=True)

    e8 = jnp.zeros((E_PER_GROUP, tm), F32)
    for g in range(N_GROUPS):
        e8 = e8 + jnp.where(g_sel == g, lg[8 + g * E_PER_GROUP:8 + (g + 1) * E_PER_GROUP], 0.0)
    ex = jnp.exp(e8 - jnp.max(e8, axis=0, keepdims=True))
    ep = ex / jnp.sum(ex, axis=0, keepdims=True)
    idx8 = lax.broadcasted_iota(I32, ep.shape, 0)
    p1 = jnp.max(ep, axis=0, keepdims=True)
    i1 = jnp.min(jnp.where(ep == p1, idx8, E_PER_GROUP), axis=0, keepdims=True)
    ep2 = jnp.where(idx8 == i1, -1.0, ep)
    p2 = jnp.max(ep2, axis=0, keepdims=True)
    i2 = jnp.min(jnp.where(ep2 == p2, idx8, E_PER_GROUP), axis=0, keepdims=True)
    psum = p1 + p2
    gate_ref[0:1, :] = g_p * (p1 / psum)
    gate_ref[1:2, :] = g_p * (p2 / psum)
    eid1 = g_sel * E_PER_GROUP + i1
    eid2 = g_sel * E_PER_GROUP + i2

    e32 = lax.broadcasted_iota(I32, (N_EXPERTS, tm), 0)
    oh1 = (e32 == eid1).astype(F32)
    oh2 = (e32 == eid2).astype(F32)
    cnt = oh1 + oh2
    tr = lax.broadcasted_iota(I32, (tm, tm), 0)
    tc = lax.broadcasted_iota(I32, (tm, tm), 1)
    upper = jnp.where(tr < tc, 1.0, 0.0).astype(BF16)
    earlier = jnp.dot(cnt.astype(BF16), upper, preferred_element_type=F32)
    seg = jnp.sum(cnt, axis=1, keepdims=True)
    seg = jnp.floor((seg + (SUBLANES - 1)) * (1.0 / SUBLANES)) * SUBLANES
    seg = jnp.broadcast_to(seg, (N_EXPERTS, LANES))
    er = lax.broadcasted_iota(I32, (N_EXPERTS, N_EXPERTS), 0)
    ec = lax.broadcasted_iota(I32, (N_EXPERTS, N_EXPERTS), 1)
    lower = jnp.where(ec < er, 1.0, 0.0).astype(BF16)
    start = jnp.dot(lower, seg.astype(BF16), preferred_element_type=F32)[:, 0:1]
    where = earlier + start
    lpos_ref[0:1, :] = jnp.sum(oh1 * where, axis=0, keepdims=True).astype(I32)
    lpos_ref[1:2, :] = jnp.sum(oh2 * where, axis=0, keepdims=True).astype(I32)
    tcnt_ref[...] = seg
    tbase_ref[...] = base_scr[...]
    base_scr[...] = base_scr[...] + seg


def _post(x2, od, osw, wout_bf, lnc, wq_bf, kv, wo_bf, lnm, wr, br, b, s, tm, m_len):
    n, d = x2.shape
    nt = s // tm
    row = lambda bi, i: (bi * nt + i, 0)
    const = lambda bi, i: (0, 0)
    return pl.pallas_call(
        functools.partial(_post_kernel, tm=tm, d=d),
        grid=(b, nt),
        in_specs=[
            pl.BlockSpec((tm, d), row),
            pl.BlockSpec((tm, d // 2), row),
            pl.BlockSpec((tm, d // 2), row),
            pl.BlockSpec((d, d), const),
            pl.BlockSpec((1, d), const),
            pl.BlockSpec((d, d), const),
            pl.BlockSpec((m_len, 2 * d), lambda bi, i: (bi, 0)),
            pl.BlockSpec((d, d), const),
            pl.BlockSpec((1, d), const),
            pl.BlockSpec((8 + N_EXPERTS, d), const),
            pl.BlockSpec((8 + N_EXPERTS, 1), const),
        ],
        out_specs=[
            pl.BlockSpec((tm, d), row),
            pl.BlockSpec((tm, d), row),
            pl.BlockSpec((2, tm), lambda bi, i: (0, bi * nt + i)),
            pl.BlockSpec((2, tm), lambda bi, i: (0, bi * nt + i)),
            pl.BlockSpec((None, N_EXPERTS, LANES), lambda bi, i: (bi * nt + i, 0, 0)),
            pl.BlockSpec((None, N_EXPERTS, LANES), lambda bi, i: (bi * nt + i, 0, 0)),
        ],
        out_shape=[
            jax.ShapeDtypeStruct((n, d), F32),
            jax.ShapeDtypeStruct((n, d), BF16),
            jax.ShapeDtypeStruct((2, n), I32),
            jax.ShapeDtypeStruct((2, n), F32),
            jax.ShapeDtypeStruct((n // tm, N_EXPERTS, LANES), F32),
            jax.ShapeDtypeStruct((n // tm, N_EXPERTS, LANES), F32),
        ],
        scratch_shapes=[pltpu.VMEM((N_EXPERTS, LANES), F32)],
        compiler_params=pltpu.CompilerParams(
            dimension_semantics=("arbitrary", "arbitrary"), vmem_limit_bytes=VMEM_LIMIT),
        name="post",
    )(x2, od, osw, wout_bf, lnc, wq_bf, kv, wo_bf, lnm, wr, br)


def _local_rows(tm):
    return 2 * tm + SUBLANES * N_EXPERTS


def _segment_copies(tile, tcnt_ref, tbase_ref, pstart_ref, make, act, max_len):
    sizes = [SUBLANES << k for k in reversed(range((max_len // SUBLANES).bit_length()))]

    if act == "wait":
        total = lax.fori_loop(0, N_EXPERTS, lambda e, acc: acc + tcnt_ref[tile * N_EXPERTS + e], 0)
        for size in sizes:
            @pl.when((total & size) != 0)
            def _():
                make(0, 0, size).wait()

        return total

    def pieces(ln, local0, glob0, some_sizes):
        for size in some_sizes:
            @pl.when((ln & size) != 0)
            def _():
                off = (ln // (2 * size)) * (2 * size)
                make(pl.multiple_of(local0 + off, SUBLANES), pl.multiple_of(glob0 + off, SUBLANES), size).start()

    big = [s for s in sizes if s >= LONG_SEGMENT_ROWS]
    small = [s for s in sizes if s < LONG_SEGMENT_ROWS]

    def per_expert(e, local0):
        ln = tcnt_ref[tile * N_EXPERTS + e]
        glob0 = pstart_ref[e] + tbase_ref[tile * N_EXPERTS + e]

        @pl.when(ln >= LONG_SEGMENT_ROWS)
        def _():
            pieces(ln, local0, glob0, big)

        pieces(ln, local0, glob0, small)
        return local0 + ln

    return lax.fori_loop(0, N_EXPERTS, per_expert, 0)


def _dispatch_kernel(tcnt_ref, tbase_ref, tot_ref, f_ref, lpos_ref, xs_hbm, pstart_ref, blk_ref, nact_ref,
                     loc_scr, zero_scr, sem, zsem, *, tm, bm, nblk):
    i = pl.program_id(0)
    nt = pl.num_programs(0)
    slot = i % 2

    @pl.when(i == 0)
    def _():
        def per_expert(e, blk0):
            nb = (tot_ref[e] + (bm - 1)) // bm
            pstart_ref[e] = blk0 * bm

            def fill(j, carry):
                blk_ref[j] = e
                return carry

            lax.fori_loop(blk0, blk0 + nb, fill, 0)
            return blk0 + nb

        nact = lax.fori_loop(0, N_EXPERTS, per_expert, 0)
        nact_ref[0] = nact

        def tail(j, carry):
            blk_ref[j] = N_EXPERTS - 1
            return carry

        lax.fori_loop(nact, nblk, tail, 0)

        zero_scr[...] = jnp.zeros(zero_scr.shape, U32)

        def zero_fill(act):
            def do(copy):
                if act == "start":
                    copy.start()
                else:
                    copy.wait()

            def per_expert_pad(e, carry):
                tot = tot_ref[e]
                pad = (tot + (bm - 1)) // bm * bm - tot
                row0 = pstart_ref[e] + tot
                for k in reversed(range((bm // SUBLANES).bit_length())):
                    size = SUBLANES << k

                    @pl.when((pad & size) != 0)
                    def _():
                        row = pl.multiple_of(row0 + (pad // (2 * size)) * (2 * size), SUBLANES)
                        do(pltpu.make_async_copy(zero_scr.at[pl.ds(0, size)], xs_hbm.at[pl.ds(row, size)], zsem))

                return carry

            lax.fori_loop(0, N_EXPERTS, per_expert_pad, 0)

            def per_unused_block(j, carry):
                do(pltpu.make_async_copy(zero_scr, xs_hbm.at[pl.ds(pl.multiple_of(j * bm, bm), bm)], zsem))
                return carry

            lax.fori_loop(nact, nblk, per_unused_block, 0)

        zero_fill("start")
        zero_fill("wait")

    def copies(tile, sl, act):
        def make(lrow, grow, size):
            return pltpu.make_async_copy(loc_scr.at[sl, pl.ds(lrow, size)], xs_hbm.at[pl.ds(grow, size)], sem.at[sl])

        _segment_copies(tile, tcnt_ref, tbase_ref, pstart_ref, make, act, _local_rows(tm))

    @pl.when(i >= 2)
    def _():
        copies(i - 2, slot, "wait")

    pos = lax.broadcasted_iota(I32, (_local_rows(tm), tm), 0)
    onehot = jnp.where(pos == lpos_ref[0:1, :], 1.0, jnp.where(pos == lpos_ref[1:2, :], 1.0, 0.0)).astype(BF16)
    loc_scr[slot] = _pack_pairs(jnp.dot(onehot, f_ref[...], preferred_element_type=F32))
    copies(i, slot, "start")

    @pl.when(i == nt - 1)
    def _():
        @pl.when(i >= 1)
        def _():
            copies(i - 1, 1 - slot, "wait")

        copies(i, slot, "wait")


def _dispatch(tcnt_i, tbase_i, tot_i, f, lpos, tm, bm, nblk):
    n, d = f.shape
    smem = pl.BlockSpec(memory_space=pltpu.SMEM)
    hbm = pl.BlockSpec(memory_space=pl.ANY)
    return pl.pallas_call(
        functools.partial(_dispatch_kernel, tm=tm, bm=bm, nblk=nblk),
        grid_spec=pltpu.PrefetchScalarGridSpec(
            num_scalar_prefetch=3,
            grid=(n // tm,),
            in_specs=[
                pl.BlockSpec((tm, d), lambda i, *_: (i, 0)),
                pl.BlockSpec((2, tm), lambda i, *_: (0, i)),
            ],
            out_specs=[hbm, smem, smem, smem],
            scratch_shapes=[
                pltpu.VMEM((2, _local_rows(tm), d // 2), U32),
                pltpu.VMEM((bm, d // 2), U32),
                pltpu.SemaphoreType.DMA((2,)),
                pltpu.SemaphoreType.DMA,
            ],
        ),
        out_shape=[
            jax.ShapeDtypeStruct((nblk * bm, d // 2), U32),
            jax.ShapeDtypeStruct((N_EXPERTS,), I32),
            jax.ShapeDtypeStruct((nblk,), I32),
            jax.ShapeDtypeStruct((1,), I32),
        ],
        compiler_params=pltpu.CompilerParams(
            dimension_semantics=("arbitrary",), vmem_limit_bytes=VMEM_LIMIT),
        name="dispatch",
    )(tcnt_i, tbase_i, tot_i, f, lpos)


def _expert_kernel(blk_ref, nact_ref, x_ref, wg_ref, wu_ref, wd_ref, y_ref, wg_bf, wu_bf, wd_bf):
    j = pl.program_id(0)
    active = j < nact_ref[0]

    @pl.when(active & ((j == 0) | (blk_ref[j] != blk_ref[jnp.maximum(j - 1, 0)])))
    def _():
        wg_bf[...] = wg_ref[...].astype(BF16)
        wu_bf[...] = wu_ref[...].astype(BF16)
        wd_bf[...] = wd_ref[...].astype(BF16)

    @pl.when(active)
    def _():
        x = _unpack_pairs(x_ref[...])
        g = jnp.dot(x, wg_bf[...], preferred_element_type=F32)
        u = jnp.dot(x, wu_bf[...], preferred_element_type=F32)
        hdn = (g * jax.nn.sigmoid(g) * u).astype(BF16)
        y = jnp.dot(hdn, wd_bf[...], preferred_element_type=F32)
        y_ref[...] = _pack_pairs(y.astype(BF16).astype(F32))

    @pl.when(j >= nact_ref[0])
    def _():
        y_ref[...] = jnp.zeros(y_ref.shape, y_ref.dtype)


def _experts(blk_e, nact, xs, wg, wu, wd, bm):
    p, dh = xs.shape
    d, dff = wg.shape[-2:]
    nblk = p // bm
    rowmap = lambda j, blk, na: (jnp.minimum(j, na[0] - 1), 0)
    wmap = lambda j, blk, na: (blk[jnp.minimum(j, na[0] - 1)], 0, 0)
    return pl.pallas_call(
        _expert_kernel,
        grid_spec=pltpu.PrefetchScalarGridSpec(
            num_scalar_prefetch=2,
            grid=(nblk,),
            in_specs=[
                pl.BlockSpec((bm, dh), rowmap),
                pl.BlockSpec((None, d, dff), wmap),
                pl.BlockSpec((None, d, dff), wmap),
                pl.BlockSpec((None, dff, d), wmap),
            ],
            out_specs=pl.BlockSpec((bm, dh), lambda j, blk, na: (j, 0)),
            scratch_shapes=[pltpu.VMEM((d, dff), BF16), pltpu.VMEM((d, dff), BF16), pltpu.VMEM((dff, d), BF16)],
        ),
        out_shape=jax.ShapeDtypeStruct((p, dh), U32),
        compiler_params=pltpu.CompilerParams(
            dimension_semantics=("arbitrary",), vmem_limit_bytes=VMEM_LIMIT),
        name="experts",
    )(blk_e, nact, xs, wg, wu, wd)


def _combine_kernel(tcnt_ref, tbase_ref, pstart_ref, h_ref, lpos_ref, gate_ref, lnf_ref, ys_hbm, o_ref,
                    loc_scr, sem, *, tm):
    i = pl.program_id(0)
    nt = pl.num_programs(0)
    slot = i % 2

    def copies(tile, sl, act):
        def make(lrow, grow, size):
            return pltpu.make_async_copy(ys_hbm.at[pl.ds(grow, size)], loc_scr.at[sl, pl.ds(lrow, size)], sem.at[sl])

        return _segment_copies(tile, tcnt_ref, tbase_ref, pstart_ref, make, act, _local_rows(tm))

    @pl.when(i == 0)
    def _():
        copies(0, 0, "start")

    @pl.when(i + 1 < nt)
    def _():
        copies(i + 1, 1 - slot, "start")

    used = copies(i, slot, "wait")
    row = lax.broadcasted_iota(I32, (_local_rows(tm), 1), 0)
    ys = _unpack_pairs(jnp.where(row < used, loc_scr[slot], U32(0)))

    r8 = lax.broadcasted_iota(I32, (8, tm), 0)
    lp = lpos_ref[...].astype(F32)
    top = jnp.where(r8 == 0, lp[0:1], jnp.where(r8 == 1, lp[1:2], jnp.where(
        r8 == 2, gate_ref[0:1, :], jnp.where(r8 == 3, gate_ref[1:2, :], 0.0))))
    cols = jnp.concatenate([top, jnp.zeros((LANES - 8, tm), F32)], axis=0).T
    pos = lax.broadcasted_iota(I32, (tm, _local_rows(tm)), 1)
    acc = h_ref[...]
    for k in range(2):
        sel = jnp.where(pos == cols[:, k:k + 1].astype(I32), 1.0, 0.0).astype(BF16)
        acc = acc + cols[:, 2 + k:3 + k] * jnp.dot(sel, ys, preferred_element_type=F32)
    o_ref[...] = _rms(acc, lnf_ref[...])


def _combine(tcnt_i, tbase_i, pstart, h2, lpos, gates, lnf, ys, tm):
    n, d = h2.shape
    return pl.pallas_call(
        functools.partial(_combine_kernel, tm=tm),
        grid_spec=pltpu.PrefetchScalarGridSpec(
            num_scalar_prefetch=3,
            grid=(n // tm,),
            in_specs=[
                pl.BlockSpec((tm, d), lambda i, *_: (i, 0)),
                pl.BlockSpec((2, tm), lambda i, *_: (0, i)),
                pl.BlockSpec((2, tm), lambda i, *_: (0, i)),
                pl.BlockSpec((1, d), lambda i, *_: (0, 0)),
                pl.BlockSpec(memory_space=pl.ANY),
            ],
            out_specs=pl.BlockSpec((tm, d), lambda i, *_: (i, 0)),
            scratch_shapes=[pltpu.VMEM((2, _local_rows(tm), d // 2), U32), pltpu.SemaphoreType.DMA((2,))],
        ),
        out_shape=jax.ShapeDtypeStruct((n, d), F32),
        compiler_params=pltpu.CompilerParams(
            dimension_semantics=("arbitrary",), vmem_limit_bytes=VMEM_LIMIT),
        name="combine",
    )(tcnt_i, tbase_i, pstart, h2, lpos, gates, lnf, ys)


def _swa_head_perm():
    cols = []
    for p in range(4):
        for half in range(2):
            head = half * 4 + p
            cols.extend(range(head * D_HEAD, (head + 1) * D_HEAD))
    return jnp.asarray(cols, dtype=I32)


def kernel(x, mem, positions, ln_mix_w, w_in, lambda_q1, lambda_k1, lambda_q2, lambda_k2, subln_w, sinks,
           w_out, ln_cross_w, ln_mem_w, wq_cross, wkv_cross, wo_cross, ln_moe_w, w_group, b_group,
           w_expert, b_expert, w_gate, w_up, w_down, ln_final_w):
    b, s, d = x.shape
    m_len = mem.shape[1]
    n = b * s
    assert w_in.shape[0] == 1 and d == 1024 and n <= 65536
    lambda_init = 0.8 - 0.6 * math.exp(-0.3 * 0)

    tm = 512
    t_attn = 256
    tk_attn = 1024
    bm = 512

    x2 = x.reshape(n, d)
    half = D_HEAD // 2
    per_row = LANES // half
    inv_freq = jnp.exp(-math.log(ROPE_THETA) * jnp.arange(0, D_HEAD, 2, dtype=F32) / D_HEAD)
    inv128 = jnp.tile(inv_freq, per_row).reshape(1, LANES)
    pos_rep = jnp.repeat(positions.reshape(n // per_row, per_row).astype(I32), half, axis=1)
    cos_t, sin_t = _rope_tables(pos_rep, inv128, min(1024, n // per_row))
    cos_t = cos_t.reshape(n, half)
    sin_t = sin_t.reshape(n, half)

    perm = _swa_head_perm()
    sq0 = 3 * 512
    w_in_l = w_in[0]
    w_in_p = jnp.concatenate([w_in_l[:, :sq0], w_in_l[:, sq0:sq0 + 512][:, perm], w_in_l[:, sq0 + 512:]], axis=1)
    w_out_l = w_out[0]
    w_out_p = jnp.concatenate([w_out_l[:512], w_out_l[512:][perm]], axis=0)
    sinks_p = sinks[0].reshape(2, 4).reshape(-1)

    qkv = _inproj(x2, ln_mix_w[0].reshape(1, d), w_in_p.astype(BF16), cos_t, sin_t, tm)
    o_diff = _diff_attention(qkv, lambda_q1[0].reshape(1, -1), lambda_k1[0].reshape(1, -1),
                             lambda_q2[0].reshape(1, -1), lambda_k2[0].reshape(1, -1),
                             subln_w[0].reshape(-1, 1), b, s, t_attn, min(tk_attn, s), lambda_init)
    o_swa = _swa_attention(qkv, sinks_p.astype(F32), b, s, tm)
    kv = _memkv(mem.reshape(b * m_len, d), ln_mem_w[0].reshape(1, d), wkv_cross[0].astype(BF16), m_len)

    wr = jnp.concatenate([w_group[0].T, jnp.zeros((8 - N_GROUPS, d), F32), w_expert[0].T], axis=0)
    br = jnp.concatenate([b_group[0], jnp.zeros((8 - N_GROUPS,), F32), b_expert[0]]).reshape(-1, 1)
    h2, f, lpos, gates, tcnt, tbase = _post(
        x2, o_diff, o_swa, w_out_p.astype(BF16), ln_cross_w[0].reshape(1, d), wq_cross[0].astype(BF16), kv,
        wo_cross[0].astype(BF16), ln_moe_w[0].reshape(1, d), wr, br, b, s, tm, m_len)

    tcnt_i = tcnt[:, :, 0].astype(I32).reshape(-1)
    tbase_i = tbase[:, :, 0].astype(I32).reshape(-1)
    tot_i = tbase_i[-N_EXPERTS:] + tcnt_i[-N_EXPERTS:]
    rows_max = 2 * n + (n // tm) * N_EXPERTS * (SUBLANES - 1)
    nblk = (rows_max + N_EXPERTS * (bm - 1) + bm - 1) // bm
    xs, pstart, blk_e, nact = _dispatch(tcnt_i, tbase_i, tot_i, f, lpos, tm, bm, nblk)
    ys = _experts(blk_e, nact, xs, w_gate[0], w_up[0], w_down[0], bm)
    out = _combine(tcnt_i, tbase_i, pstart, h2, lpos, gates, ln_final_w.reshape(1, d), ys, tm)
    return out.reshape(b, s, d)
```

```python
import functools
import math

import jax
import jax.numpy as jnp
from jax import lax
from jax.experimental import pallas as pl
from jax.experimental.pallas import tpu as pltpu

F32 = jnp.float32
BF16 = jnp.bfloat16
I32 = jnp.int32

D_HEAD = 64
ROPE_THETA = 10000.0
H_DIFF = 4
N_Q_SWA = 8
N_KV_SWA = 2
WINDOW = 128
H_CROSS = 4
N_GROUPS = 4
E_PER_GROUP = 8
N_EXPERTS = N_GROUPS * E_PER_GROUP
EPS = 1e-6
LANES = 128
SUBLANES = 8
LONG_SEGMENT_ROWS = 128
IN_W = 2304
NEG = -1e30

VMEM_LIMIT = 56 * 1024 * 1024

NT_DIMS = (((1,), (1,)), ((), ()))


U32 = jnp.uint32


def _pack_pairs(x):
    h = x.shape[1] // 2
    hi = lax.bitcast_convert_type(x[:, :h], U32)
    lo = lax.bitcast_convert_type(x[:, h:], U32)
    return hi | (lo >> 16)


def _unpack_pairs(p):
    a = lax.bitcast_convert_type(p & U32(0xFFFF0000), F32)
    b = lax.bitcast_convert_type(p << 16, F32)
    return jnp.concatenate([a, b], axis=1).astype(BF16)


def _rms(x, w):
    ms = jnp.mean(x * x, axis=-1, keepdims=True)
    return x * lax.rsqrt(ms + EPS) * w


def _rope_table_kernel(pos_ref, inv_ref, cos_ref, sin_ref):
    ang = pos_ref[...].astype(F32) * inv_ref[...]
    cos_ref[...] = jnp.cos(ang)
    sin_ref[...] = jnp.sin(ang)


def _rope_tables(pos_rep, inv128, rows):
    n4 = pos_rep.shape[0]
    spec = pl.BlockSpec((rows, LANES), lambda i: (i, 0))
    return pl.pallas_call(
        _rope_table_kernel,
        grid=(n4 // rows,),
        in_specs=[spec, pl.BlockSpec((1, LANES), lambda i: (0, 0))],
        out_specs=[spec, spec],
        out_shape=[jax.ShapeDtypeStruct((n4, LANES), F32)] * 2,
        compiler_params=pltpu.CompilerParams(dimension_semantics=("arbitrary",)),
        name="ropetab",
    )(pos_rep, inv128)


def _inproj_kernel(x_ref, lnw_ref, w_ref, cos_ref, sin_ref, o_ref):
    a = _rms(x_ref[...], lnw_ref[...]).astype(BF16)
    reps = LANES // (D_HEAD // 2)
    cos = jnp.concatenate([cos_ref[...]] * reps, axis=1)
    sin = jnp.concatenate([sin_ref[...]] * reps, axis=1)
    lane = lax.broadcasted_iota(I32, (1, LANES), 1)
    first = (lane % D_HEAD) < (D_HEAD // 2)
    sin_signed = jnp.where(first, -sin, sin)
    n_chunks = IN_W // 256
    value_chunks = (4, 5)
    for c in [c for c in range(n_chunks) if c not in value_chunks] + list(value_chunks):
        p = jnp.dot(a, w_ref[:, c * 256:(c + 1) * 256], preferred_element_type=F32)
        for hh in range(2):
            g = c * 2 + hh
            xg = p[:, hh * LANES:(hh + 1) * LANES]
            is_v = (8 <= g < 12) or g == 17
            if not is_v:
                partner = jnp.where(first, pltpu.roll(xg, 96, 1), pltpu.roll(xg, 32, 1))
                xg = xg * cos + partner * sin_signed
                if g < 4:
                    xg = xg * (D_HEAD ** -0.5 * math.log2(math.e))
                elif 12 <= g < 16:
                    xg = xg * (D_HEAD ** -0.5)
            o_ref[:, g * LANES:(g + 1) * LANES] = xg.astype(BF16)


def _inproj(x2, ln_w, w_in_bf, cos, sin, tm):
    n, d = x2.shape
    half = D_HEAD // 2
    return pl.pallas_call(
        _inproj_kernel,
        grid=(n // tm,),
        in_specs=[
            pl.BlockSpec((tm, d), lambda i: (i, 0)),
            pl.BlockSpec((1, d), lambda i: (0, 0)),
            pl.BlockSpec((d, IN_W), lambda i: (0, 0)),
            pl.BlockSpec((tm, half), lambda i: (i, 0)),
            pl.BlockSpec((tm, half), lambda i: (i, 0)),
        ],
        out_specs=pl.BlockSpec((tm, IN_W), lambda i: (i, 0)),
        out_shape=jax.ShapeDtypeStruct((n, IN_W), BF16),
        compiler_params=pltpu.CompilerParams(
            dimension_semantics=("arbitrary",), vmem_limit_bytes=VMEM_LIMIT),
        name="inproj",
    )(x2, ln_w, w_in_bf, cos, sin)


def _diff_kernel(q_ref, k_ref, v_ref, lq1_ref, lk1_ref, lq2_ref, lk2_ref, sw_ref, o_ref,
                 vt_scr, acc_scr, m_scr, l_scr, sa_scr, sb_scr, sc_scr, *, t, tk, lambda_init):
    s = k_ref.shape[0]
    per = tk // t
    for c in range(s // tk):
        for r in range(per):
            rows = slice(c * tk + r * t, c * tk + (r + 1) * t)
            vt_scr[c, :, r * t:(r + 1) * t] = v_ref[rows, :].astype(F32).T.astype(BF16)

    lane = lax.broadcasted_iota(I32, (1, LANES), 1)
    lam = (jnp.exp(jnp.sum(lq1_ref[...] * lk1_ref[...], axis=1, keepdims=True))
           - jnp.exp(jnp.sum(lq2_ref[...] * lk2_ref[...], axis=1, keepdims=True))
           + lambda_init)

    def tile(i, carry):
        q = q_ref[pl.ds(pl.multiple_of(i * t, t), t), :]
        zero = jnp.zeros_like(q)
        q2 = jnp.concatenate([jnp.where(lane < D_HEAD, q, zero), jnp.where(lane >= D_HEAD, q, zero)], axis=0)
        acc_scr[...] = jnp.zeros(acc_scr.shape, F32)
        m_scr[...] = jnp.full(m_scr.shape, NEG, F32)
        l_scr[...] = jnp.zeros(l_scr.shape, F32)
        n_full = i // per

        def scores(j, buf):
            k = k_ref[pl.ds(pl.multiple_of(j * tk, tk), tk), :]
            buf[...] = lax.dot_general(k, q2, NT_DIMS, preferred_element_type=F32)

        def accumulate(st_parts, vt):
            m_prev = m_scr[...]
            m_new = m_prev
            for st in st_parts:
                m_new = jnp.maximum(m_new, jnp.max(st, axis=0, keepdims=True))
            ps = [jnp.exp2(st - m_new) for st in st_parts]
            alpha = jnp.exp2(m_prev - m_new)
            l_new = alpha * l_scr[...]
            for p in ps:
                l_new = l_new + jnp.sum(p, axis=0, keepdims=True)
            l_scr[...] = l_new
            pb = [p.astype(BF16) for p in ps]
            pb = pb[0] if len(pb) == 1 else jnp.concatenate(pb, axis=0)
            acc_scr[...] = alpha * acc_scr[...] + jnp.dot(vt, pb, preferred_element_type=F32)
            m_scr[...] = m_new

        def full_block(j, buf):
            accumulate([buf[...]], vt_scr[j])

        def diagonal_block(v):
            rows = (v + 1) * t
            k = k_ref[pl.ds(pl.multiple_of(n_full * tk, tk), rows), :]
            sc_scr[0:rows, :] = lax.dot_general(k, q2, NT_DIMS, preferred_element_type=F32)
            scores(0, sa_scr)
            key = lax.broadcasted_iota(I32, (t, 2 * t), 0)
            col = lax.broadcasted_iota(I32, (t, 2 * t), 1)
            diag = jnp.where(key <= jnp.where(col >= t, col - t, col), sc_scr[v * t:rows, :], NEG)
            parts = [diag] if v == 0 else [sc_scr[0:v * t, :], diag]
            accumulate(parts, vt_scr[n_full, :, 0:rows])

        for v in range(per):
            pl.when(i % per == v)(functools.partial(diagonal_block, v))

        def pair(jj, c2):
            scores(2 * jj + 1, sb_scr)
            full_block(2 * jj, sa_scr)
            scores(2 * jj + 2, sa_scr)
            full_block(2 * jj + 1, sb_scr)
            return c2

        lax.fori_loop(0, n_full // 2, pair, 0)

        @pl.when(n_full % 2 == 1)
        def _():
            full_block(n_full - 1, sa_scr)

        on = acc_scr[...] / l_scr[...]
        o = on[:, :t] - lam * on[:, t:]
        ms = jnp.mean(o * o, axis=0, keepdims=True)
        o = o * lax.rsqrt(ms + EPS) * sw_ref[...] * (1.0 - lambda_init)
        o_ref[pl.ds(pl.multiple_of(i * t, t), t), :] = o.T.astype(BF16)
        return carry

    lax.fori_loop(0, s // t, tile, 0)


def _diff_attention(qkv, lq1, lk1, lq2, lk2, subln_col, b, s, t, tk, lambda_init):
    n = qkv.shape[0]
    small = pl.BlockSpec((1, D_HEAD), lambda bi, h: (0, 0))
    return pl.pallas_call(
        functools.partial(_diff_kernel, t=t, tk=tk, lambda_init=lambda_init),
        grid=(b, H_DIFF),
        in_specs=[
            pl.BlockSpec((s, LANES), lambda bi, h: (bi, h)),
            pl.BlockSpec((s, LANES), lambda bi, h: (bi, 4 + h)),
            pl.BlockSpec((s, LANES), lambda bi, h: (bi, 8 + h)),
            small, small, small, small,
            pl.BlockSpec((LANES, 1), lambda bi, h: (0, 0)),
        ],
        out_specs=pl.BlockSpec((s, LANES), lambda bi, h: (bi, h)),
        out_shape=jax.ShapeDtypeStruct((n, H_DIFF * LANES), BF16),
        scratch_shapes=[
            pltpu.VMEM((s // tk, LANES, tk), BF16),
            pltpu.VMEM((LANES, 2 * t), F32),
            pltpu.VMEM((1, 2 * t), F32),
            pltpu.VMEM((1, 2 * t), F32),
            pltpu.VMEM((tk, 2 * t), F32),
            pltpu.VMEM((tk, 2 * t), F32),
            pltpu.VMEM((tk, 2 * t), F32),
        ],
        compiler_params=pltpu.CompilerParams(
            dimension_semantics=("arbitrary", "arbitrary"), vmem_limit_bytes=VMEM_LIMIT),
        name="diffattn",
    )(qkv, qkv, qkv, lq1, lk1, lq2, lk2, subln_col)


def _swa_kernel(sink_ref, q_ref, kc_ref, kp_ref, vc_ref, vp_ref, o_ref, kbuf, vbuf, *, tq):
    i = pl.program_id(1)
    w = WINDOW
    kbuf[0:w, :] = kp_ref[...]
    kbuf[w:w + tq, :] = kc_ref[...]
    vbuf[0:w, :] = vp_ref[...]
    vbuf[w:w + tq, :] = vc_ref[...]
    lane = lax.broadcasted_iota(I32, (1, LANES), 1)
    lo = lane < D_HEAD
    qi = lax.broadcasted_iota(I32, (w, 2 * w), 0)
    ki = lax.broadcasted_iota(I32, (w, 2 * w), 1)
    band = (ki > qi) & (ki <= qi + w)
    band_first = band & (ki >= jnp.where(i > 0, 0, w))
    for r in range(tq // w):
        keys = kbuf[r * w:(r + 2) * w, :]
        vals = vbuf[r * w:(r + 2) * w, :]
        valid = band_first if r == 0 else band
        qs = []
        for half in range(2):
            for p in range(4):
                qg = q_ref[r * w:(r + 1) * w, p * LANES:(p + 1) * LANES]
                qs.append(jnp.where(lo if half == 0 else ~lo, qg, jnp.zeros_like(qg)))
        qstack = jnp.concatenate(qs, axis=0)
        s_all = lax.dot_general(qstack, keys, NT_DIMS, preferred_element_type=F32)
        ps = []
        for hb in range(8):
            sink = sink_ref[hb]
            s = jnp.where(valid, s_all[hb * w:(hb + 1) * w], NEG)
            m = jnp.maximum(jnp.max(s, axis=1, keepdims=True), sink)
            e = jnp.exp(s - m)
            den = jnp.sum(e, axis=1, keepdims=True) + jnp.exp(sink - m)
            ps.append((e / den).astype(BF16))
        pv = jnp.dot(jnp.concatenate(ps, axis=0), vals, preferred_element_type=F32)
        for p in range(4):
            og = jnp.where(lo, pv[p * w:(p + 1) * w], pv[(4 + p) * w:(5 + p) * w])
            o_ref[r * w:(r + 1) * w, p * LANES:(p + 1) * LANES] = og.astype(BF16)


def _swa_attention(qkv, sinks, b, s, tq):
    n = qkv.shape[0]
    nq = s // tq
    per = tq // WINDOW
    prev_map = lambda col: (lambda bi, i, sk: (jnp.maximum(bi * (s // WINDOW) + i * per - 1, 0), col))
    cur_map = lambda col: (lambda bi, i, sk: (bi * nq + i, col))
    return pl.pallas_call(
        functools.partial(_swa_kernel, tq=tq),
        grid_spec=pltpu.PrefetchScalarGridSpec(
            num_scalar_prefetch=1,
            grid=(b, nq),
            in_specs=[
                pl.BlockSpec((tq, 4 * LANES), lambda bi, i, sk: (bi * nq + i, 3)),
                pl.BlockSpec((tq, LANES), cur_map(16)),
                pl.BlockSpec((WINDOW, LANES), prev_map(16)),
                pl.BlockSpec((tq, LANES), cur_map(17)),
                pl.BlockSpec((WINDOW, LANES), prev_map(17)),
            ],
            out_specs=pl.BlockSpec((tq, 4 * LANES), lambda bi, i, sk: (bi * nq + i, 0)),
            scratch_shapes=[pltpu.VMEM((WINDOW + tq, LANES), BF16)] * 2,
        ),
        out_shape=jax.ShapeDtypeStruct((n, 4 * LANES), BF16),
        compiler_params=pltpu.CompilerParams(
            dimension_semantics=("arbitrary", "arbitrary"), vmem_limit_bytes=VMEM_LIMIT),
        name="swa",
    )(sinks, qkv, qkv, qkv, qkv, qkv)


def _memkv_kernel(m_ref, lnw_ref, w_ref, o_ref):
    a = _rms(m_ref[...], lnw_ref[...]).astype(BF16)
    o_ref[...] = jnp.dot(a, w_ref[...], preferred_element_type=F32).astype(BF16)


def _memkv(mem2, ln_w, wkv_bf, m_len):
    n, d = mem2.shape
    return pl.pallas_call(
        _memkv_kernel,
        grid=(n // m_len,),
        in_specs=[
            pl.BlockSpec((m_len, d), lambda i: (i, 0)),
            pl.BlockSpec((1, d), lambda i: (0, 0)),
            pl.BlockSpec((d, 2 * d), lambda i: (0, 0)),
        ],
        out_specs=pl.BlockSpec((m_len, 2 * d), lambda i: (i, 0)),
        out_shape=jax.ShapeDtypeStruct((n, 2 * d), BF16),
        compiler_params=pltpu.CompilerParams(
            dimension_semantics=("arbitrary",), vmem_limit_bytes=VMEM_LIMIT),
        name="memkv",
    )(mem2, ln_w, wkv_bf)


def _split_bf16(x):
    hi = x.astype(BF16)
    lo = (x - hi.astype(F32)).astype(BF16)
    return hi, lo


def _post_kernel(x_ref, od_ref, os_ref, wout_ref, lnc_ref, wq_ref, kv_ref, wo_ref, lnm_ref, wr_ref, br_ref,
                 h_ref, f_ref, lpos_ref, gate_ref, tcnt_ref, tbase_ref, base_scr, fprev_scr, *, tm, d):
    g = pl.program_id(0)

    @pl.when(g == 0)
    def _():
        base_scr[...] = jnp.zeros(base_scr.shape, F32)
        fprev_scr[...] = jnp.zeros(fprev_scr.shape, F32)

    mix = jnp.concatenate([od_ref[...], os_ref[...]], axis=1)
    h1 = x_ref[...] + jnp.dot(mix, wout_ref[...], preferred_element_type=F32)
    logits = _route_logits(fprev_scr[...], wr_ref, br_ref)

    c = _rms(h1, lnc_ref[...]).astype(BF16)
    dc = d // H_CROSS
    q = (jnp.dot(c, wq_ref[...], preferred_element_type=F32) * (dc ** -0.5)).astype(BF16)
    eid1, eid2 = _route_select(logits, gate_ref, tm)
    outs = []
    for hd in range(H_CROSS):
        k = kv_ref[:, hd * dc:(hd + 1) * dc]
        v = kv_ref[:, d + hd * dc:d + (hd + 1) * dc]
        s = lax.dot_general(q[:, hd * dc:(hd + 1) * dc], k, NT_DIMS, preferred_element_type=F32)
        e = jnp.exp(s - jnp.max(s, axis=1, keepdims=True))
        den = jnp.sum(e, axis=1, keepdims=True)
        outs.append((jnp.dot(e.astype(BF16), v, preferred_element_type=F32) / den).astype(BF16))
    o = jnp.concatenate(outs, axis=1)
    h2 = h1 + jnp.dot(o, wo_ref[...], preferred_element_type=F32)
    _route_positions(eid1, eid2, jnp.where(g > 0, 1.0, 0.0), lpos_ref, tcnt_ref, tbase_ref, base_scr, tm)
    h_ref[...] = h2
    f = _rms(h2, lnm_ref[...])
    f_ref[...] = f.astype(BF16)
    fprev_scr[...] = f


def _route_logits(f, wr_ref, br_ref):
    f_hi, f_lo = _split_bf16(f)
    w_hi, w_lo = _split_bf16(wr_ref[...])
    return (lax.dot_general(w_hi, f_hi, NT_DIMS, preferred_element_type=F32)
            + lax.dot_general(w_hi, f_lo, NT_DIMS, preferred_element_type=F32)
            + lax.dot_general(w_lo, f_hi, NT_DIMS, preferred_element_type=F32)) + br_ref[...]


def _route_select(lg, gate_ref, tm):
    gl = lg[0:N_GROUPS]
    gmax = jnp.max(gl, axis=0, keepdims=True)
    gidx = lax.broadcasted_iota(I32, gl.shape, 0)
    g_sel = jnp.min(jnp.where(gl == gmax, gidx, N_GROUPS), axis=0, keepdims=True)
    g_p = 1.0 / jnp.sum(jnp.exp(gl - gmax), axis=0, keepdims=True)

    e8 = jnp.zeros((E_PER_GROUP, tm), F32)
    for g in range(N_GROUPS):
        e8 = e8 + jnp.where(g_sel == g, lg[8 + g * E_PER_GROUP:8 + (g + 1) * E_PER_GROUP], 0.0)
    ex = jnp.exp(e8 - jnp.max(e8, axis=0, keepdims=True))
    ep = ex / jnp.sum(ex, axis=0, keepdims=True)
    idx8 = lax.broadcasted_iota(I32, ep.shape, 0)
    p1 = jnp.max(ep, axis=0, keepdims=True)
    i1 = jnp.min(jnp.where(ep == p1, idx8, E_PER_GROUP), axis=0, keepdims=True)
    ep2 = jnp.where(idx8 == i1, -1.0, ep)
    p2 = jnp.max(ep2, axis=0, keepdims=True)
    i2 = jnp.min(jnp.where(ep2 == p2, idx8, E_PER_GROUP), axis=0, keepdims=True)
    psum = p1 + p2
    gate_ref[0:1, :] = g_p * (p1 / psum)
    gate_ref[1:2, :] = g_p * (p2 / psum)
    return g_sel * E_PER_GROUP + i1, g_sel * E_PER_GROUP + i2


def _route_positions(eid1, eid2, live, lpos_ref, tcnt_ref, tbase_ref, base_scr, tm):
    e32 = lax.broadcasted_iota(I32, (N_EXPERTS, tm), 0)
    oh1 = (e32 == eid1).astype(F32)
    oh2 = (e32 == eid2).astype(F32)
    cnt = oh1 + oh2
    tr = lax.broadcasted_iota(I32, (tm, tm), 0)
    tc = lax.broadcasted_iota(I32, (tm, tm), 1)
    upper = jnp.where(tr < tc, 1.0, 0.0).astype(BF16)
    earlier = jnp.dot(cnt.astype(BF16), upper, preferred_element_type=F32)
    seg = jnp.sum(cnt, axis=1, keepdims=True)
    seg = jnp.floor((seg + (SUBLANES - 1)) * (1.0 / SUBLANES)) * SUBLANES * live
    seg = jnp.broadcast_to(seg, (N_EXPERTS, LANES))
    er = lax.broadcasted_iota(I32, (N_EXPERTS, N_EXPERTS), 0)
    ec = lax.broadcasted_iota(I32, (N_EXPERTS, N_EXPERTS), 1)
    lower = jnp.where(ec < er, 1.0, 0.0).astype(BF16)
    start = jnp.dot(lower, seg.astype(BF16), preferred_element_type=F32)[:, 0:1]
    where = earlier + start
    lpos_ref[0:1, :] = jnp.sum(oh1 * where, axis=0, keepdims=True).astype(I32)
    lpos_ref[1:2, :] = jnp.sum(oh2 * where, axis=0, keepdims=True).astype(I32)
    tcnt_ref[...] = seg
    tbase_ref[...] = base_scr[...]
    base_scr[...] = base_scr[...] + seg


def _post(x2, od, osw, wout_bf, lnc, wq_bf, kv, wo_bf, lnm, wr, br, b, s, tm, m_len):
    n, d = x2.shape
    nt = s // tm
    tiles = n // tm
    main = lambda g: jnp.minimum(g, tiles - 1)
    routed = lambda g: jnp.maximum(g - 1, 0)
    row = lambda g: (main(g), 0)
    const = lambda g: (0, 0)
    return pl.pallas_call(
        functools.partial(_post_kernel, tm=tm, d=d),
        grid=(tiles + 1,),
        in_specs=[
            pl.BlockSpec((tm, d), row),
            pl.BlockSpec((tm, d // 2), row),
            pl.BlockSpec((tm, d // 2), row),
            pl.BlockSpec((d, d), const),
            pl.BlockSpec((1, d), const),
            pl.BlockSpec((d, d), const),
            pl.BlockSpec((m_len, 2 * d), lambda g: (main(g) // nt, 0)),
            pl.BlockSpec((d, d), const),
            pl.BlockSpec((1, d), const),
            pl.BlockSpec((8 + N_EXPERTS, d), const),
            pl.BlockSpec((8 + N_EXPERTS, 1), const),
        ],
        out_specs=[
            pl.BlockSpec((tm, d), row),
            pl.BlockSpec((tm, d), row),
            pl.BlockSpec((2, tm), lambda g: (0, routed(g))),
            pl.BlockSpec((2, tm), lambda g: (0, routed(g))),
            pl.BlockSpec((None, N_EXPERTS, LANES), lambda g: (routed(g), 0, 0)),
            pl.BlockSpec((None, N_EXPERTS, LANES), lambda g: (routed(g), 0, 0)),
        ],
        out_shape=[
            jax.ShapeDtypeStruct((n, d), F32),
            jax.ShapeDtypeStruct((n, d), BF16),
            jax.ShapeDtypeStruct((2, n), I32),
            jax.ShapeDtypeStruct((2, n), F32),
            jax.ShapeDtypeStruct((n // tm, N_EXPERTS, LANES), F32),
            jax.ShapeDtypeStruct((n // tm, N_EXPERTS, LANES), F32),
        ],
        scratch_shapes=[pltpu.VMEM((N_EXPERTS, LANES), F32), pltpu.VMEM((tm, d), F32)],
        compiler_params=pltpu.CompilerParams(
            dimension_semantics=("arbitrary",), vmem_limit_bytes=VMEM_LIMIT),
        name="post",
    )(x2, od, osw, wout_bf, lnc, wq_bf, kv, wo_bf, lnm, wr, br)


def _local_rows(tm):
    return 2 * tm + SUBLANES * N_EXPERTS


def _segment_copies(tile, tcnt_ref, tbase_ref, pstart_ref, make, act, max_len):
    sizes = [SUBLANES << k for k in reversed(range((max_len // SUBLANES).bit_length()))]

    if act == "wait":
        total = lax.fori_loop(0, N_EXPERTS, lambda e, acc: acc + tcnt_ref[tile * N_EXPERTS + e], 0)
        for size in sizes:
            @pl.when((total & size) != 0)
            def _():
                make(0, 0, size).wait()

        return total

    def pieces(ln, local0, glob0, some_sizes):
        for size in some_sizes:
            @pl.when((ln & size) != 0)
            def _():
                off = (ln // (2 * size)) * (2 * size)
                make(pl.multiple_of(local0 + off, SUBLANES), pl.multiple_of(glob0 + off, SUBLANES), size).start()

    big = [s for s in sizes if s >= LONG_SEGMENT_ROWS]
    small = [s for s in sizes if s < LONG_SEGMENT_ROWS]

    def per_expert(e, local0):
        ln = tcnt_ref[tile * N_EXPERTS + e]
        glob0 = pstart_ref[e] + tbase_ref[tile * N_EXPERTS + e]

        @pl.when(ln >= LONG_SEGMENT_ROWS)
        def _():
            pieces(ln, local0, glob0, big)

        pieces(ln, local0, glob0, small)
        return local0 + ln

    return lax.fori_loop(0, N_EXPERTS, per_expert, 0)


def _dispatch_kernel(tcnt_ref, tbase_ref, tot_ref, f_ref, lpos_ref, xs_hbm, pstart_ref, blk_ref, nact_ref,
                     loc_scr, zero_scr, sem, zsem, *, tm, bm, nblk):
    i = pl.program_id(0)
    nt = pl.num_programs(0)
    slot = i % 2

    @pl.when(i == 0)
    def _():
        def per_expert(e, blk0):
            nb = (tot_ref[e] + (bm - 1)) // bm
            pstart_ref[e] = blk0 * bm

            def fill(j, carry):
                blk_ref[j] = e
                return carry

            lax.fori_loop(blk0, blk0 + nb, fill, 0)
            return blk0 + nb

        nact = lax.fori_loop(0, N_EXPERTS, per_expert, 0)
        nact_ref[0] = nact

        def tail(j, carry):
            blk_ref[j] = N_EXPERTS - 1
            return carry

        lax.fori_loop(nact, nblk, tail, 0)

        zero_scr[...] = jnp.zeros(zero_scr.shape, U32)

        def zero_fill(act):
            def do(copy):
                if act == "start":
                    copy.start()
                else:
                    copy.wait()

            def per_expert_pad(e, carry):
                tot = tot_ref[e]
                pad = (tot + (bm - 1)) // bm * bm - tot
                row0 = pstart_ref[e] + tot
                for k in reversed(range((bm // SUBLANES).bit_length())):
                    size = SUBLANES << k

                    @pl.when((pad & size) != 0)
                    def _():
                        row = pl.multiple_of(row0 + (pad // (2 * size)) * (2 * size), SUBLANES)
                        do(pltpu.make_async_copy(zero_scr.at[pl.ds(0, size)], xs_hbm.at[pl.ds(row, size)], zsem))

                return carry

            lax.fori_loop(0, N_EXPERTS, per_expert_pad, 0)

            def per_unused_block(j, carry):
                do(pltpu.make_async_copy(zero_scr, xs_hbm.at[pl.ds(pl.multiple_of(j * bm, bm), bm)], zsem))
                return carry

            lax.fori_loop(nact, nblk, per_unused_block, 0)

        zero_fill("start")
        zero_fill("wait")

    def copies(tile, sl, act):
        def make(lrow, grow, size):
            return pltpu.make_async_copy(loc_scr.at[sl, pl.ds(lrow, size)], xs_hbm.at[pl.ds(grow, size)], sem.at[sl])

        _segment_copies(tile, tcnt_ref, tbase_ref, pstart_ref, make, act, _local_rows(tm))

    @pl.when(i >= 2)
    def _():
        copies(i - 2, slot, "wait")

    pos = lax.broadcasted_iota(I32, (_local_rows(tm), tm), 0)
    onehot = jnp.where(pos == lpos_ref[0:1, :], 1.0, jnp.where(pos == lpos_ref[1:2, :], 1.0, 0.0)).astype(BF16)
    loc_scr[slot] = _pack_pairs(jnp.dot(onehot, f_ref[...], preferred_element_type=F32))
    copies(i, slot, "start")

    @pl.when(i == nt - 1)
    def _():
        @pl.when(i >= 1)
        def _():
            copies(i - 1, 1 - slot, "wait")

        copies(i, slot, "wait")


def _dispatch(tcnt_i, tbase_i, tot_i, f, lpos, tm, bm, nblk):
    n, d = f.shape
    smem = pl.BlockSpec(memory_space=pltpu.SMEM)
    hbm = pl.BlockSpec(memory_space=pl.ANY)
    return pl.pallas_call(
        functools.partial(_dispatch_kernel, tm=tm, bm=bm, nblk=nblk),
        grid_spec=pltpu.PrefetchScalarGridSpec(
            num_scalar_prefetch=3,
            grid=(n // tm,),
            in_specs=[
                pl.BlockSpec((tm, d), lambda i, *_: (i, 0)),
                pl.BlockSpec((2, tm), lambda i, *_: (0, i)),
            ],
            out_specs=[hbm, smem, smem, smem],
            scratch_shapes=[
                pltpu.VMEM((2, _local_rows(tm), d // 2), U32),
                pltpu.VMEM((bm, d // 2), U32),
                pltpu.SemaphoreType.DMA((2,)),
                pltpu.SemaphoreType.DMA,
            ],
        ),
        out_shape=[
            jax.ShapeDtypeStruct((nblk * bm, d // 2), U32),
            jax.ShapeDtypeStruct((N_EXPERTS,), I32),
            jax.ShapeDtypeStruct((nblk,), I32),
            jax.ShapeDtypeStruct((1,), I32),
        ],
        compiler_params=pltpu.CompilerParams(
            dimension_semantics=("arbitrary",), vmem_limit_bytes=VMEM_LIMIT),
        name="dispatch",
    )(tcnt_i, tbase_i, tot_i, f, lpos)


def _expert_kernel(blk_ref, nact_ref, x_ref, wg_ref, wu_ref, wd_ref, y_ref, wg_bf, wu_bf, wd_bf):
    j = pl.program_id(0)
    active = j < nact_ref[0]

    @pl.when(active & ((j == 0) | (blk_ref[j] != blk_ref[jnp.maximum(j - 1, 0)])))
    def _():
        wg_bf[...] = wg_ref[...].astype(BF16)
        wu_bf[...] = wu_ref[...].astype(BF16)
        wd_bf[...] = wd_ref[...].astype(BF16)

    @pl.when(active)
    def _():
        x = _unpack_pairs(x_ref[...])
        g = jnp.dot(x, wg_bf[...], preferred_element_type=F32)
        u = jnp.dot(x, wu_bf[...], preferred_element_type=F32)
        hdn = (g * jax.nn.sigmoid(g) * u).astype(BF16)
        y = jnp.dot(hdn, wd_bf[...], preferred_element_type=F32)
        y_ref[...] = _pack_pairs(y.astype(BF16).astype(F32))

    @pl.when(j >= nact_ref[0])
    def _():
        y_ref[...] = jnp.zeros(y_ref.shape, y_ref.dtype)


def _experts(blk_e, nact, xs, wg, wu, wd, bm):
    p, dh = xs.shape
    d, dff = wg.shape[-2:]
    nblk = p // bm
    rowmap = lambda j, blk, na: (jnp.minimum(j, na[0] - 1), 0)
    wmap = lambda j, blk, na: (blk[jnp.minimum(j, na[0] - 1)], 0, 0)
    return pl.pallas_call(
        _expert_kernel,
        grid_spec=pltpu.PrefetchScalarGridSpec(
            num_scalar_prefetch=2,
            grid=(nblk,),
            in_specs=[
                pl.BlockSpec((bm, dh), rowmap),
                pl.BlockSpec((None, d, dff), wmap),
                pl.BlockSpec((None, d, dff), wmap),
                pl.BlockSpec((None, dff, d), wmap),
            ],
            out_specs=pl.BlockSpec((bm, dh), lambda j, blk, na: (j, 0)),
            scratch_shapes=[pltpu.VMEM((d, dff), BF16), pltpu.VMEM((d, dff), BF16), pltpu.VMEM((dff, d), BF16)],
        ),
        out_shape=jax.ShapeDtypeStruct((p, dh), U32),
        compiler_params=pltpu.CompilerParams(
            dimension_semantics=("arbitrary",), vmem_limit_bytes=VMEM_LIMIT),
        name="experts",
    )(blk_e, nact, xs, wg, wu, wd)


def _combine_kernel(tcnt_ref, tbase_ref, pstart_ref, h_ref, lpos_ref, gate_ref, lnf_ref, ys_hbm, o_ref,
                    loc_scr, sem, *, tm):
    i = pl.program_id(0)
    nt = pl.num_programs(0)
    slot = i % 2

    def copies(tile, sl, act):
        def make(lrow, grow, size):
            return pltpu.make_async_copy(ys_hbm.at[pl.ds(grow, size)], loc_scr.at[sl, pl.ds(lrow, size)], sem.at[sl])

        return _segment_copies(tile, tcnt_ref, tbase_ref, pstart_ref, make, act, _local_rows(tm))

    @pl.when(i == 0)
    def _():
        copies(0, 0, "start")

    @pl.when(i + 1 < nt)
    def _():
        copies(i + 1, 1 - slot, "start")

    used = copies(i, slot, "wait")
    row = lax.broadcasted_iota(I32, (_local_rows(tm), 1), 0)
    ys = _unpack_pairs(jnp.where(row < used, loc_scr[slot], U32(0)))

    r8 = lax.broadcasted_iota(I32, (8, tm), 0)
    lp = lpos_ref[...].astype(F32)
    top = jnp.where(r8 == 0, lp[0:1], jnp.where(r8 == 1, lp[1:2], jnp.where(
        r8 == 2, gate_ref[0:1, :], jnp.where(r8 == 3, gate_ref[1:2, :], 0.0))))
    cols = jnp.concatenate([top, jnp.zeros((LANES - 8, tm), F32)], axis=0).T
    pos = lax.broadcasted_iota(I32, (tm, _local_rows(tm)), 1)
    acc = h_ref[...]
    for k in range(2):
        sel = jnp.where(pos == cols[:, k:k + 1].astype(I32), 1.0, 0.0).astype(BF16)
        acc = acc + cols[:, 2 + k:3 + k] * jnp.dot(sel, ys, preferred_element_type=F32)
    o_ref[...] = _rms(acc, lnf_ref[...])


def _combine(tcnt_i, tbase_i, pstart, h2, lpos, gates, lnf, ys, tm):
    n, d = h2.shape
    return pl.pallas_call(
        functools.partial(_combine_kernel, tm=tm),
        grid_spec=pltpu.PrefetchScalarGridSpec(
            num_scalar_prefetch=3,
            grid=(n // tm,),
            in_specs=[
                pl.BlockSpec((tm, d), lambda i, *_: (i, 0)),
                pl.BlockSpec((2, tm), lambda i, *_: (0, i)),
                pl.BlockSpec((2, tm), lambda i, *_: (0, i)),
                pl.BlockSpec((1, d), lambda i, *_: (0, 0)),
                pl.BlockSpec(memory_space=pl.ANY),
            ],
            out_specs=pl.BlockSpec((tm, d), lambda i, *_: (i, 0)),
            scratch_shapes=[pltpu.VMEM((2, _local_rows(tm), d // 2), U32), pltpu.SemaphoreType.DMA((2,))],
        ),
        out_shape=jax.ShapeDtypeStruct((n, d), F32),
        compiler_params=pltpu.CompilerParams(
            dimension_semantics=("arbitrary",), vmem_limit_bytes=VMEM_LIMIT),
        name="combine",
    )(tcnt_i, tbase_i, pstart, h2, lpos, gates, lnf, ys)


def _swa_head_perm():
    cols = []
    for p in range(4):
        for half in range(2):
            head = half * 4 + p
            cols.extend(range(head * D_HEAD, (head + 1) * D_HEAD))
    return jnp.asarray(cols, dtype=I32)


def kernel(x, mem, positions, ln_mix_w, w_in, lambda_q1, lambda_k1, lambda_q2, lambda_k2, subln_w, sinks,
           w_out, ln_cross_w, ln_mem_w, wq_cross, wkv_cross, wo_cross, ln_moe_w, w_group, b_group,
           w_expert, b_expert, w_gate, w_up, w_down, ln_final_w):
    b, s, d = x.shape
    m_len = mem.shape[1]
    n = b * s
    assert w_in.shape[0] == 1 and d == 1024 and n <= 65536
    lambda_init = 0.8 - 0.6 * math.exp(-0.3 * 0)

    tm = 512
    t_attn = 256
    tk_attn = 1024
    bm = 512

    x2 = x.reshape(n, d)
    half = D_HEAD // 2
    per_row = LANES // half
    inv_freq = jnp.exp(-math.log(ROPE_THETA) * jnp.arange(0, D_HEAD, 2, dtype=F32) / D_HEAD)
    inv128 = jnp.tile(inv_freq, per_row).reshape(1, LANES)
    pos_rep = jnp.repeat(positions.reshape(n // per_row, per_row).astype(I32), half, axis=1)
    cos_t, sin_t = _rope_tables(pos_rep, inv128, min(1024, n // per_row))
    cos_t = cos_t.reshape(n, half)
    sin_t = sin_t.reshape(n, half)

    perm = _swa_head_perm()
    sq0 = 3 * 512
    w_in_l = w_in[0]
    w_in_p = jnp.concatenate([w_in_l[:, :sq0], w_in_l[:, sq0:sq0 + 512][:, perm], w_in_l[:, sq0 + 512:]], axis=1)
    w_out_l = w_out[0]
    w_out_p = jnp.concatenate([w_out_l[:512], w_out_l[512:][perm]], axis=0)
    sinks_p = sinks[0].reshape(2, 4).reshape(-1)

    qkv = _inproj(x2, ln_mix_w[0].reshape(1, d), w_in_p.astype(BF16), cos_t, sin_t, tm)
    o_diff = _diff_attention(qkv, lambda_q1[0].reshape(1, -1), lambda_k1[0].reshape(1, -1),
                             lambda_q2[0].reshape(1, -1), lambda_k2[0].reshape(1, -1),
                             subln_w[0].reshape(-1, 1), b, s, t_attn, min(tk_attn, s), lambda_init)
    o_swa = _swa_attention(qkv, sinks_p.astype(F32), b, s, tm)
    kv = _memkv(mem.reshape(b * m_len, d), ln_mem_w[0].reshape(1, d), wkv_cross[0].astype(BF16), m_len)

    wr = jnp.concatenate([w_group[0].T, jnp.zeros((8 - N_GROUPS, d), F32), w_expert[0].T], axis=0)
    br = jnp.concatenate([b_group[0], jnp.zeros((8 - N_GROUPS,), F32), b_expert[0]]).reshape(-1, 1)
    h2, f, lpos, gates, tcnt, tbase = _post(
        x2, o_diff, o_swa, w_out_p.astype(BF16), ln_cross_w[0].reshape(1, d), wq_cross[0].astype(BF16), kv,
        wo_cross[0].astype(BF16), ln_moe_w[0].reshape(1, d), wr, br, b, s, tm, m_len)

    tcnt_i = tcnt[:, :, 0].astype(I32).reshape(-1)
    tbase_i = tbase[:, :, 0].astype(I32).reshape(-1)
    tot_i = tbase_i[-N_EXPERTS:] + tcnt_i[-N_EXPERTS:]
    rows_max = 2 * n + (n // tm) * N_EXPERTS * (SUBLANES - 1)
    nblk = (rows_max + N_EXPERTS * (bm - 1) + bm - 1) // bm
    xs, pstart, blk_e, nact = _dispatch(tcnt_i, tbase_i, tot_i, f, lpos, tm, bm, nblk)
    ys = _experts(blk_e, nact, xs, w_gate[0], w_up[0], w_down[0], bm)
    out = _combine(tcnt_i, tbase_i, pstart, h2, lpos, gates, ln_final_w.reshape(1, d), ys, tm)
    return out.reshape(b, s, d)
```

```python
import functools
import math

import jax
import jax.numpy as jnp
from jax import lax
from jax.experimental import pallas as pl
from jax.experimental.pallas import tpu as pltpu

F32 = jnp.float32
BF16 = jnp.bfloat16
I32 = jnp.int32

D_HEAD = 64
ROPE_THETA = 10000.0
H_DIFF = 4
N_Q_SWA = 8
N_KV_SWA = 2
WINDOW = 128
H_CROSS = 4
N_GROUPS = 4
E_PER_GROUP = 8
N_EXPERTS = N_GROUPS * E_PER_GROUP
EPS = 1e-6
LANES = 128
SUBLANES = 8
LONG_SEGMENT_ROWS = 128
IN_W = 2304
NEG = -1e30

VMEM_LIMIT = 56 * 1024 * 1024

NT_DIMS = (((1,), (1,)), ((), ()))


U32 = jnp.uint32


def _pack_pairs(x):
    h = x.shape[1] // 2
    hi = lax.bitcast_convert_type(x[:, :h], U32)
    lo = lax.bitcast_convert_type(x[:, h:], U32)
    return hi | (lo >> 16)


def _unpack_pairs(p):
    a = lax.bitcast_convert_type(p & U32(0xFFFF0000), F32)
    b = lax.bitcast_convert_type(p << 16, F32)
    return jnp.concatenate([a, b], axis=1).astype(BF16)


def _rms(x, w):
    ms = jnp.mean(x * x, axis=-1, keepdims=True)
    return x * lax.rsqrt(ms + EPS) * w


def _rope_table_kernel(pos_ref, inv_ref, cos_ref, sin_ref):
    ang = pos_ref[...].astype(F32) * inv_ref[...]
    cos_ref[...] = jnp.cos(ang)
    sin_ref[...] = jnp.sin(ang)


def _rope_tables(pos_rep, inv128, rows):
    n4 = pos_rep.shape[0]
    spec = pl.BlockSpec((rows, LANES), lambda i: (i, 0))
    return pl.pallas_call(
        _rope_table_kernel,
        grid=(n4 // rows,),
        in_specs=[spec, pl.BlockSpec((1, LANES), lambda i: (0, 0))],
        out_specs=[spec, spec],
        out_shape=[jax.ShapeDtypeStruct((n4, LANES), F32)] * 2,
        compiler_params=pltpu.CompilerParams(dimension_semantics=("arbitrary",)),
        name="ropetab",
    )(pos_rep, inv128)


def _inproj_kernel(x_ref, lnw_ref, w_ref, cos_ref, sin_ref, o_ref):
    a = _rms(x_ref[...], lnw_ref[...]).astype(BF16)
    reps = LANES // (D_HEAD // 2)
    cos = jnp.concatenate([cos_ref[...]] * reps, axis=1)
    sin = jnp.concatenate([sin_ref[...]] * reps, axis=1)
    lane = lax.broadcasted_iota(I32, (1, LANES), 1)
    first = (lane % D_HEAD) < (D_HEAD // 2)
    sin_signed = jnp.where(first, -sin, sin)
    n_chunks = IN_W // 256
    value_chunks = (4, 5)
    for c in [c for c in range(n_chunks) if c not in value_chunks] + list(value_chunks):
        p = jnp.dot(a, w_ref[:, c * 256:(c + 1) * 256], preferred_element_type=F32)
        for hh in range(2):
            g = c * 2 + hh
            xg = p[:, hh * LANES:(hh + 1) * LANES]
            is_v = (8 <= g < 12) or g == 17
            if not is_v:
                partner = jnp.where(first, pltpu.roll(xg, 96, 1), pltpu.roll(xg, 32, 1))
                xg = xg * cos + partner * sin_signed
                if g < 4:
                    xg = xg * (D_HEAD ** -0.5 * math.log2(math.e))
                elif 12 <= g < 16:
                    xg = xg * (D_HEAD ** -0.5)
            o_ref[:, g * LANES:(g + 1) * LANES] = xg.astype(BF16)


def _inproj(x2, ln_w, w_in_bf, cos, sin, tm):
    n, d = x2.shape
    half = D_HEAD // 2
    return pl.pallas_call(
        _inproj_kernel,
        grid=(n // tm,),
        in_specs=[
            pl.BlockSpec((tm, d), lambda i: (i, 0)),
            pl.BlockSpec((1, d), lambda i: (0, 0)),
            pl.BlockSpec((d, IN_W), lambda i: (0, 0)),
            pl.BlockSpec((tm, half), lambda i: (i, 0)),
            pl.BlockSpec((tm, half), lambda i: (i, 0)),
        ],
        out_specs=pl.BlockSpec((tm, IN_W), lambda i: (i, 0)),
        out_shape=jax.ShapeDtypeStruct((n, IN_W), BF16),
        compiler_params=pltpu.CompilerParams(
            dimension_semantics=("arbitrary",), vmem_limit_bytes=VMEM_LIMIT),
        name="inproj",
    )(x2, ln_w, w_in_bf, cos, sin)


def _diff_group_kernel(q_ref, k_ref, v_ref, lq1_ref, lk1_ref, lq2_ref, lk2_ref, sw_ref, o_ref,
                       vt_scr, q2_scr, acc_scr, m_scr, l_scr, s_scr, *, t, tk, lambda_init):
    s = k_ref.shape[0]
    per = tk // t
    for c in range(s // tk):
        for r in range(per):
            rows = slice(c * tk + r * t, c * tk + (r + 1) * t)
            vt_scr[c, :, r * t:(r + 1) * t] = v_ref[rows, :].astype(F32).T.astype(BF16)

    lane = lax.broadcasted_iota(I32, (1, LANES), 1)
    lam = (jnp.exp(jnp.sum(lq1_ref[...] * lk1_ref[...], axis=1, keepdims=True))
           - jnp.exp(jnp.sum(lq2_ref[...] * lk2_ref[...], axis=1, keepdims=True))
           + lambda_init)

    def accumulate(r, st_parts, vt):
        m_prev = m_scr[r]
        m_new = m_prev
        for st in st_parts:
            m_new = jnp.maximum(m_new, jnp.max(st, axis=0, keepdims=True))
        ps = [jnp.exp2(st - m_new) for st in st_parts]
        alpha = jnp.exp2(m_prev - m_new)
        l_new = alpha * l_scr[r]
        for p in ps:
            l_new = l_new + jnp.sum(p, axis=0, keepdims=True)
        l_scr[r] = l_new
        pb = [p.astype(BF16) for p in ps]
        pb = pb[0] if len(pb) == 1 else jnp.concatenate(pb, axis=0)
        acc_scr[r] = alpha * acc_scr[r] + jnp.dot(vt, pb, preferred_element_type=F32)
        m_scr[r] = m_new

    def interleaved(score_fn, softmax_fn):
        score_fn(0)
        for r in range(per):
            if r + 1 < per:
                score_fn(r + 1)
            softmax_fn(r)

    def group(gi, carry):
        row0 = pl.multiple_of(gi * tk, tk)
        for r in range(per):
            q = q_ref[pl.ds(row0 + r * t, t), :]
            zero = jnp.zeros_like(q)
            q2_scr[r] = jnp.concatenate(
                [jnp.where(lane < D_HEAD, q, zero), jnp.where(lane >= D_HEAD, q, zero)], axis=0)
        acc_scr[...] = jnp.zeros(acc_scr.shape, F32)
        m_scr[...] = jnp.full(m_scr.shape, NEG, F32)
        l_scr[...] = jnp.zeros(l_scr.shape, F32)

        def diag_scores(r):
            rows = (r + 1) * t
            k = k_ref[pl.ds(row0, rows), :]
            s_scr[r, 0:rows, :] = lax.dot_general(k, q2_scr[r], NT_DIMS, preferred_element_type=F32)

        def diag_softmax(r):
            rows = (r + 1) * t
            key = lax.broadcasted_iota(I32, (t, 2 * t), 0)
            col = lax.broadcasted_iota(I32, (t, 2 * t), 1)
            diag = jnp.where(key <= jnp.where(col >= t, col - t, col), s_scr[r, r * t:rows, :], NEG)
            parts = [diag] if r == 0 else [s_scr[r, 0:r * t, :], diag]
            accumulate(r, parts, vt_scr[gi, :, 0:rows])

        interleaved(diag_scores, diag_softmax)

        def key_block(j, c2):
            k = k_ref[pl.ds(pl.multiple_of(j * tk, tk), tk), :]

            def full_scores(r):
                s_scr[r] = lax.dot_general(k, q2_scr[r], NT_DIMS, preferred_element_type=F32)

            def full_softmax(r):
                accumulate(r, [s_scr[r]], vt_scr[j])

            interleaved(full_scores, full_softmax)
            return c2

        lax.fori_loop(0, gi, key_block, 0)

        for r in range(per):
            on = acc_scr[r] / l_scr[r]
            o = on[:, :t] - lam * on[:, t:]
            ms = jnp.mean(o * o, axis=0, keepdims=True)
            o = o * lax.rsqrt(ms + EPS) * sw_ref[...] * (1.0 - lambda_init)
            o_ref[pl.ds(row0 + r * t, t), :] = o.T.astype(BF16)
        return carry

    lax.fori_loop(0, s // tk, group, 0)


def _diff_attention(qkv, lq1, lk1, lq2, lk2, subln_col, b, s, t, tk, lambda_init):
    n = qkv.shape[0]
    per = tk // t
    small = pl.BlockSpec((1, D_HEAD), lambda bi, h: (0, 0))
    return pl.pallas_call(
        functools.partial(_diff_group_kernel, t=t, tk=tk, lambda_init=lambda_init),
        grid=(b, H_DIFF),
        in_specs=[
            pl.BlockSpec((s, LANES), lambda bi, h: (bi, h)),
            pl.BlockSpec((s, LANES), lambda bi, h: (bi, 4 + h)),
            pl.BlockSpec((s, LANES), lambda bi, h: (bi, 8 + h)),
            small, small, small, small,
            pl.BlockSpec((LANES, 1), lambda bi, h: (0, 0)),
        ],
        out_specs=pl.BlockSpec((s, LANES), lambda bi, h: (bi, h)),
        out_shape=jax.ShapeDtypeStruct((n, H_DIFF * LANES), BF16),
        scratch_shapes=[
            pltpu.VMEM((s // tk, LANES, tk), BF16),
            pltpu.VMEM((per, 2 * t, LANES), BF16),
            pltpu.VMEM((per, LANES, 2 * t), F32),
            pltpu.VMEM((per, 1, 2 * t), F32),
            pltpu.VMEM((per, 1, 2 * t), F32),
            pltpu.VMEM((per, tk, 2 * t), F32),
        ],
        compiler_params=pltpu.CompilerParams(
            dimension_semantics=("arbitrary", "arbitrary"), vmem_limit_bytes=VMEM_LIMIT),
        name="diffattn",
    )(qkv, qkv, qkv, lq1, lk1, lq2, lk2, subln_col)


def _swa_kernel(sink_ref, q_ref, kc_ref, kp_ref, vc_ref, vp_ref, o_ref, kbuf, vbuf, *, tq):
    i = pl.program_id(1)
    w = WINDOW
    kbuf[0:w, :] = kp_ref[...]
    kbuf[w:w + tq, :] = kc_ref[...]
    vbuf[0:w, :] = vp_ref[...]
    vbuf[w:w + tq, :] = vc_ref[...]
    lane = lax.broadcasted_iota(I32, (1, LANES), 1)
    lo = lane < D_HEAD
    qi = lax.broadcasted_iota(I32, (w, 2 * w), 0)
    ki = lax.broadcasted_iota(I32, (w, 2 * w), 1)
    band = (ki > qi) & (ki <= qi + w)
    band_first = band & (ki >= jnp.where(i > 0, 0, w))
    for r in range(tq // w):
        keys = kbuf[r * w:(r + 2) * w, :]
        vals = vbuf[r * w:(r + 2) * w, :]
        valid = band_first if r == 0 else band
        qs = []
        for half in range(2):
            for p in range(4):
                qg = q_ref[r * w:(r + 1) * w, p * LANES:(p + 1) * LANES]
                qs.append(jnp.where(lo if half == 0 else ~lo, qg, jnp.zeros_like(qg)))
        qstack = jnp.concatenate(qs, axis=0)
        s_all = lax.dot_general(qstack, keys, NT_DIMS, preferred_element_type=F32)
        ps = []
        for hb in range(8):
            sink = sink_ref[hb]
            s = jnp.where(valid, s_all[hb * w:(hb + 1) * w], NEG)
            m = jnp.maximum(jnp.max(s, axis=1, keepdims=True), sink)
            e = jnp.exp(s - m)
            den = jnp.sum(e, axis=1, keepdims=True) + jnp.exp(sink - m)
            ps.append((e / den).astype(BF16))
        pv = jnp.dot(jnp.concatenate(ps, axis=0), vals, preferred_element_type=F32)
        for p in range(4):
            og = jnp.where(lo, pv[p * w:(p + 1) * w], pv[(4 + p) * w:(5 + p) * w])
            o_ref[r * w:(r + 1) * w, p * LANES:(p + 1) * LANES] = og.astype(BF16)


def _swa_attention(qkv, sinks, b, s, tq):
    n = qkv.shape[0]
    nq = s // tq
    per = tq // WINDOW
    prev_map = lambda col: (lambda bi, i, sk: (jnp.maximum(bi * (s // WINDOW) + i * per - 1, 0), col))
    cur_map = lambda col: (lambda bi, i, sk: (bi * nq + i, col))
    return pl.pallas_call(
        functools.partial(_swa_kernel, tq=tq),
        grid_spec=pltpu.PrefetchScalarGridSpec(
            num_scalar_prefetch=1,
            grid=(b, nq),
            in_specs=[
                pl.BlockSpec((tq, 4 * LANES), lambda bi, i, sk: (bi * nq + i, 3)),
                pl.BlockSpec((tq, LANES), cur_map(16)),
                pl.BlockSpec((WINDOW, LANES), prev_map(16)),
                pl.BlockSpec((tq, LANES), cur_map(17)),
                pl.BlockSpec((WINDOW, LANES), prev_map(17)),
            ],
            out_specs=pl.BlockSpec((tq, 4 * LANES), lambda bi, i, sk: (bi * nq + i, 0)),
            scratch_shapes=[pltpu.VMEM((WINDOW + tq, LANES), BF16)] * 2,
        ),
        out_shape=jax.ShapeDtypeStruct((n, 4 * LANES), BF16),
        compiler_params=pltpu.CompilerParams(
            dimension_semantics=("arbitrary", "arbitrary"), vmem_limit_bytes=VMEM_LIMIT),
        name="swa",
    )(sinks, qkv, qkv, qkv, qkv, qkv)


def _memkv_kernel(m_ref, lnw_ref, w_ref, o_ref):
    a = _rms(m_ref[...], lnw_ref[...]).astype(BF16)
    o_ref[...] = jnp.dot(a, w_ref[...], preferred_element_type=F32).astype(BF16)


def _memkv(mem2, ln_w, wkv_bf, m_len):
    n, d = mem2.shape
    return pl.pallas_call(
        _memkv_kernel,
        grid=(n // m_len,),
        in_specs=[
            pl.BlockSpec((m_len, d), lambda i: (i, 0)),
            pl.BlockSpec((1, d), lambda i: (0, 0)),
            pl.BlockSpec((d, 2 * d), lambda i: (0, 0)),
        ],
        out_specs=pl.BlockSpec((m_len, 2 * d), lambda i: (i, 0)),
        out_shape=jax.ShapeDtypeStruct((n, 2 * d), BF16),
        compiler_params=pltpu.CompilerParams(
            dimension_semantics=("arbitrary",), vmem_limit_bytes=VMEM_LIMIT),
        name="memkv",
    )(mem2, ln_w, wkv_bf)


def _split_bf16(x):
    hi = x.astype(BF16)
    lo = (x - hi.astype(F32)).astype(BF16)
    return hi, lo


def _post_kernel(x_ref, od_ref, os_ref, wout_ref, lnc_ref, wq_ref, kv_ref, wo_ref, lnm_ref, wr_ref, br_ref,
                 h_ref, f_ref, lpos_ref, gate_ref, tcnt_ref, tbase_ref, base_scr, fprev_scr, *, tm, d):
    g = pl.program_id(0)

    @pl.when(g == 0)
    def _():
        base_scr[...] = jnp.zeros(base_scr.shape, F32)
        fprev_scr[...] = jnp.zeros(fprev_scr.shape, F32)

    mix = jnp.concatenate([od_ref[...], os_ref[...]], axis=1)
    h1 = x_ref[...] + jnp.dot(mix, wout_ref[...], preferred_element_type=F32)
    logits = _route_logits(fprev_scr[...], wr_ref, br_ref)

    c = _rms(h1, lnc_ref[...]).astype(BF16)
    dc = d // H_CROSS
    q = (jnp.dot(c, wq_ref[...], preferred_element_type=F32) * (dc ** -0.5)).astype(BF16)
    eid1, eid2 = _route_select(logits, gate_ref, tm)
    outs = []
    for hd in range(H_CROSS):
        k = kv_ref[:, hd * dc:(hd + 1) * dc]
        v = kv_ref[:, d + hd * dc:d + (hd + 1) * dc]
        s = lax.dot_general(q[:, hd * dc:(hd + 1) * dc], k, NT_DIMS, preferred_element_type=F32)
        e = jnp.exp(s - jnp.max(s, axis=1, keepdims=True))
        den = jnp.sum(e, axis=1, keepdims=True)
        outs.append((jnp.dot(e.astype(BF16), v, preferred_element_type=F32) / den).astype(BF16))
    o = jnp.concatenate(outs, axis=1)
    h2 = h1 + jnp.dot(o, wo_ref[...], preferred_element_type=F32)
    _route_positions(eid1, eid2, jnp.where(g > 0, 1.0, 0.0), lpos_ref, tcnt_ref, tbase_ref, base_scr, tm)
    h_ref[...] = h2
    f = _rms(h2, lnm_ref[...])
    f_ref[...] = f.astype(BF16)
    fprev_scr[...] = f


def _route_logits(f, wr_ref, br_ref):
    f_hi, f_lo = _split_bf16(f)
    w_hi, w_lo = _split_bf16(wr_ref[...])
    return (lax.dot_general(w_hi, f_hi, NT_DIMS, preferred_element_type=F32)
            + lax.dot_general(w_hi, f_lo, NT_DIMS, preferred_element_type=F32)
            + lax.dot_general(w_lo, f_hi, NT_DIMS, preferred_element_type=F32)) + br_ref[...]


def _route_select(lg, gate_ref, tm):
    gl = lg[0:N_GROUPS]
    gmax = jnp.max(gl, axis=0, keepdims=True)
    gidx = lax.broadcasted_iota(I32, gl.shape, 0)
    g_sel = jnp.min(jnp.where(gl == gmax, gidx, N_GROUPS), axis=0, keepdims=True)
    g_p = 1.0 / jnp.sum(jnp.exp(gl - gmax), axis=0, keepdims=True)

    e8 = jnp.zeros((E_PER_GROUP, tm), F32)
    for g in range(N_GROUPS):
        e8 = e8 + jnp.where(g_sel == g, lg[8 + g * E_PER_GROUP:8 + (g + 1) * E_PER_GROUP], 0.0)
    ex = jnp.exp(e8 - jnp.max(e8, axis=0, keepdims=True))
    ep = ex / jnp.sum(ex, axis=0, keepdims=True)
    idx8 = lax.broadcasted_iota(I32, ep.shape, 0)
    p1 = jnp.max(ep, axis=0, keepdims=True)
    i1 = jnp.min(jnp.where(ep == p1, idx8, E_PER_GROUP), axis=0, keepdims=True)
    ep2 = jnp.where(idx8 == i1, -1.0, ep)
    p2 = jnp.max(ep2, axis=0, keepdims=True)
    i2 = jnp.min(jnp.where(ep2 == p2, idx8, E_PER_GROUP), axis=0, keepdims=True)
    psum = p1 + p2
    gate_ref[0:1, :] = g_p * (p1 / psum)
    gate_ref[1:2, :] = g_p * (p2 / psum)
    return g_sel * E_PER_GROUP + i1, g_sel * E_PER_GROUP + i2


def _route_positions(eid1, eid2, live, lpos_ref, tcnt_ref, tbase_ref, base_scr, tm):
    e32 = lax.broadcasted_iota(I32, (N_EXPERTS, tm), 0)
    oh1 = (e32 == eid1).astype(F32)
    oh2 = (e32 == eid2).astype(F32)
    cnt = oh1 + oh2
    tr = lax.broadcasted_iota(I32, (tm, tm), 0)
    tc = lax.broadcasted_iota(I32, (tm, tm), 1)
    upper = jnp.where(tr < tc, 1.0, 0.0).astype(BF16)
    earlier = jnp.dot(cnt.astype(BF16), upper, preferred_element_type=F32)
    seg = jnp.sum(cnt, axis=1, keepdims=True)
    seg = jnp.floor((seg + (SUBLANES - 1)) * (1.0 / SUBLANES)) * SUBLANES * live
    seg = jnp.broadcast_to(seg, (N_EXPERTS, LANES))
    er = lax.broadcasted_iota(I32, (N_EXPERTS, N_EXPERTS), 0)
    ec = lax.broadcasted_iota(I32, (N_EXPERTS, N_EXPERTS), 1)
    lower = jnp.where(ec < er, 1.0, 0.0).astype(BF16)
    start = jnp.dot(lower, seg.astype(BF16), preferred_element_type=F32)[:, 0:1]
    where = earlier + start
    lpos_ref[0:1, :] = jnp.sum(oh1 * where, axis=0, keepdims=True).astype(I32)
    lpos_ref[1:2, :] = jnp.sum(oh2 * where, axis=0, keepdims=True).astype(I32)
    tcnt_ref[...] = seg
    tbase_ref[...] = base_scr[...]
    base_scr[...] = base_scr[...] + seg


def _post(x2, od, osw, wout_bf, lnc, wq_bf, kv, wo_bf, lnm, wr, br, b, s, tm, m_len):
    n, d = x2.shape
    nt = s // tm
    tiles = n // tm
    main = lambda g: jnp.minimum(g, tiles - 1)
    routed = lambda g: jnp.maximum(g - 1, 0)
    row = lambda g: (main(g), 0)
    const = lambda g: (0, 0)
    return pl.pallas_call(
        functools.partial(_post_kernel, tm=tm, d=d),
        grid=(tiles + 1,),
        in_specs=[
            pl.BlockSpec((tm, d), row),
            pl.BlockSpec((tm, d // 2), row),
            pl.BlockSpec((tm, d // 2), row),
            pl.BlockSpec((d, d), const),
            pl.BlockSpec((1, d), const),
            pl.BlockSpec((d, d), const),
            pl.BlockSpec((m_len, 2 * d), lambda g: (main(g) // nt, 0)),
            pl.BlockSpec((d, d), const),
            pl.BlockSpec((1, d), const),
            pl.BlockSpec((8 + N_EXPERTS, d), const),
            pl.BlockSpec((8 + N_EXPERTS, 1), const),
        ],
        out_specs=[
            pl.BlockSpec((tm, d), row),
            pl.BlockSpec((tm, d), row),
            pl.BlockSpec((2, tm), lambda g: (0, routed(g))),
            pl.BlockSpec((2, tm), lambda g: (0, routed(g))),
            pl.BlockSpec((None, N_EXPERTS, LANES), lambda g: (routed(g), 0, 0)),
            pl.BlockSpec((None, N_EXPERTS, LANES), lambda g: (routed(g), 0, 0)),
        ],
        out_shape=[
            jax.ShapeDtypeStruct((n, d), F32),
            jax.ShapeDtypeStruct((n, d), BF16),
            jax.ShapeDtypeStruct((2, n), I32),
            jax.ShapeDtypeStruct((2, n), F32),
            jax.ShapeDtypeStruct((n // tm, N_EXPERTS, LANES), F32),
            jax.ShapeDtypeStruct((n // tm, N_EXPERTS, LANES), F32),
        ],
        scratch_shapes=[pltpu.VMEM((N_EXPERTS, LANES), F32), pltpu.VMEM((tm, d), F32)],
        compiler_params=pltpu.CompilerParams(
            dimension_semantics=("arbitrary",), vmem_limit_bytes=VMEM_LIMIT),
        name="post",
    )(x2, od, osw, wout_bf, lnc, wq_bf, kv, wo_bf, lnm, wr, br)


def _local_rows(tm):
    return 2 * tm + SUBLANES * N_EXPERTS


def _segment_copies(tile, tcnt_ref, tbase_ref, pstart_ref, make, act, max_len):
    sizes = [SUBLANES << k for k in reversed(range((max_len // SUBLANES).bit_length()))]

    if act == "wait":
        total = lax.fori_loop(0, N_EXPERTS, lambda e, acc: acc + tcnt_ref[tile * N_EXPERTS + e], 0)
        for size in sizes:
            @pl.when((total & size) != 0)
            def _():
                make(0, 0, size).wait()

        return total

    def pieces(ln, local0, glob0, some_sizes):
        for size in some_sizes:
            @pl.when((ln & size) != 0)
            def _():
                off = (ln // (2 * size)) * (2 * size)
                make(pl.multiple_of(local0 + off, SUBLANES), pl.multiple_of(glob0 + off, SUBLANES), size).start()

    big = [s for s in sizes if s >= LONG_SEGMENT_ROWS]
    small = [s for s in sizes if s < LONG_SEGMENT_ROWS]

    def per_expert(e, local0):
        ln = tcnt_ref[tile * N_EXPERTS + e]
        glob0 = pstart_ref[e] + tbase_ref[tile * N_EXPERTS + e]

        @pl.when(ln >= LONG_SEGMENT_ROWS)
        def _():
            pieces(ln, local0, glob0, big)

        pieces(ln, local0, glob0, small)
        return local0 + ln

    return lax.fori_loop(0, N_EXPERTS, per_expert, 0)


def _dispatch_kernel(tcnt_ref, tbase_ref, tot_ref, f_ref, lpos_ref, xs_hbm, pstart_ref, blk_ref, nact_ref,
                     loc_scr, zero_scr, sem, zsem, *, tm, bm, nblk):
    i = pl.program_id(0)
    nt = pl.num_programs(0)
    slot = i % 2

    @pl.when(i == 0)
    def _():
        def per_expert(e, blk0):
            nb = (tot_ref[e] + (bm - 1)) // bm
            pstart_ref[e] = blk0 * bm

            def fill(j, carry):
                blk_ref[j] = e
                return carry

            lax.fori_loop(blk0, blk0 + nb, fill, 0)
            return blk0 + nb

        nact = lax.fori_loop(0, N_EXPERTS, per_expert, 0)
        nact_ref[0] = nact

        def tail(j, carry):
            blk_ref[j] = N_EXPERTS - 1
            return carry

        lax.fori_loop(nact, nblk, tail, 0)

        zero_scr[...] = jnp.zeros(zero_scr.shape, U32)

        def zero_fill(act):
            def do(copy):
                if act == "start":
                    copy.start()
                else:
                    copy.wait()

            def per_expert_pad(e, carry):
                tot = tot_ref[e]
                pad = (tot + (bm - 1)) // bm * bm - tot
                row0 = pstart_ref[e] + tot
                for k in reversed(range((bm // SUBLANES).bit_length())):
                    size = SUBLANES << k

                    @pl.when((pad & size) != 0)
                    def _():
                        row = pl.multiple_of(row0 + (pad // (2 * size)) * (2 * size), SUBLANES)
                        do(pltpu.make_async_copy(zero_scr.at[pl.ds(0, size)], xs_hbm.at[pl.ds(row, size)], zsem))

                return carry

            lax.fori_loop(0, N_EXPERTS, per_expert_pad, 0)

            def per_unused_block(j, carry):
                do(pltpu.make_async_copy(zero_scr, xs_hbm.at[pl.ds(pl.multiple_of(j * bm, bm), bm)], zsem))
                return carry

            lax.fori_loop(nact, nblk, per_unused_block, 0)

        zero_fill("start")
        zero_fill("wait")

    def copies(tile, sl, act):
        def make(lrow, grow, size):
            return pltpu.make_async_copy(loc_scr.at[sl, pl.ds(lrow, size)], xs_hbm.at[pl.ds(grow, size)], sem.at[sl])

        _segment_copies(tile, tcnt_ref, tbase_ref, pstart_ref, make, act, _local_rows(tm))

    @pl.when(i >= 2)
    def _():
        copies(i - 2, slot, "wait")

    pos = lax.broadcasted_iota(I32, (_local_rows(tm), tm), 0)
    onehot = jnp.where(pos == lpos_ref[0:1, :], 1.0, jnp.where(pos == lpos_ref[1:2, :], 1.0, 0.0)).astype(BF16)
    loc_scr[slot] = _pack_pairs(jnp.dot(onehot, f_ref[...], preferred_element_type=F32))
    copies(i, slot, "start")

    @pl.when(i == nt - 1)
    def _():
        @pl.when(i >= 1)
        def _():
            copies(i - 1, 1 - slot, "wait")

        copies(i, slot, "wait")


def _dispatch(tcnt_i, tbase_i, tot_i, f, lpos, tm, bm, nblk):
    n, d = f.shape
    smem = pl.BlockSpec(memory_space=pltpu.SMEM)
    hbm = pl.BlockSpec(memory_space=pl.ANY)
    return pl.pallas_call(
        functools.partial(_dispatch_kernel, tm=tm, bm=bm, nblk=nblk),
        grid_spec=pltpu.PrefetchScalarGridSpec(
            num_scalar_prefetch=3,
            grid=(n // tm,),
            in_specs=[
                pl.BlockSpec((tm, d), lambda i, *_: (i, 0)),
                pl.BlockSpec((2, tm), lambda i, *_: (0, i)),
            ],
            out_specs=[hbm, smem, smem, smem],
            scratch_shapes=[
                pltpu.VMEM((2, _local_rows(tm), d // 2), U32),
                pltpu.VMEM((bm, d // 2), U32),
                pltpu.SemaphoreType.DMA((2,)),
                pltpu.SemaphoreType.DMA,
            ],
        ),
        out_shape=[
            jax.ShapeDtypeStruct((nblk * bm, d // 2), U32),
            jax.ShapeDtypeStruct((N_EXPERTS,), I32),
            jax.ShapeDtypeStruct((nblk,), I32),
            jax.ShapeDtypeStruct((1,), I32),
        ],
        compiler_params=pltpu.CompilerParams(
            dimension_semantics=("arbitrary",), vmem_limit_bytes=VMEM_LIMIT),
        name="dispatch",
    )(tcnt_i, tbase_i, tot_i, f, lpos)


def _expert_kernel(blk_ref, nact_ref, x_ref, wg_ref, wu_ref, wd_ref, y_ref, wg_bf, wu_bf, wd_bf):
    j = pl.program_id(0)
    active = j < nact_ref[0]

    @pl.when(active & ((j == 0) | (blk_ref[j] != blk_ref[jnp.maximum(j - 1, 0)])))
    def _():
        wg_bf[...] = wg_ref[...].astype(BF16)
        wu_bf[...] = wu_ref[...].astype(BF16)
        wd_bf[...] = wd_ref[...].astype(BF16)

    @pl.when(active)
    def _():
        x = _unpack_pairs(x_ref[...])
        g = jnp.dot(x, wg_bf[...], preferred_element_type=F32)
        u = jnp.dot(x, wu_bf[...], preferred_element_type=F32)
        hdn = (g * jax.nn.sigmoid(g) * u).astype(BF16)
        y = jnp.dot(hdn, wd_bf[...], preferred_element_type=F32)
        y_ref[...] = _pack_pairs(y.astype(BF16).astype(F32))

    @pl.when(j >= nact_ref[0])
    def _():
        y_ref[...] = jnp.zeros(y_ref.shape, y_ref.dtype)


def _experts(blk_e, nact, xs, wg, wu, wd, bm):
    p, dh = xs.shape
    d, dff = wg.shape[-2:]
    nblk = p // bm
    rowmap = lambda j, blk, na: (jnp.minimum(j, na[0] - 1), 0)
    wmap = lambda j, blk, na: (blk[jnp.minimum(j, na[0] - 1)], 0, 0)
    return pl.pallas_call(
        _expert_kernel,
        grid_spec=pltpu.PrefetchScalarGridSpec(
            num_scalar_prefetch=2,
            grid=(nblk,),
            in_specs=[
                pl.BlockSpec((bm, dh), rowmap),
                pl.BlockSpec((None, d, dff), wmap),
                pl.BlockSpec((None, d, dff), wmap),
                pl.BlockSpec((None, dff, d), wmap),
            ],
            out_specs=pl.BlockSpec((bm, dh), lambda j, blk, na: (j, 0)),
            scratch_shapes=[pltpu.VMEM((d, dff), BF16), pltpu.VMEM((d, dff), BF16), pltpu.VMEM((dff, d), BF16)],
        ),
        out_shape=jax.ShapeDtypeStruct((p, dh), U32),
        compiler_params=pltpu.CompilerParams(
            dimension_semantics=("arbitrary",), vmem_limit_bytes=VMEM_LIMIT),
        name="experts",
    )(blk_e, nact, xs, wg, wu, wd)


def _combine_kernel(tcnt_ref, tbase_ref, pstart_ref, h_ref, lpos_ref, gate_ref, lnf_ref, ys_hbm, o_ref,
                    loc_scr, sem, *, tm):
    i = pl.program_id(0)
    nt = pl.num_programs(0)
    slot = i % 2

    def copies(tile, sl, act):
        def make(lrow, grow, size):
            return pltpu.make_async_copy(ys_hbm.at[pl.ds(grow, size)], loc_scr.at[sl, pl.ds(lrow, size)], sem.at[sl])

        return _segment_copies(tile, tcnt_ref, tbase_ref, pstart_ref, make, act, _local_rows(tm))

    @pl.when(i == 0)
    def _():
        copies(0, 0, "start")

    @pl.when(i + 1 < nt)
    def _():
        copies(i + 1, 1 - slot, "start")

    used = copies(i, slot, "wait")
    row = lax.broadcasted_iota(I32, (_local_rows(tm), 1), 0)
    ys = _unpack_pairs(jnp.where(row < used, loc_scr[slot], U32(0)))

    r8 = lax.broadcasted_iota(I32, (8, tm), 0)
    lp = lpos_ref[...].astype(F32)
    top = jnp.where(r8 == 0, lp[0:1], jnp.where(r8 == 1, lp[1:2], jnp.where(
        r8 == 2, gate_ref[0:1, :], jnp.where(r8 == 3, gate_ref[1:2, :], 0.0))))
    cols = jnp.concatenate([top, jnp.zeros((LANES - 8, tm), F32)], axis=0).T
    pos = lax.broadcasted_iota(I32, (tm, _local_rows(tm)), 1)
    acc = h_ref[...]
    for k in range(2):
        sel = jnp.where(pos == cols[:, k:k + 1].astype(I32), 1.0, 0.0).astype(BF16)
        acc = acc + cols[:, 2 + k:3 + k] * jnp.dot(sel, ys, preferred_element_type=F32)
    o_ref[...] = _rms(acc, lnf_ref[...])


def _combine(tcnt_i, tbase_i, pstart, h2, lpos, gates, lnf, ys, tm):
    n, d = h2.shape
    return pl.pallas_call(
        functools.partial(_combine_kernel, tm=tm),
        grid_spec=pltpu.PrefetchScalarGridSpec(
            num_scalar_prefetch=3,
            grid=(n // tm,),
            in_specs=[
                pl.BlockSpec((tm, d), lambda i, *_: (i, 0)),
                pl.BlockSpec((2, tm), lambda i, *_: (0, i)),
                pl.BlockSpec((2, tm), lambda i, *_: (0, i)),
                pl.BlockSpec((1, d), lambda i, *_: (0, 0)),
                pl.BlockSpec(memory_space=pl.ANY),
            ],
            out_specs=pl.BlockSpec((tm, d), lambda i, *_: (i, 0)),
            scratch_shapes=[pltpu.VMEM((2, _local_rows(tm), d // 2), U32), pltpu.SemaphoreType.DMA((2,))],
        ),
        out_shape=jax.ShapeDtypeStruct((n, d), F32),
        compiler_params=pltpu.CompilerParams(
            dimension_semantics=("arbitrary",), vmem_limit_bytes=VMEM_LIMIT),
        name="combine",
    )(tcnt_i, tbase_i, pstart, h2, lpos, gates, lnf, ys)


def _swa_head_perm():
    cols = []
    for p in range(4):
        for half in range(2):
            head = half * 4 + p
            cols.extend(range(head * D_HEAD, (head + 1) * D_HEAD))
    return jnp.asarray(cols, dtype=I32)


def kernel(x, mem, positions, ln_mix_w, w_in, lambda_q1, lambda_k1, lambda_q2, lambda_k2, subln_w, sinks,
           w_out, ln_cross_w, ln_mem_w, wq_cross, wkv_cross, wo_cross, ln_moe_w, w_group, b_group,
           w_expert, b_expert, w_gate, w_up, w_down, ln_final_w):
    b, s, d = x.shape
    m_len = mem.shape[1]
    n = b * s
    assert w_in.shape[0] == 1 and d == 1024 and n <= 65536
    lambda_init = 0.8 - 0.6 * math.exp(-0.3 * 0)

    tm = 512
    t_attn = 256
    tk_attn = 1024
    bm = 512

    x2 = x.reshape(n, d)
    half = D_HEAD // 2
    per_row = LANES // half
    inv_freq = jnp.exp(-math.log(ROPE_THETA) * jnp.arange(0, D_HEAD, 2, dtype=F32) / D_HEAD)
    inv128 = jnp.tile(inv_freq, per_row).reshape(1, LANES)
    pos_rep = jnp.repeat(positions.reshape(n // per_row, per_row).astype(I32), half, axis=1)
    cos_t, sin_t = _rope_tables(pos_rep, inv128, min(1024, n // per_row))
    cos_t = cos_t.reshape(n, half)
    sin_t = sin_t.reshape(n, half)

    perm = _swa_head_perm()
    sq0 = 3 * 512
    w_in_l = w_in[0]
    w_in_p = jnp.concatenate([w_in_l[:, :sq0], w_in_l[:, sq0:sq0 + 512][:, perm], w_in_l[:, sq0 + 512:]], axis=1)
    w_out_l = w_out[0]
    w_out_p = jnp.concatenate([w_out_l[:512], w_out_l[512:][perm]], axis=0)
    sinks_p = sinks[0].reshape(2, 4).reshape(-1)

    qkv = _inproj(x2, ln_mix_w[0].reshape(1, d), w_in_p.astype(BF16), cos_t, sin_t, tm)
    o_diff = _diff_attention(qkv, lambda_q1[0].reshape(1, -1), lambda_k1[0].reshape(1, -1),
                             lambda_q2[0].reshape(1, -1), lambda_k2[0].reshape(1, -1),
                             subln_w[0].reshape(-1, 1), b, s, t_attn, min(tk_attn, s), lambda_init)
    o_swa = _swa_attention(qkv, sinks_p.astype(F32), b, s, tm)
    kv = _memkv(mem.reshape(b * m_len, d), ln_mem_w[0].reshape(1, d), wkv_cross[0].astype(BF16), m_len)

    wr = jnp.concatenate([w_group[0].T, jnp.zeros((8 - N_GROUPS, d), F32), w_expert[0].T], axis=0)
    br = jnp.concatenate([b_group[0], jnp.zeros((8 - N_GROUPS,), F32), b_expert[0]]).reshape(-1, 1)
    h2, f, lpos, gates, tcnt, tbase = _post(
        x2, o_diff, o_swa, w_out_p.astype(BF16), ln_cross_w[0].reshape(1, d), wq_cross[0].astype(BF16), kv,
        wo_cross[0].astype(BF16), ln_moe_w[0].reshape(1, d), wr, br, b, s, tm, m_len)

    tcnt_i = tcnt[:, :, 0].astype(I32).reshape(-1)
    tbase_i = tbase[:, :, 0].astype(I32).reshape(-1)
    tot_i = tbase_i[-N_EXPERTS:] + tcnt_i[-N_EXPERTS:]
    rows_max = 2 * n + (n // tm) * N_EXPERTS * (SUBLANES - 1)
    nblk = (rows_max + N_EXPERTS * (bm - 1) + bm - 1) // bm
    xs, pstart, blk_e, nact = _dispatch(tcnt_i, tbase_i, tot_i, f, lpos, tm, bm, nblk)
    ys = _experts(blk_e, nact, xs, w_gate[0], w_up[0], w_down[0], bm)
    out = _combine(tcnt_i, tbase_i, pstart, h2, lpos, gates, ln_final_w.reshape(1, d), ys, tm)
    return out.reshape(b, s, d)
```

```python
import functools
import math

import jax
import jax.numpy as jnp
from jax import lax
from jax.experimental import pallas as pl
from jax.experimental.pallas import tpu as pltpu

F32 = jnp.float32
BF16 = jnp.bfloat16
I32 = jnp.int32

D_HEAD = 64
ROPE_THETA = 10000.0
H_DIFF = 4
N_Q_SWA = 8
N_KV_SWA = 2
WINDOW = 128
H_CROSS = 4
N_GROUPS = 4
E_PER_GROUP = 8
N_EXPERTS = N_GROUPS * E_PER_GROUP
EPS = 1e-6
LANES = 128
SUBLANES = 8
LONG_SEGMENT_ROWS = 128
IN_W = 2304
NEG = -1e30

VMEM_LIMIT = 56 * 1024 * 1024

NT_DIMS = (((1,), (1,)), ((), ()))


U32 = jnp.uint32


def _pack_pairs(x):
    h = x.shape[1] // 2
    hi = lax.bitcast_convert_type(x[:, :h], U32)
    lo = lax.bitcast_convert_type(x[:, h:], U32)
    return hi | (lo >> 16)


def _unpack_pairs(p):
    a = lax.bitcast_convert_type(p & U32(0xFFFF0000), F32)
    b = lax.bitcast_convert_type(p << 16, F32)
    return jnp.concatenate([a, b], axis=1).astype(BF16)


def _rms(x, w):
    ms = jnp.mean(x * x, axis=-1, keepdims=True)
    return x * lax.rsqrt(ms + EPS) * w


def _rope_table_kernel(pos_ref, inv_ref, cos_ref, sin_ref):
    ang = pos_ref[...].astype(F32) * inv_ref[...]
    cos_ref[...] = jnp.cos(ang)
    sin_ref[...] = jnp.sin(ang)


def _rope_tables(pos_rep, inv128, rows):
    n4 = pos_rep.shape[0]
    spec = pl.BlockSpec((rows, LANES), lambda i: (i, 0))
    return pl.pallas_call(
        _rope_table_kernel,
        grid=(n4 // rows,),
        in_specs=[spec, pl.BlockSpec((1, LANES), lambda i: (0, 0))],
        out_specs=[spec, spec],
        out_shape=[jax.ShapeDtypeStruct((n4, LANES), F32)] * 2,
        compiler_params=pltpu.CompilerParams(dimension_semantics=("arbitrary",)),
        name="ropetab",
    )(pos_rep, inv128)


def _inproj_kernel(x_ref, lnw_ref, w_ref, cos_ref, sin_ref, o_ref):
    a = _rms(x_ref[...], lnw_ref[...]).astype(BF16)
    reps = LANES // (D_HEAD // 2)
    cos = jnp.concatenate([cos_ref[...]] * reps, axis=1)
    sin = jnp.concatenate([sin_ref[...]] * reps, axis=1)
    lane = lax.broadcasted_iota(I32, (1, LANES), 1)
    first = (lane % D_HEAD) < (D_HEAD // 2)
    sin_signed = jnp.where(first, -sin, sin)
    n_chunks = IN_W // 256
    value_chunks = (4, 5)
    for c in [c for c in range(n_chunks) if c not in value_chunks] + list(value_chunks):
        p = jnp.dot(a, w_ref[:, c * 256:(c + 1) * 256], preferred_element_type=F32)
        for hh in range(2):
            g = c * 2 + hh
            xg = p[:, hh * LANES:(hh + 1) * LANES]
            is_v = (8 <= g < 12) or g == 17
            if not is_v:
                partner = jnp.where(first, pltpu.roll(xg, 96, 1), pltpu.roll(xg, 32, 1))
                xg = xg * cos + partner * sin_signed
                if g < 4:
                    xg = xg * (D_HEAD ** -0.5 * math.log2(math.e))
                elif 12 <= g < 16:
                    xg = xg * (D_HEAD ** -0.5)
            o_ref[:, g * LANES:(g + 1) * LANES] = xg.astype(BF16)


def _inproj(x2, ln_w, w_in_bf, cos, sin, tm):
    n, d = x2.shape
    half = D_HEAD // 2
    return pl.pallas_call(
        _inproj_kernel,
        grid=(n // tm,),
        in_specs=[
            pl.BlockSpec((tm, d), lambda i: (i, 0)),
            pl.BlockSpec((1, d), lambda i: (0, 0)),
            pl.BlockSpec((d, IN_W), lambda i: (0, 0)),
            pl.BlockSpec((tm, half), lambda i: (i, 0)),
            pl.BlockSpec((tm, half), lambda i: (i, 0)),
        ],
        out_specs=pl.BlockSpec((tm, IN_W), lambda i: (i, 0)),
        out_shape=jax.ShapeDtypeStruct((n, IN_W), BF16),
        compiler_params=pltpu.CompilerParams(
            dimension_semantics=("arbitrary",), vmem_limit_bytes=VMEM_LIMIT),
        name="inproj",
    )(x2, ln_w, w_in_bf, cos, sin)


def _diff_group_kernel(q_ref, k_ref, v_ref, lq1_ref, lk1_ref, lq2_ref, lk2_ref, sw_ref, o_ref,
                       vt_scr, q2_scr, acc_scr, m_scr, l_scr, s_scr, *, t, tk, lambda_init):
    s = k_ref.shape[0]
    per = tk // t
    for c in range(s // tk):
        for r in range(per):
            rows = slice(c * tk + r * t, c * tk + (r + 1) * t)
            vt_scr[c, :, r * t:(r + 1) * t] = v_ref[rows, :].astype(F32).T.astype(BF16)

    lane = lax.broadcasted_iota(I32, (1, LANES), 1)
    lam = (jnp.exp(jnp.sum(lq1_ref[...] * lk1_ref[...], axis=1, keepdims=True))
           - jnp.exp(jnp.sum(lq2_ref[...] * lk2_ref[...], axis=1, keepdims=True))
           + lambda_init)

    def accumulate(r, st_parts, vt):
        m_prev = m_scr[r]
        m_new = m_prev
        for st in st_parts:
            m_new = jnp.maximum(m_new, jnp.max(st, axis=0, keepdims=True))
        ps = [jnp.exp2(st - m_new) for st in st_parts]
        alpha = jnp.exp2(m_prev - m_new)
        l_new = alpha * l_scr[r]
        for p in ps:
            l_new = l_new + jnp.sum(p, axis=0, keepdims=True)
        l_scr[r] = l_new
        pb = [p.astype(BF16) for p in ps]
        pb = pb[0] if len(pb) == 1 else jnp.concatenate(pb, axis=0)
        acc_scr[r] = alpha * acc_scr[r] + jnp.dot(vt, pb, preferred_element_type=F32)
        m_scr[r] = m_new

    def interleaved(score_fn, softmax_fn, next_first_scores):
        for r in range(per):
            if r + 1 < per:
                score_fn(r + 1)
            else:
                next_first_scores()
            softmax_fn(r)

    def group(gi, carry):
        row0 = pl.multiple_of(gi * tk, tk)
        for r in range(per):
            q = q_ref[pl.ds(row0 + r * t, t), :]
            zero = jnp.zeros_like(q)
            q2_scr[r] = jnp.concatenate(
                [jnp.where(lane < D_HEAD, q, zero), jnp.where(lane >= D_HEAD, q, zero)], axis=0)
        acc_scr[...] = jnp.zeros(acc_scr.shape, F32)
        m_scr[...] = jnp.full(m_scr.shape, NEG, F32)
        l_scr[...] = jnp.zeros(l_scr.shape, F32)

        def diag_scores(r):
            rows = (r + 1) * t
            k = k_ref[pl.ds(row0, rows), :]
            s_scr[r, 0:rows, :] = lax.dot_general(k, q2_scr[r], NT_DIMS, preferred_element_type=F32)

        def diag_softmax(r):
            rows = (r + 1) * t
            key = lax.broadcasted_iota(I32, (t, 2 * t), 0)
            col = lax.broadcasted_iota(I32, (t, 2 * t), 1)
            diag = jnp.where(key <= jnp.where(col >= t, col - t, col), s_scr[r, r * t:rows, :], NEG)
            parts = [diag] if r == 0 else [s_scr[r, 0:r * t, :], diag]
            accumulate(r, parts, vt_scr[gi, :, 0:rows])

        def full_scores(j, r):
            k = k_ref[pl.ds(pl.multiple_of(j * tk, tk), tk), :]
            s_scr[r] = lax.dot_general(k, q2_scr[r], NT_DIMS, preferred_element_type=F32)

        diag_scores(0)
        interleaved(diag_scores, diag_softmax, functools.partial(full_scores, 0, 0))

        def key_block(j, c2):
            interleaved(functools.partial(full_scores, j), lambda r: accumulate(r, [s_scr[r]], vt_scr[j]),
                        functools.partial(full_scores, j + 1, 0))
            return c2

        lax.fori_loop(0, gi, key_block, 0)

        for r in range(per):
            on = acc_scr[r] / l_scr[r]
            o = on[:, :t] - lam * on[:, t:]
            ms = jnp.mean(o * o, axis=0, keepdims=True)
            o = o * lax.rsqrt(ms + EPS) * sw_ref[...] * (1.0 - lambda_init)
            o_ref[pl.ds(row0 + r * t, t), :] = o.T.astype(BF16)
        return carry

    lax.fori_loop(0, s // tk, group, 0)


def _diff_attention(qkv, lq1, lk1, lq2, lk2, subln_col, b, s, t, tk, lambda_init):
    n = qkv.shape[0]
    per = tk // t
    small = pl.BlockSpec((1, D_HEAD), lambda bi, h: (0, 0))
    return pl.pallas_call(
        functools.partial(_diff_group_kernel, t=t, tk=tk, lambda_init=lambda_init),
        grid=(b, H_DIFF),
        in_specs=[
            pl.BlockSpec((s, LANES), lambda bi, h: (bi, h)),
            pl.BlockSpec((s, LANES), lambda bi, h: (bi, 4 + h)),
            pl.BlockSpec((s, LANES), lambda bi, h: (bi, 8 + h)),
            small, small, small, small,
            pl.BlockSpec((LANES, 1), lambda bi, h: (0, 0)),
        ],
        out_specs=pl.BlockSpec((s, LANES), lambda bi, h: (bi, h)),
        out_shape=jax.ShapeDtypeStruct((n, H_DIFF * LANES), BF16),
        scratch_shapes=[
            pltpu.VMEM((s // tk, LANES, tk), BF16),
            pltpu.VMEM((per, 2 * t, LANES), BF16),
            pltpu.VMEM((per, LANES, 2 * t), F32),
            pltpu.VMEM((per, 1, 2 * t), F32),
            pltpu.VMEM((per, 1, 2 * t), F32),
            pltpu.VMEM((per, tk, 2 * t), F32),
        ],
        compiler_params=pltpu.CompilerParams(
            dimension_semantics=("arbitrary", "arbitrary"), vmem_limit_bytes=VMEM_LIMIT),
        name="diffattn",
    )(qkv, qkv, qkv, lq1, lk1, lq2, lk2, subln_col)


def _swa_kernel(sink_ref, q_ref, kc_ref, kp_ref, vc_ref, vp_ref, o_ref, kbuf, vbuf, *, tq):
    i = pl.program_id(1)
    w = WINDOW
    kbuf[0:w, :] = kp_ref[...]
    kbuf[w:w + tq, :] = kc_ref[...]
    vbuf[0:w, :] = vp_ref[...]
    vbuf[w:w + tq, :] = vc_ref[...]
    lane = lax.broadcasted_iota(I32, (1, LANES), 1)
    lo = lane < D_HEAD
    qi = lax.broadcasted_iota(I32, (w, 2 * w), 0)
    ki = lax.broadcasted_iota(I32, (w, 2 * w), 1)
    band = (ki > qi) & (ki <= qi + w)
    band_first = band & (ki >= jnp.where(i > 0, 0, w))
    for r in range(tq // w):
        keys = kbuf[r * w:(r + 2) * w, :]
        vals = vbuf[r * w:(r + 2) * w, :]
        valid = band_first if r == 0 else band
        qs = []
        for half in range(2):
            for p in range(4):
                qg = q_ref[r * w:(r + 1) * w, p * LANES:(p + 1) * LANES]
                qs.append(jnp.where(lo if half == 0 else ~lo, qg, jnp.zeros_like(qg)))
        qstack = jnp.concatenate(qs, axis=0)
        s_all = lax.dot_general(qstack, keys, NT_DIMS, preferred_element_type=F32)
        ps = []
        for hb in range(8):
            sink = sink_ref[hb]
            s = jnp.where(valid, s_all[hb * w:(hb + 1) * w], NEG)
            m = jnp.maximum(jnp.max(s, axis=1, keepdims=True), sink)
            e = jnp.exp(s - m)
            den = jnp.sum(e, axis=1, keepdims=True) + jnp.exp(sink - m)
            ps.append((e / den).astype(BF16))
        pv = jnp.dot(jnp.concatenate(ps, axis=0), vals, preferred_element_type=F32)
        for p in range(4):
            og = jnp.where(lo, pv[p * w:(p + 1) * w], pv[(4 + p) * w:(5 + p) * w])
            o_ref[r * w:(r + 1) * w, p * LANES:(p + 1) * LANES] = og.astype(BF16)


def _swa_attention(qkv, sinks, b, s, tq):
    n = qkv.shape[0]
    nq = s // tq
    per = tq // WINDOW
    prev_map = lambda col: (lambda bi, i, sk: (jnp.maximum(bi * (s // WINDOW) + i * per - 1, 0), col))
    cur_map = lambda col: (lambda bi, i, sk: (bi * nq + i, col))
    return pl.pallas_call(
        functools.partial(_swa_kernel, tq=tq),
        grid_spec=pltpu.PrefetchScalarGridSpec(
            num_scalar_prefetch=1,
            grid=(b, nq),
            in_specs=[
                pl.BlockSpec((tq, 4 * LANES), lambda bi, i, sk: (bi * nq + i, 3)),
                pl.BlockSpec((tq, LANES), cur_map(16)),
                pl.BlockSpec((WINDOW, LANES), prev_map(16)),
                pl.BlockSpec((tq, LANES), cur_map(17)),
                pl.BlockSpec((WINDOW, LANES), prev_map(17)),
            ],
            out_specs=pl.BlockSpec((tq, 4 * LANES), lambda bi, i, sk: (bi * nq + i, 0)),
            scratch_shapes=[pltpu.VMEM((WINDOW + tq, LANES), BF16)] * 2,
        ),
        out_shape=jax.ShapeDtypeStruct((n, 4 * LANES), BF16),
        compiler_params=pltpu.CompilerParams(
            dimension_semantics=("arbitrary", "arbitrary"), vmem_limit_bytes=VMEM_LIMIT),
        name="swa",
    )(sinks, qkv, qkv, qkv, qkv, qkv)


def _memkv_kernel(m_ref, lnw_ref, w_ref, o_ref):
    a = _rms(m_ref[...], lnw_ref[...]).astype(BF16)
    o_ref[...] = jnp.dot(a, w_ref[...], preferred_element_type=F32).astype(BF16)


def _memkv(mem2, ln_w, wkv_bf, m_len):
    n, d = mem2.shape
    return pl.pallas_call(
        _memkv_kernel,
        grid=(n // m_len,),
        in_specs=[
            pl.BlockSpec((m_len, d), lambda i: (i, 0)),
            pl.BlockSpec((1, d), lambda i: (0, 0)),
            pl.BlockSpec((d, 2 * d), lambda i: (0, 0)),
        ],
        out_specs=pl.BlockSpec((m_len, 2 * d), lambda i: (i, 0)),
        out_shape=jax.ShapeDtypeStruct((n, 2 * d), BF16),
        compiler_params=pltpu.CompilerParams(
            dimension_semantics=("arbitrary",), vmem_limit_bytes=VMEM_LIMIT),
        name="memkv",
    )(mem2, ln_w, wkv_bf)


def _split_bf16(x):
    hi = x.astype(BF16)
    lo = (x - hi.astype(F32)).astype(BF16)
    return hi, lo


def _post_kernel(x_ref, od_ref, os_ref, wout_ref, lnc_ref, wq_ref, kv_ref, wo_ref, lnm_ref, wr_ref, br_ref,
                 h_ref, f_ref, lpos_ref, gate_ref, tcnt_ref, tbase_ref, base_scr, fprev_scr, *, tm, d):
    g = pl.program_id(0)

    @pl.when(g == 0)
    def _():
        base_scr[...] = jnp.zeros(base_scr.shape, F32)
        fprev_scr[...] = jnp.zeros(fprev_scr.shape, F32)

    mix = jnp.concatenate([od_ref[...], os_ref[...]], axis=1)
    h1 = x_ref[...] + jnp.dot(mix, wout_ref[...], preferred_element_type=F32)
    logits = _route_logits(fprev_scr[...], wr_ref, br_ref)

    c = _rms(h1, lnc_ref[...]).astype(BF16)
    dc = d // H_CROSS
    q = (jnp.dot(c, wq_ref[...], preferred_element_type=F32) * (dc ** -0.5)).astype(BF16)
    eid1, eid2 = _route_select(logits, gate_ref, tm)
    outs = []
    for hd in range(H_CROSS):
        k = kv_ref[:, hd * dc:(hd + 1) * dc]
        v = kv_ref[:, d + hd * dc:d + (hd + 1) * dc]
        s = lax.dot_general(q[:, hd * dc:(hd + 1) * dc], k, NT_DIMS, preferred_element_type=F32)
        e = jnp.exp(s - jnp.max(s, axis=1, keepdims=True))
        den = jnp.sum(e, axis=1, keepdims=True)
        outs.append((jnp.dot(e.astype(BF16), v, preferred_element_type=F32) / den).astype(BF16))
    o = jnp.concatenate(outs, axis=1)
    h2 = h1 + jnp.dot(o, wo_ref[...], preferred_element_type=F32)
    _route_positions(eid1, eid2, jnp.where(g > 0, 1.0, 0.0), lpos_ref, tcnt_ref, tbase_ref, base_scr, tm)
    h_ref[...] = h2
    f = _rms(h2, lnm_ref[...])
    f_ref[...] = f.astype(BF16)
    fprev_scr[...] = f


def _route_logits(f, wr_ref, br_ref):
    f_hi, f_lo = _split_bf16(f)
    w_hi, w_lo = _split_bf16(wr_ref[...])
    return (lax.dot_general(w_hi, f_hi, NT_DIMS, preferred_element_type=F32)
            + lax.dot_general(w_hi, f_lo, NT_DIMS, preferred_element_type=F32)
            + lax.dot_general(w_lo, f_hi, NT_DIMS, preferred_element_type=F32)) + br_ref[...]


def _route_select(lg, gate_ref, tm):
    gl = lg[0:N_GROUPS]
    gmax = jnp.max(gl, axis=0, keepdims=True)
    gidx = lax.broadcasted_iota(I32, gl.shape, 0)
    g_sel = jnp.min(jnp.where(gl == gmax, gidx, N_GROUPS), axis=0, keepdims=True)
    g_p = 1.0 / jnp.sum(jnp.exp(gl - gmax), axis=0, keepdims=True)

    e8 = jnp.zeros((E_PER_GROUP, tm), F32)
    for g in range(N_GROUPS):
        e8 = e8 + jnp.where(g_sel == g, lg[8 + g * E_PER_GROUP:8 + (g + 1) * E_PER_GROUP], 0.0)
    ex = jnp.exp(e8 - jnp.max(e8, axis=0, keepdims=True))
    ep = ex / jnp.sum(ex, axis=0, keepdims=True)
    idx8 = lax.broadcasted_iota(I32, ep.shape, 0)
    p1 = jnp.max(ep, axis=0, keepdims=True)
    i1 = jnp.min(jnp.where(ep == p1, idx8, E_PER_GROUP), axis=0, keepdims=True)
    ep2 = jnp.where(idx8 == i1, -1.0, ep)
    p2 = jnp.max(ep2, axis=0, keepdims=True)
    i2 = jnp.min(jnp.where(ep2 == p2, idx8, E_PER_GROUP), axis=0, keepdims=True)
    psum = p1 + p2
    gate_ref[0:1, :] = g_p * (p1 / psum)
    gate_ref[1:2, :] = g_p * (p2 / psum)
    return g_sel * E_PER_GROUP + i1, g_sel * E_PER_GROUP + i2


def _route_positions(eid1, eid2, live, lpos_ref, tcnt_ref, tbase_ref, base_scr, tm):
    e32 = lax.broadcasted_iota(I32, (N_EXPERTS, tm), 0)
    oh1 = (e32 == eid1).astype(F32)
    oh2 = (e32 == eid2).astype(F32)
    cnt = oh1 + oh2
    tr = lax.broadcasted_iota(I32, (tm, tm), 0)
    tc = lax.broadcasted_iota(I32, (tm, tm), 1)
    upper = jnp.where(tr < tc, 1.0, 0.0).astype(BF16)
    earlier = jnp.dot(cnt.astype(BF16), upper, preferred_element_type=F32)
    seg = jnp.sum(cnt, axis=1, keepdims=True)
    seg = jnp.floor((seg + (SUBLANES - 1)) * (1.0 / SUBLANES)) * SUBLANES * live
    seg = jnp.broadcast_to(seg, (N_EXPERTS, LANES))
    er = lax.broadcasted_iota(I32, (N_EXPERTS, N_EXPERTS), 0)
    ec = lax.broadcasted_iota(I32, (N_EXPERTS, N_EXPERTS), 1)
    lower = jnp.where(ec < er, 1.0, 0.0).astype(BF16)
    start = jnp.dot(lower, seg.astype(BF16), preferred_element_type=F32)[:, 0:1]
    where = earlier + start
    lpos_ref[0:1, :] = jnp.sum(oh1 * where, axis=0, keepdims=True).astype(I32)
    lpos_ref[1:2, :] = jnp.sum(oh2 * where, axis=0, keepdims=True).astype(I32)
    tcnt_ref[...] = seg
    tbase_ref[...] = base_scr[...]
    base_scr[...] = base_scr[...] + seg


def _post(x2, od, osw, wout_bf, lnc, wq_bf, kv, wo_bf, lnm, wr, br, b, s, tm, m_len):
    n, d = x2.shape
    nt = s // tm
    tiles = n // tm
    main = lambda g: jnp.minimum(g, tiles - 1)
    routed = lambda g: jnp.maximum(g - 1, 0)
    row = lambda g: (main(g), 0)
    const = lambda g: (0, 0)
    return pl.pallas_call(
        functools.partial(_post_kernel, tm=tm, d=d),
        grid=(tiles + 1,),
        in_specs=[
            pl.BlockSpec((tm, d), row),
            pl.BlockSpec((tm, d // 2), row),
            pl.BlockSpec((tm, d // 2), row),
            pl.BlockSpec((d, d), const),
            pl.BlockSpec((1, d), const),
            pl.BlockSpec((d, d), const),
            pl.BlockSpec((m_len, 2 * d), lambda g: (main(g) // nt, 0)),
            pl.BlockSpec((d, d), const),
            pl.BlockSpec((1, d), const),
            pl.BlockSpec((8 + N_EXPERTS, d), const),
            pl.BlockSpec((8 + N_EXPERTS, 1), const),
        ],
        out_specs=[
            pl.BlockSpec((tm, d), row),
            pl.BlockSpec((tm, d), row),
            pl.BlockSpec((2, tm), lambda g: (0, routed(g))),
            pl.BlockSpec((2, tm), lambda g: (0, routed(g))),
            pl.BlockSpec((None, N_EXPERTS, LANES), lambda g: (routed(g), 0, 0)),
            pl.BlockSpec((None, N_EXPERTS, LANES), lambda g: (routed(g), 0, 0)),
        ],
        out_shape=[
            jax.ShapeDtypeStruct((n, d), F32),
            jax.ShapeDtypeStruct((n, d), BF16),
            jax.ShapeDtypeStruct((2, n), I32),
            jax.ShapeDtypeStruct((2, n), F32),
            jax.ShapeDtypeStruct((n // tm, N_EXPERTS, LANES), F32),
            jax.ShapeDtypeStruct((n // tm, N_EXPERTS, LANES), F32),
        ],
        scratch_shapes=[pltpu.VMEM((N_EXPERTS, LANES), F32), pltpu.VMEM((tm, d), F32)],
        compiler_params=pltpu.CompilerParams(
            dimension_semantics=("arbitrary",), vmem_limit_bytes=VMEM_LIMIT),
        name="post",
    )(x2, od, osw, wout_bf, lnc, wq_bf, kv, wo_bf, lnm, wr, br)


def _local_rows(tm):
    return 2 * tm + SUBLANES * N_EXPERTS


def _segment_copies(tile, tcnt_ref, tbase_ref, pstart_ref, make, act, max_len):
    sizes = [SUBLANES << k for k in reversed(range((max_len // SUBLANES).bit_length()))]

    if act == "wait":
        total = lax.fori_loop(0, N_EXPERTS, lambda e, acc: acc + tcnt_ref[tile * N_EXPERTS + e], 0)
        for size in sizes:
            @pl.when((total & size) != 0)
            def _():
                make(0, 0, size).wait()

        return total

    def pieces(ln, local0, glob0, some_sizes):
        for size in some_sizes:
            @pl.when((ln & size) != 0)
            def _():
                off = (ln // (2 * size)) * (2 * size)
                make(pl.multiple_of(local0 + off, SUBLANES), pl.multiple_of(glob0 + off, SUBLANES), size).start()

    big = [s for s in sizes if s >= LONG_SEGMENT_ROWS]
    small = [s for s in sizes if s < LONG_SEGMENT_ROWS]

    def per_expert(e, local0):
        ln = tcnt_ref[tile * N_EXPERTS + e]
        glob0 = pstart_ref[e] + tbase_ref[tile * N_EXPERTS + e]

        @pl.when(ln >= LONG_SEGMENT_ROWS)
        def _():
            pieces(ln, local0, glob0, big)

        pieces(ln, local0, glob0, small)
        return local0 + ln

    return lax.fori_loop(0, N_EXPERTS, per_expert, 0)


def _dispatch_kernel(tcnt_ref, tbase_ref, tot_ref, f_ref, lpos_ref, gate_ref, xs_hbm, pstart_ref, blk_ref, nact_ref,
                     loc_scr, zero_scr, sem, zsem, *, tm, bm, nblk):
    i = pl.program_id(0)
    nt = pl.num_programs(0)
    slot = i % 2

    @pl.when(i == 0)
    def _():
        def per_expert(e, blk0):
            nb = (tot_ref[e] + (bm - 1)) // bm
            pstart_ref[e] = blk0 * bm

            def fill(j, carry):
                blk_ref[j] = e
                return carry

            lax.fori_loop(blk0, blk0 + nb, fill, 0)
            return blk0 + nb

        nact = lax.fori_loop(0, N_EXPERTS, per_expert, 0)
        nact_ref[0] = nact

        def tail(j, carry):
            blk_ref[j] = N_EXPERTS - 1
            return carry

        lax.fori_loop(nact, nblk, tail, 0)

        zero_scr[...] = jnp.zeros(zero_scr.shape, U32)

        def zero_fill(act):
            def do(copy):
                if act == "start":
                    copy.start()
                else:
                    copy.wait()

            def per_expert_pad(e, carry):
                tot = tot_ref[e]
                pad = (tot + (bm - 1)) // bm * bm - tot
                row0 = pstart_ref[e] + tot
                for k in reversed(range((bm // SUBLANES).bit_length())):
                    size = SUBLANES << k

                    @pl.when((pad & size) != 0)
                    def _():
                        row = pl.multiple_of(row0 + (pad // (2 * size)) * (2 * size), SUBLANES)
                        do(pltpu.make_async_copy(zero_scr.at[pl.ds(0, size)], xs_hbm.at[pl.ds(row, size)], zsem))

                return carry

            lax.fori_loop(0, N_EXPERTS, per_expert_pad, 0)

            def per_unused_block(j, carry):
                do(pltpu.make_async_copy(zero_scr, xs_hbm.at[pl.ds(pl.multiple_of(j * bm, bm), bm)], zsem))
                return carry

            lax.fori_loop(nact, nblk, per_unused_block, 0)

        zero_fill("start")
        zero_fill("wait")

    def copies(tile, sl, act):
        def make(lrow, grow, size):
            return pltpu.make_async_copy(loc_scr.at[sl, pl.ds(lrow, size)], xs_hbm.at[pl.ds(grow, size)], sem.at[sl])

        _segment_copies(tile, tcnt_ref, tbase_ref, pstart_ref, make, act, _local_rows(tm))

    @pl.when(i >= 2)
    def _():
        copies(i - 2, slot, "wait")

    pos = lax.broadcasted_iota(I32, (_local_rows(tm), tm), 0)
    first = pos == lpos_ref[0:1, :]
    second = pos == lpos_ref[1:2, :]
    onehot = jnp.where(first, 1.0, jnp.where(second, 1.0, 0.0)).astype(BF16)
    half = f_ref.shape[1] // 2
    loc_scr[slot, :, 0:half] = _pack_pairs(jnp.dot(onehot, f_ref[...], preferred_element_type=F32))
    gate = jnp.sum(jnp.where(first, gate_ref[0:1, :], jnp.where(second, gate_ref[1:2, :], 0.0)),
                   axis=1, keepdims=True)
    gate = jnp.broadcast_to(gate, (_local_rows(tm), LANES))
    loc_scr[slot, :, half:half + LANES] = lax.bitcast_convert_type(gate, U32)
    copies(i, slot, "start")

    @pl.when(i == nt - 1)
    def _():
        @pl.when(i >= 1)
        def _():
            copies(i - 1, 1 - slot, "wait")

        copies(i, slot, "wait")


def _dispatch(tcnt_i, tbase_i, tot_i, f, lpos, gates, tm, bm, nblk):
    n, d = f.shape
    width = d // 2 + LANES
    smem = pl.BlockSpec(memory_space=pltpu.SMEM)
    hbm = pl.BlockSpec(memory_space=pl.ANY)
    return pl.pallas_call(
        functools.partial(_dispatch_kernel, tm=tm, bm=bm, nblk=nblk),
        grid_spec=pltpu.PrefetchScalarGridSpec(
            num_scalar_prefetch=3,
            grid=(n // tm,),
            in_specs=[
                pl.BlockSpec((tm, d), lambda i, *_: (i, 0)),
                pl.BlockSpec((2, tm), lambda i, *_: (0, i)),
                pl.BlockSpec((2, tm), lambda i, *_: (0, i)),
            ],
            out_specs=[hbm, smem, smem, smem],
            scratch_shapes=[
                pltpu.VMEM((2, _local_rows(tm), width), U32),
                pltpu.VMEM((bm, width), U32),
                pltpu.SemaphoreType.DMA((2,)),
                pltpu.SemaphoreType.DMA,
            ],
        ),
        out_shape=[
            jax.ShapeDtypeStruct((nblk * bm, width), U32),
            jax.ShapeDtypeStruct((N_EXPERTS,), I32),
            jax.ShapeDtypeStruct((nblk,), I32),
            jax.ShapeDtypeStruct((1,), I32),
        ],
        compiler_params=pltpu.CompilerParams(
            dimension_semantics=("arbitrary",), vmem_limit_bytes=VMEM_LIMIT),
        name="dispatch",
    )(tcnt_i, tbase_i, tot_i, f, lpos, gates)


def _expert_kernel(blk_ref, nact_ref, x_ref, wg_ref, wu_ref, wd_ref, y_ref, wg_bf, wu_bf, wd_bf):
    j = pl.program_id(0)
    active = j < nact_ref[0]

    @pl.when(active & ((j == 0) | (blk_ref[j] != blk_ref[jnp.maximum(j - 1, 0)])))
    def _():
        wg_bf[...] = wg_ref[...].astype(BF16)
        wu_bf[...] = wu_ref[...].astype(BF16)
        wd_bf[...] = wd_ref[...].astype(BF16)

    @pl.when(active)
    def _():
        half = y_ref.shape[1]
        x = _unpack_pairs(x_ref[:, 0:half])
        gate = lax.bitcast_convert_type(x_ref[:, half:half + 1], F32)
        g = jnp.dot(x, wg_bf[...], preferred_element_type=F32)
        u = jnp.dot(x, wu_bf[...], preferred_element_type=F32)
        hdn = (g * jax.nn.sigmoid(g) * u).astype(BF16)
        y = jnp.dot(hdn, wd_bf[...], preferred_element_type=F32) * gate
        y_ref[...] = _pack_pairs(y.astype(BF16).astype(F32))

    @pl.when(j >= nact_ref[0])
    def _():
        y_ref[...] = jnp.zeros(y_ref.shape, y_ref.dtype)


def _experts(blk_e, nact, xs, wg, wu, wd, bm):
    p, width = xs.shape
    d, dff = wg.shape[-2:]
    dh = d // 2
    nblk = p // bm
    rowmap = lambda j, blk, na: (jnp.minimum(j, na[0] - 1), 0)
    wmap = lambda j, blk, na: (blk[jnp.minimum(j, na[0] - 1)], 0, 0)
    return pl.pallas_call(
        _expert_kernel,
        grid_spec=pltpu.PrefetchScalarGridSpec(
            num_scalar_prefetch=2,
            grid=(nblk,),
            in_specs=[
                pl.BlockSpec((bm, width), rowmap),
                pl.BlockSpec((None, d, dff), wmap),
                pl.BlockSpec((None, d, dff), wmap),
                pl.BlockSpec((None, dff, d), wmap),
            ],
            out_specs=pl.BlockSpec((bm, dh), lambda j, blk, na: (j, 0)),
            scratch_shapes=[pltpu.VMEM((d, dff), BF16), pltpu.VMEM((d, dff), BF16), pltpu.VMEM((dff, d), BF16)],
        ),
        out_shape=jax.ShapeDtypeStruct((p, dh), U32),
        compiler_params=pltpu.CompilerParams(
            dimension_semantics=("arbitrary",), vmem_limit_bytes=VMEM_LIMIT),
        name="experts",
    )(blk_e, nact, xs, wg, wu, wd)


def _combine_kernel(tcnt_ref, tbase_ref, pstart_ref, h_ref, lpos_ref, lnf_ref, ys_hbm, o_ref,
                    loc_scr, sem, *, tm):
    i = pl.program_id(0)
    nt = pl.num_programs(0)
    slot = i % 2

    def copies(tile, sl, act):
        def make(lrow, grow, size):
            return pltpu.make_async_copy(ys_hbm.at[pl.ds(grow, size)], loc_scr.at[sl, pl.ds(lrow, size)], sem.at[sl])

        return _segment_copies(tile, tcnt_ref, tbase_ref, pstart_ref, make, act, _local_rows(tm))

    @pl.when(i == 0)
    def _():
        copies(0, 0, "start")

    @pl.when(i + 1 < nt)
    def _():
        copies(i + 1, 1 - slot, "start")

    used = copies(i, slot, "wait")
    row = lax.broadcasted_iota(I32, (_local_rows(tm), 1), 0)
    ys = _unpack_pairs(jnp.where(row < used, loc_scr[slot], U32(0)))

    r8 = lax.broadcasted_iota(I32, (8, tm), 0)
    lp = lpos_ref[...].astype(F32)
    top = jnp.where(r8 == 0, lp[0:1], jnp.where(r8 == 1, lp[1:2], 0.0))
    cols = jnp.concatenate([top, jnp.zeros((LANES - 8, tm), F32)], axis=0).T.astype(I32)
    pos = lax.broadcasted_iota(I32, (tm, _local_rows(tm)), 1)
    sel = jnp.where(pos == cols[:, 0:1], 1.0, jnp.where(pos == cols[:, 1:2], 1.0, 0.0)).astype(BF16)
    o_ref[...] = _rms(h_ref[...] + jnp.dot(sel, ys, preferred_element_type=F32), lnf_ref[...])


def _combine(tcnt_i, tbase_i, pstart, h2, lpos, lnf, ys, tm):
    n, d = h2.shape
    return pl.pallas_call(
        functools.partial(_combine_kernel, tm=tm),
        grid_spec=pltpu.PrefetchScalarGridSpec(
            num_scalar_prefetch=3,
            grid=(n // tm,),
            in_specs=[
                pl.BlockSpec((tm, d), lambda i, *_: (i, 0)),
                pl.BlockSpec((2, tm), lambda i, *_: (0, i)),
                pl.BlockSpec((1, d), lambda i, *_: (0, 0)),
                pl.BlockSpec(memory_space=pl.ANY),
            ],
            out_specs=pl.BlockSpec((tm, d), lambda i, *_: (i, 0)),
            scratch_shapes=[pltpu.VMEM((2, _local_rows(tm), d // 2), U32), pltpu.SemaphoreType.DMA((2,))],
        ),
        out_shape=jax.ShapeDtypeStruct((n, d), F32),
        compiler_params=pltpu.CompilerParams(
            dimension_semantics=("arbitrary",), vmem_limit_bytes=VMEM_LIMIT),
        name="combine",
    )(tcnt_i, tbase_i, pstart, h2, lpos, lnf, ys)


def _swa_head_perm():
    cols = []
    for p in range(4):
        for half in range(2):
            head = half * 4 + p
            cols.extend(range(head * D_HEAD, (head + 1) * D_HEAD))
    return jnp.asarray(cols, dtype=I32)


def kernel(x, mem, positions, ln_mix_w, w_in, lambda_q1, lambda_k1, lambda_q2, lambda_k2, subln_w, sinks,
           w_out, ln_cross_w, ln_mem_w, wq_cross, wkv_cross, wo_cross, ln_moe_w, w_group, b_group,
           w_expert, b_expert, w_gate, w_up, w_down, ln_final_w):
    b, s, d = x.shape
    m_len = mem.shape[1]
    n = b * s
    assert w_in.shape[0] == 1 and d == 1024 and n <= 65536
    lambda_init = 0.8 - 0.6 * math.exp(-0.3 * 0)

    tm = 512
    t_attn = 256
    tk_attn = 1024
    bm = 512

    x2 = x.reshape(n, d)
    half = D_HEAD // 2
    per_row = LANES // half
    inv_freq = jnp.exp(-math.log(ROPE_THETA) * jnp.arange(0, D_HEAD, 2, dtype=F32) / D_HEAD)
    inv128 = jnp.tile(inv_freq, per_row).reshape(1, LANES)
    pos_rep = jnp.repeat(positions.reshape(n // per_row, per_row).astype(I32), half, axis=1)
    cos_t, sin_t = _rope_tables(pos_rep, inv128, min(1024, n // per_row))
    cos_t = cos_t.reshape(n, half)
    sin_t = sin_t.reshape(n, half)

    perm = _swa_head_perm()
    sq0 = 3 * 512
    w_in_l = w_in[0]
    w_in_p = jnp.concatenate([w_in_l[:, :sq0], w_in_l[:, sq0:sq0 + 512][:, perm], w_in_l[:, sq0 + 512:]], axis=1)
    w_out_l = w_out[0]
    w_out_p = jnp.concatenate([w_out_l[:512], w_out_l[512:][perm]], axis=0)
    sinks_p = sinks[0].reshape(2, 4).reshape(-1)

    qkv = _inproj(x2, ln_mix_w[0].reshape(1, d), w_in_p.astype(BF16), cos_t, sin_t, tm)
    o_diff = _diff_attention(qkv, lambda_q1[0].reshape(1, -1), lambda_k1[0].reshape(1, -1),
                             lambda_q2[0].reshape(1, -1), lambda_k2[0].reshape(1, -1),
                             subln_w[0].reshape(-1, 1), b, s, t_attn, min(tk_attn, s), lambda_init)
    o_swa = _swa_attention(qkv, sinks_p.astype(F32), b, s, tm)
    kv = _memkv(mem.reshape(b * m_len, d), ln_mem_w[0].reshape(1, d), wkv_cross[0].astype(BF16), m_len)

    wr = jnp.concatenate([w_group[0].T, jnp.zeros((8 - N_GROUPS, d), F32), w_expert[0].T], axis=0)
    br = jnp.concatenate([b_group[0], jnp.zeros((8 - N_GROUPS,), F32), b_expert[0]]).reshape(-1, 1)
    h2, f, lpos, gates, tcnt, tbase = _post(
        x2, o_diff, o_swa, w_out_p.astype(BF16), ln_cross_w[0].reshape(1, d), wq_cross[0].astype(BF16), kv,
        wo_cross[0].astype(BF16), ln_moe_w[0].reshape(1, d), wr, br, b, s, tm, m_len)

    tcnt_i = tcnt[:, :, 0].astype(I32).reshape(-1)
    tbase_i = tbase[:, :, 0].astype(I32).reshape(-1)
    tot_i = tbase_i[-N_EXPERTS:] + tcnt_i[-N_EXPERTS:]
    rows_max = 2 * n + (n // tm) * N_EXPERTS * (SUBLANES - 1)
    nblk = (rows_max + N_EXPERTS * (bm - 1) + bm - 1) // bm
    xs, pstart, blk_e, nact = _dispatch(tcnt_i, tbase_i, tot_i, f, lpos, gates, tm, bm, nblk)
    ys = _experts(blk_e, nact, xs, w_gate[0], w_up[0], w_down[0], bm)
    out = _combine(tcnt_i, tbase_i, pstart, h2, lpos, ln_final_w.reshape(1, d), ys, tm)
    return out.reshape(b, s, d)
```

```python
import functools
import math

import jax
import jax.numpy as jnp
from jax import lax
from jax.experimental import pallas as pl
from jax.experimental.pallas import tpu as pltpu

F32 = jnp.float32
BF16 = jnp.bfloat16
I32 = jnp.int32

D_HEAD = 64
ROPE_THETA = 10000.0
H_DIFF = 4
N_Q_SWA = 8
N_KV_SWA = 2
WINDOW = 128
H_CROSS = 4
N_GROUPS = 4
E_PER_GROUP = 8
N_EXPERTS = N_GROUPS * E_PER_GROUP
EPS = 1e-6
LANES = 128
SUBLANES = 8
LONG_SEGMENT_ROWS = 128
IN_W = 2304
NEG = -1e30

VMEM_LIMIT = 56 * 1024 * 1024

NT_DIMS = (((1,), (1,)), ((), ()))


U32 = jnp.uint32


def _pack_pairs(x):
    h = x.shape[1] // 2
    hi = lax.bitcast_convert_type(x[:, :h], U32)
    lo = lax.bitcast_convert_type(x[:, h:], U32)
    return hi | (lo >> 16)


def _unpack_pairs(p):
    a = lax.bitcast_convert_type(p & U32(0xFFFF0000), F32)
    b = lax.bitcast_convert_type(p << 16, F32)
    return jnp.concatenate([a, b], axis=1).astype(BF16)


def _rms(x, w):
    ms = jnp.mean(x * x, axis=-1, keepdims=True)
    return x * lax.rsqrt(ms + EPS) * w


def _rope_table_kernel(pos_ref, inv_ref, cos_ref, sin_ref):
    ang = pos_ref[...].astype(F32) * inv_ref[...]
    cos_ref[...] = jnp.cos(ang)
    sin_ref[...] = jnp.sin(ang)


def _rope_tables(pos_rep, inv128, rows):
    n4 = pos_rep.shape[0]
    spec = pl.BlockSpec((rows, LANES), lambda i: (i, 0))
    return pl.pallas_call(
        _rope_table_kernel,
        grid=(n4 // rows,),
        in_specs=[spec, pl.BlockSpec((1, LANES), lambda i: (0, 0))],
        out_specs=[spec, spec],
        out_shape=[jax.ShapeDtypeStruct((n4, LANES), F32)] * 2,
        compiler_params=pltpu.CompilerParams(dimension_semantics=("arbitrary",)),
        name="ropetab",
    )(pos_rep, inv128)


def _inproj_kernel(x_ref, lnw_ref, w_ref, cos_ref, sin_ref, o_ref):
    a = _rms(x_ref[...], lnw_ref[...]).astype(BF16)
    half = D_HEAD // 2
    reps = LANES // half

    def expand(tab_ref):
        parts = [jnp.concatenate([tab_ref[:, j * half:(j + 1) * half]] * reps, axis=1) for j in range(reps)]
        return jnp.concatenate(parts, axis=0)

    cos = expand(cos_ref)
    sin = expand(sin_ref)
    lane = lax.broadcasted_iota(I32, (1, LANES), 1)
    first = (lane % D_HEAD) < (D_HEAD // 2)
    sin_signed = jnp.where(first, -sin, sin)
    n_chunks = IN_W // 256
    value_chunks = (4, 5)
    for c in [c for c in range(n_chunks) if c not in value_chunks] + list(value_chunks):
        p = jnp.dot(a, w_ref[:, c * 256:(c + 1) * 256], preferred_element_type=F32)
        for hh in range(2):
            g = c * 2 + hh
            xg = p[:, hh * LANES:(hh + 1) * LANES]
            is_v = (8 <= g < 12) or g == 17
            if not is_v:
                partner = jnp.where(first, pltpu.roll(xg, 96, 1), pltpu.roll(xg, 32, 1))
                xg = xg * cos + partner * sin_signed
                if g < 4:
                    xg = xg * (D_HEAD ** -0.5 * math.log2(math.e))
                elif 12 <= g < 16:
                    xg = xg * (D_HEAD ** -0.5)
            o_ref[:, g * LANES:(g + 1) * LANES] = xg.astype(BF16)


def _inproj(x2, ln_w, w_in_bf, cos, sin, tm):
    n, d = x2.shape
    dense_rows = tm // (LANES // (D_HEAD // 2))
    return pl.pallas_call(
        _inproj_kernel,
        grid=(n // tm,),
        in_specs=[
            pl.BlockSpec((tm, d), lambda i: (i, 0)),
            pl.BlockSpec((1, d), lambda i: (0, 0)),
            pl.BlockSpec((d, IN_W), lambda i: (0, 0)),
            pl.BlockSpec((dense_rows, LANES), lambda i: (i, 0)),
            pl.BlockSpec((dense_rows, LANES), lambda i: (i, 0)),
        ],
        out_specs=pl.BlockSpec((tm, IN_W), lambda i: (i, 0)),
        out_shape=jax.ShapeDtypeStruct((n, IN_W), BF16),
        compiler_params=pltpu.CompilerParams(
            dimension_semantics=("arbitrary",), vmem_limit_bytes=VMEM_LIMIT),
        name="inproj",
    )(x2, ln_w, w_in_bf, cos, sin)


def _diff_group_kernel(q_ref, k_ref, v_ref, lq1_ref, lk1_ref, lq2_ref, lk2_ref, sw_ref, o_ref,
                       vt_scr, q2_scr, acc_scr, m_scr, l_scr, s_scr, *, t, tk, lambda_init):
    s = k_ref.shape[0]
    per = tk // t
    for c in range(s // tk):
        for r in range(per):
            rows = slice(c * tk + r * t, c * tk + (r + 1) * t)
            vt_scr[c, :, r * t:(r + 1) * t] = v_ref[rows, :].astype(F32).T.astype(BF16)

    lane = lax.broadcasted_iota(I32, (1, LANES), 1)
    lam = (jnp.exp(jnp.sum(lq1_ref[...] * lk1_ref[...], axis=1, keepdims=True))
           - jnp.exp(jnp.sum(lq2_ref[...] * lk2_ref[...], axis=1, keepdims=True))
           + lambda_init)

    def accumulate(r, st_parts, vt):
        m_prev = m_scr[r]
        m_new = m_prev
        for st in st_parts:
            m_new = jnp.maximum(m_new, jnp.max(st, axis=0, keepdims=True))
        ps = [jnp.exp2(st - m_new) for st in st_parts]
        alpha = jnp.exp2(m_prev - m_new)
        l_new = alpha * l_scr[r]
        for p in ps:
            l_new = l_new + jnp.sum(p, axis=0, keepdims=True)
        l_scr[r] = l_new
        pb = [p.astype(BF16) for p in ps]
        pb = pb[0] if len(pb) == 1 else jnp.concatenate(pb, axis=0)
        acc_scr[r] = alpha * acc_scr[r] + jnp.dot(vt, pb, preferred_element_type=F32)
        m_scr[r] = m_new

    def interleaved(score_fn, softmax_fn, next_first_scores):
        for r in range(per):
            if r + 1 < per:
                score_fn(r + 1)
            else:
                next_first_scores()
            softmax_fn(r)

    def group(gi, carry):
        row0 = pl.multiple_of(gi * tk, tk)
        for r in range(per):
            q = q_ref[pl.ds(row0 + r * t, t), :]
            zero = jnp.zeros_like(q)
            q2_scr[r] = jnp.concatenate(
                [jnp.where(lane < D_HEAD, q, zero), jnp.where(lane >= D_HEAD, q, zero)], axis=0)
        acc_scr[...] = jnp.zeros(acc_scr.shape, F32)
        m_scr[...] = jnp.full(m_scr.shape, NEG, F32)
        l_scr[...] = jnp.zeros(l_scr.shape, F32)

        def diag_scores(r):
            rows = (r + 1) * t
            k = k_ref[pl.ds(row0, rows), :]
            s_scr[r, 0:rows, :] = lax.dot_general(k, q2_scr[r], NT_DIMS, preferred_element_type=F32)

        def diag_softmax(r):
            rows = (r + 1) * t
            key = lax.broadcasted_iota(I32, (t, 2 * t), 0)
            col = lax.broadcasted_iota(I32, (t, 2 * t), 1)
            diag = jnp.where(key <= jnp.where(col >= t, col - t, col), s_scr[r, r * t:rows, :], NEG)
            parts = [diag] if r == 0 else [s_scr[r, 0:r * t, :], diag]
            accumulate(r, parts, vt_scr[gi, :, 0:rows])

        def full_scores(j, r):
            k = k_ref[pl.ds(pl.multiple_of(j * tk, tk), tk), :]
            s_scr[r] = lax.dot_general(k, q2_scr[r], NT_DIMS, preferred_element_type=F32)

        diag_scores(0)
        interleaved(diag_scores, diag_softmax, functools.partial(full_scores, 0, 0))

        def key_block(j, c2):
            interleaved(functools.partial(full_scores, j), lambda r: accumulate(r, [s_scr[r]], vt_scr[j]),
                        functools.partial(full_scores, j + 1, 0))
            return c2

        lax.fori_loop(0, gi, key_block, 0)

        for r in range(per):
            on = acc_scr[r] / l_scr[r]
            o = on[:, :t] - lam * on[:, t:]
            ms = jnp.mean(o * o, axis=0, keepdims=True)
            o = o * lax.rsqrt(ms + EPS) * sw_ref[...] * (1.0 - lambda_init)
            o_ref[pl.ds(row0 + r * t, t), :] = o.T.astype(BF16)
        return carry

    lax.fori_loop(0, s // tk, group, 0)


def _diff_attention(qkv, lq1, lk1, lq2, lk2, subln_col, b, s, t, tk, lambda_init):
    n = qkv.shape[0]
    per = tk // t
    small = pl.BlockSpec((1, D_HEAD), lambda bi, h: (0, 0))
    return pl.pallas_call(
        functools.partial(_diff_group_kernel, t=t, tk=tk, lambda_init=lambda_init),
        grid=(b, H_DIFF),
        in_specs=[
            pl.BlockSpec((s, LANES), lambda bi, h: (bi, h)),
            pl.BlockSpec((s, LANES), lambda bi, h: (bi, 4 + h)),
            pl.BlockSpec((s, LANES), lambda bi, h: (bi, 8 + h)),
            small, small, small, small,
            pl.BlockSpec((LANES, 1), lambda bi, h: (0, 0)),
        ],
        out_specs=pl.BlockSpec((s, LANES), lambda bi, h: (bi, h)),
        out_shape=jax.ShapeDtypeStruct((n, H_DIFF * LANES), BF16),
        scratch_shapes=[
            pltpu.VMEM((s // tk, LANES, tk), BF16),
            pltpu.VMEM((per, 2 * t, LANES), BF16),
            pltpu.VMEM((per, LANES, 2 * t), F32),
            pltpu.VMEM((per, 1, 2 * t), F32),
            pltpu.VMEM((per, 1, 2 * t), F32),
            pltpu.VMEM((per, tk, 2 * t), F32),
        ],
        compiler_params=pltpu.CompilerParams(
            dimension_semantics=("arbitrary", "arbitrary"), vmem_limit_bytes=VMEM_LIMIT),
        name="diffattn",
    )(qkv, qkv, qkv, lq1, lk1, lq2, lk2, subln_col)


def _swa_kernel(sink_ref, q_ref, kc_ref, kp_ref, vc_ref, vp_ref, o_ref, kbuf, vbuf, *, tq):
    i = pl.program_id(1)
    w = WINDOW
    kbuf[0:w, :] = kp_ref[...]
    kbuf[w:w + tq, :] = kc_ref[...]
    vbuf[0:w, :] = vp_ref[...]
    vbuf[w:w + tq, :] = vc_ref[...]
    lane = lax.broadcasted_iota(I32, (1, LANES), 1)
    lo = lane < D_HEAD
    qi = lax.broadcasted_iota(I32, (w, 2 * w), 0)
    ki = lax.broadcasted_iota(I32, (w, 2 * w), 1)
    band = (ki > qi) & (ki <= qi + w)
    band_first = band & (ki >= jnp.where(i > 0, 0, w))
    def scores(r):
        keys = kbuf[r * w:(r + 2) * w, :]
        qs = []
        for half in range(2):
            for p in range(4):
                qg = q_ref[r * w:(r + 1) * w, p * LANES:(p + 1) * LANES]
                qs.append(jnp.where(lo if half == 0 else ~lo, qg, jnp.zeros_like(qg)))
        qstack = jnp.concatenate(qs, axis=0)
        return lax.dot_general(qstack, keys, NT_DIMS, preferred_element_type=F32)

    s_next = scores(0)
    for r in range(tq // w):
        s_all = s_next
        if r + 1 < tq // w:
            s_next = scores(r + 1)
        vals = vbuf[r * w:(r + 2) * w, :]
        valid = band_first if r == 0 else band
        ps = []
        for hb in range(8):
            sink = sink_ref[hb]
            s = jnp.where(valid, s_all[hb * w:(hb + 1) * w], NEG)
            m = jnp.maximum(jnp.max(s, axis=1, keepdims=True), sink)
            e = jnp.exp(s - m)
            den = jnp.sum(e, axis=1, keepdims=True) + jnp.exp(sink - m)
            ps.append((e / den).astype(BF16))
        pv = jnp.dot(jnp.concatenate(ps, axis=0), vals, preferred_element_type=F32)
        for p in range(4):
            og = jnp.where(lo, pv[p * w:(p + 1) * w], pv[(4 + p) * w:(5 + p) * w])
            o_ref[r * w:(r + 1) * w, p * LANES:(p + 1) * LANES] = og.astype(BF16)


def _swa_attention(qkv, sinks, b, s, tq):
    n = qkv.shape[0]
    nq = s // tq
    per = tq // WINDOW
    prev_map = lambda col: (lambda bi, i, sk: (jnp.maximum(bi * (s // WINDOW) + i * per - 1, 0), col))
    cur_map = lambda col: (lambda bi, i, sk: (bi * nq + i, col))
    return pl.pallas_call(
        functools.partial(_swa_kernel, tq=tq),
        grid_spec=pltpu.PrefetchScalarGridSpec(
            num_scalar_prefetch=1,
            grid=(b, nq),
            in_specs=[
                pl.BlockSpec((tq, 4 * LANES), lambda bi, i, sk: (bi * nq + i, 3)),
                pl.BlockSpec((tq, LANES), cur_map(16)),
                pl.BlockSpec((WINDOW, LANES), prev_map(16)),
                pl.BlockSpec((tq, LANES), cur_map(17)),
                pl.BlockSpec((WINDOW, LANES), prev_map(17)),
            ],
            out_specs=pl.BlockSpec((tq, 4 * LANES), lambda bi, i, sk: (bi * nq + i, 0)),
            scratch_shapes=[pltpu.VMEM((WINDOW + tq, LANES), BF16)] * 2,
        ),
        out_shape=jax.ShapeDtypeStruct((n, 4 * LANES), BF16),
        compiler_params=pltpu.CompilerParams(
            dimension_semantics=("arbitrary", "arbitrary"), vmem_limit_bytes=VMEM_LIMIT),
        name="swa",
    )(sinks, qkv, qkv, qkv, qkv, qkv)


def _memkv_kernel(m_ref, lnw_ref, w_ref, o_ref):
    a = _rms(m_ref[...], lnw_ref[...]).astype(BF16)
    o_ref[...] = jnp.dot(a, w_ref[...], preferred_element_type=F32).astype(BF16)


def _memkv(mem2, ln_w, wkv_bf, m_len):
    n, d = mem2.shape
    return pl.pallas_call(
        _memkv_kernel,
        grid=(n // m_len,),
        in_specs=[
            pl.BlockSpec((m_len, d), lambda i: (i, 0)),
            pl.BlockSpec((1, d), lambda i: (0, 0)),
            pl.BlockSpec((d, 2 * d), lambda i: (0, 0)),
        ],
        out_specs=pl.BlockSpec((m_len, 2 * d), lambda i: (i, 0)),
        out_shape=jax.ShapeDtypeStruct((n, 2 * d), BF16),
        compiler_params=pltpu.CompilerParams(
            dimension_semantics=("arbitrary",), vmem_limit_bytes=VMEM_LIMIT),
        name="memkv",
    )(mem2, ln_w, wkv_bf)


def _split_bf16(x):
    hi = x.astype(BF16)
    lo = (x - hi.astype(F32)).astype(BF16)
    return hi, lo


def _post_kernel(x_ref, od_ref, os_ref, wout_ref, lnc_ref, wq_ref, kv_ref, wo_ref, lnm_ref, wr_ref, br_ref,
                 h_ref, f_ref, lpos_ref, gate_ref, tcnt_ref, tbase_ref, base_scr, fprev_scr, *, tm, d):
    g = pl.program_id(0)

    @pl.when(g == 0)
    def _():
        base_scr[...] = jnp.zeros(base_scr.shape, F32)
        fprev_scr[...] = jnp.zeros(fprev_scr.shape, F32)

    mix = jnp.concatenate([od_ref[...], os_ref[...]], axis=1)
    h1 = x_ref[...] + jnp.dot(mix, wout_ref[...], preferred_element_type=F32)
    logits = _route_logits(fprev_scr[...], wr_ref, br_ref)

    c = _rms(h1, lnc_ref[...]).astype(BF16)
    dc = d // H_CROSS
    q = (jnp.dot(c, wq_ref[...], preferred_element_type=F32) * (dc ** -0.5)).astype(BF16)
    eid1, eid2 = _route_select(logits, gate_ref, tm)
    outs = []
    for hd in range(H_CROSS):
        k = kv_ref[:, hd * dc:(hd + 1) * dc]
        v = kv_ref[:, d + hd * dc:d + (hd + 1) * dc]
        s = lax.dot_general(q[:, hd * dc:(hd + 1) * dc], k, NT_DIMS, preferred_element_type=F32)
        e = jnp.exp(s - jnp.max(s, axis=1, keepdims=True))
        den = jnp.sum(e, axis=1, keepdims=True)
        outs.append((jnp.dot(e.astype(BF16), v, preferred_element_type=F32) / den).astype(BF16))
    o = jnp.concatenate(outs, axis=1)
    h2 = h1 + jnp.dot(o, wo_ref[...], preferred_element_type=F32)
    _route_positions(eid1, eid2, jnp.where(g > 0, 1.0, 0.0), lpos_ref, tcnt_ref, tbase_ref, base_scr, tm)
    h_ref[...] = h2
    f = _rms(h2, lnm_ref[...])
    f_ref[...] = f.astype(BF16)
    fprev_scr[...] = f


def _route_logits(f, wr_ref, br_ref):
    f_hi, f_lo = _split_bf16(f)
    w_hi, w_lo = _split_bf16(wr_ref[...])
    return (lax.dot_general(w_hi, f_hi, NT_DIMS, preferred_element_type=F32)
            + lax.dot_general(w_hi, f_lo, NT_DIMS, preferred_element_type=F32)
            + lax.dot_general(w_lo, f_hi, NT_DIMS, preferred_element_type=F32)) + br_ref[...]


def _route_select(lg, gate_ref, tm):
    gl = lg[0:N_GROUPS]
    gmax = jnp.max(gl, axis=0, keepdims=True)
    gidx = lax.broadcasted_iota(I32, gl.shape, 0)
    g_sel = jnp.min(jnp.where(gl == gmax, gidx, N_GROUPS), axis=0, keepdims=True)
    g_p = 1.0 / jnp.sum(jnp.exp(gl - gmax), axis=0, keepdims=True)

    e8 = jnp.zeros((E_PER_GROUP, tm), F32)
    for g in range(N_GROUPS):
        e8 = e8 + jnp.where(g_sel == g, lg[8 + g * E_PER_GROUP:8 + (g + 1) * E_PER_GROUP], 0.0)
    ex = jnp.exp(e8 - jnp.max(e8, axis=0, keepdims=True))
    ep = ex / jnp.sum(ex, axis=0, keepdims=True)
    idx8 = lax.broadcasted_iota(I32, ep.shape, 0)
    p1 = jnp.max(ep, axis=0, keepdims=True)
    i1 = jnp.min(jnp.where(ep == p1, idx8, E_PER_GROUP), axis=0, keepdims=True)
    ep2 = jnp.where(idx8 == i1, -1.0, ep)
    p2 = jnp.max(ep2, axis=0, keepdims=True)
    i2 = jnp.min(jnp.where(ep2 == p2, idx8, E_PER_GROUP), axis=0, keepdims=True)
    psum = p1 + p2
    gate_ref[0:1, :] = g_p * (p1 / psum)
    gate_ref[1:2, :] = g_p * (p2 / psum)
    return g_sel * E_PER_GROUP + i1, g_sel * E_PER_GROUP + i2


def _route_positions(eid1, eid2, live, lpos_ref, tcnt_ref, tbase_ref, base_scr, tm):
    e32 = lax.broadcasted_iota(I32, (N_EXPERTS, tm), 0)
    oh1 = (e32 == eid1).astype(F32)
    oh2 = (e32 == eid2).astype(F32)
    cnt = oh1 + oh2
    tr = lax.broadcasted_iota(I32, (tm, tm), 0)
    tc = lax.broadcasted_iota(I32, (tm, tm), 1)
    upper = jnp.where(tr < tc, 1.0, 0.0).astype(BF16)
    earlier = jnp.dot(cnt.astype(BF16), upper, preferred_element_type=F32)
    seg = jnp.sum(cnt, axis=1, keepdims=True)
    seg = jnp.floor((seg + (SUBLANES - 1)) * (1.0 / SUBLANES)) * SUBLANES * live
    seg = jnp.broadcast_to(seg, (N_EXPERTS, LANES))
    er = lax.broadcasted_iota(I32, (N_EXPERTS, N_EXPERTS), 0)
    ec = lax.broadcasted_iota(I32, (N_EXPERTS, N_EXPERTS), 1)
    lower = jnp.where(ec < er, 1.0, 0.0).astype(BF16)
    start = jnp.dot(lower, seg.astype(BF16), preferred_element_type=F32)[:, 0:1]
    where = earlier + start
    lpos_ref[0:1, :] = jnp.sum(oh1 * where, axis=0, keepdims=True).astype(I32)
    lpos_ref[1:2, :] = jnp.sum(oh2 * where, axis=0, keepdims=True).astype(I32)
    tcnt_ref[...] = seg
    tbase_ref[...] = base_scr[...]
    base_scr[...] = base_scr[...] + seg


def _post(x2, od, osw, wout_bf, lnc, wq_bf, kv, wo_bf, lnm, wr, br, b, s, tm, m_len):
    n, d = x2.shape
    nt = s // tm
    tiles = n // tm
    main = lambda g: jnp.minimum(g, tiles - 1)
    routed = lambda g: jnp.maximum(g - 1, 0)
    row = lambda g: (main(g), 0)
    const = lambda g: (0, 0)
    return pl.pallas_call(
        functools.partial(_post_kernel, tm=tm, d=d),
        grid=(tiles + 1,),
        in_specs=[
            pl.BlockSpec((tm, d), row),
            pl.BlockSpec((tm, d // 2), row),
            pl.BlockSpec((tm, d // 2), row),
            pl.BlockSpec((d, d), const),
            pl.BlockSpec((1, d), const),
            pl.BlockSpec((d, d), const),
            pl.BlockSpec((m_len, 2 * d), lambda g: (main(g) // nt, 0)),
            pl.BlockSpec((d, d), const),
            pl.BlockSpec((1, d), const),
            pl.BlockSpec((8 + N_EXPERTS, d), const),
            pl.BlockSpec((8 + N_EXPERTS, 1), const),
        ],
        out_specs=[
            pl.BlockSpec((tm, d), row),
            pl.BlockSpec((tm, d), row),
            pl.BlockSpec((2, tm), lambda g: (0, routed(g))),
            pl.BlockSpec((2, tm), lambda g: (0, routed(g))),
            pl.BlockSpec((None, N_EXPERTS, LANES), lambda g: (routed(g), 0, 0)),
            pl.BlockSpec((None, N_EXPERTS, LANES), lambda g: (routed(g), 0, 0)),
        ],
        out_shape=[
            jax.ShapeDtypeStruct((n, d), F32),
            jax.ShapeDtypeStruct((n, d), BF16),
            jax.ShapeDtypeStruct((2, n), I32),
            jax.ShapeDtypeStruct((2, n), F32),
            jax.ShapeDtypeStruct((n // tm, N_EXPERTS, LANES), F32),
            jax.ShapeDtypeStruct((n // tm, N_EXPERTS, LANES), F32),
        ],
        scratch_shapes=[pltpu.VMEM((N_EXPERTS, LANES), F32), pltpu.VMEM((tm, d), F32)],
        compiler_params=pltpu.CompilerParams(
            dimension_semantics=("arbitrary",), vmem_limit_bytes=VMEM_LIMIT),
        name="post",
    )(x2, od, osw, wout_bf, lnc, wq_bf, kv, wo_bf, lnm, wr, br)


def _local_rows(tm):
    return 2 * tm + SUBLANES * N_EXPERTS


def _segment_copies(tile, tcnt_ref, tbase_ref, pstart_ref, make, act, max_len):
    sizes = [SUBLANES << k for k in reversed(range((max_len // SUBLANES).bit_length()))]

    if act == "wait":
        total = lax.fori_loop(0, N_EXPERTS, lambda e, acc: acc + tcnt_ref[tile * N_EXPERTS + e], 0)
        for size in sizes:
            @pl.when((total & size) != 0)
            def _():
                make(0, 0, size).wait()

        return total

    def pieces(ln, local0, glob0, some_sizes):
        for size in some_sizes:
            @pl.when((ln & size) != 0)
            def _():
                off = (ln // (2 * size)) * (2 * size)
                make(pl.multiple_of(local0 + off, SUBLANES), pl.multiple_of(glob0 + off, SUBLANES), size).start()

    big = [s for s in sizes if s >= LONG_SEGMENT_ROWS]
    small = [s for s in sizes if s < LONG_SEGMENT_ROWS]

    def per_expert(e, local0):
        ln = tcnt_ref[tile * N_EXPERTS + e]
        glob0 = pstart_ref[e] + tbase_ref[tile * N_EXPERTS + e]

        @pl.when(ln >= LONG_SEGMENT_ROWS)
        def _():
            pieces(ln, local0, glob0, big)

        pieces(ln, local0, glob0, small)
        return local0 + ln

    return lax.fori_loop(0, N_EXPERTS, per_expert, 0)


def _dispatch_kernel(tcnt_ref, tbase_ref, tot_ref, f_ref, lpos_ref, gate_ref, xs_hbm, pstart_ref, blk_ref, nact_ref,
                     loc_scr, zero_scr, sem, zsem, *, tm, bm, nblk):
    i = pl.program_id(0)
    nt = pl.num_programs(0)
    slot = i % 2

    @pl.when(i == 0)
    def _():
        def per_expert(e, blk0):
            nb = (tot_ref[e] + (bm - 1)) // bm
            pstart_ref[e] = blk0 * bm

            def fill(j, carry):
                blk_ref[j] = e
                return carry

            lax.fori_loop(blk0, blk0 + nb, fill, 0)
            return blk0 + nb

        nact = lax.fori_loop(0, N_EXPERTS, per_expert, 0)
        nact_ref[0] = nact

        def tail(j, carry):
            blk_ref[j] = N_EXPERTS - 1
            return carry

        lax.fori_loop(nact, nblk, tail, 0)

        zero_scr[...] = jnp.zeros(zero_scr.shape, U32)

        def zero_fill(act):
            def do(copy):
                if act == "start":
                    copy.start()
                else:
                    copy.wait()

            def per_expert_pad(e, carry):
                tot = tot_ref[e]
                pad = (tot + (bm - 1)) // bm * bm - tot
                row0 = pstart_ref[e] + tot
                for k in reversed(range((bm // SUBLANES).bit_length())):
                    size = SUBLANES << k

                    @pl.when((pad & size) != 0)
                    def _():
                        row = pl.multiple_of(row0 + (pad // (2 * size)) * (2 * size), SUBLANES)
                        do(pltpu.make_async_copy(zero_scr.at[pl.ds(0, size)], xs_hbm.at[pl.ds(row, size)], zsem))

                return carry

            lax.fori_loop(0, N_EXPERTS, per_expert_pad, 0)

            def per_unused_block(j, carry):
                do(pltpu.make_async_copy(zero_scr, xs_hbm.at[pl.ds(pl.multiple_of(j * bm, bm), bm)], zsem))
                return carry

            lax.fori_loop(nact, nblk, per_unused_block, 0)

        zero_fill("start")
        zero_fill("wait")

    def copies(tile, sl, act):
        def make(lrow, grow, size):
            return pltpu.make_async_copy(loc_scr.at[sl, pl.ds(lrow, size)], xs_hbm.at[pl.ds(grow, size)], sem.at[sl])

        _segment_copies(tile, tcnt_ref, tbase_ref, pstart_ref, make, act, _local_rows(tm))

    @pl.when(i >= 2)
    def _():
        copies(i - 2, slot, "wait")

    pos = lax.broadcasted_iota(I32, (_local_rows(tm), tm), 0)
    first = pos == lpos_ref[0:1, :]
    second = pos == lpos_ref[1:2, :]
    onehot = jnp.where(first, 1.0, jnp.where(second, 1.0, 0.0)).astype(BF16)
    half = f_ref.shape[1] // 2
    loc_scr[slot, :, 0:half] = _pack_pairs(jnp.dot(onehot, f_ref[...], preferred_element_type=F32))
    gate = jnp.sum(jnp.where(first, gate_ref[0:1, :], jnp.where(second, gate_ref[1:2, :], 0.0)),
                   axis=1, keepdims=True)
    gate = jnp.broadcast_to(gate, (_local_rows(tm), LANES))
    loc_scr[slot, :, half:half + LANES] = lax.bitcast_convert_type(gate, U32)
    copies(i, slot, "start")

    @pl.when(i == nt - 1)
    def _():
        @pl.when(i >= 1)
        def _():
            copies(i - 1, 1 - slot, "wait")

        copies(i, slot, "wait")


def _dispatch(tcnt_i, tbase_i, tot_i, f, lpos, gates, tm, bm, nblk):
    n, d = f.shape
    width = d // 2 + LANES
    smem = pl.BlockSpec(memory_space=pltpu.SMEM)
    hbm = pl.BlockSpec(memory_space=pl.ANY)
    return pl.pallas_call(
        functools.partial(_dispatch_kernel, tm=tm, bm=bm, nblk=nblk),
        grid_spec=pltpu.PrefetchScalarGridSpec(
            num_scalar_prefetch=3,
            grid=(n // tm,),
            in_specs=[
                pl.BlockSpec((tm, d), lambda i, *_: (i, 0)),
                pl.BlockSpec((2, tm), lambda i, *_: (0, i)),
                pl.BlockSpec((2, tm), lambda i, *_: (0, i)),
            ],
            out_specs=[hbm, smem, smem, smem],
            scratch_shapes=[
                pltpu.VMEM((2, _local_rows(tm), width), U32),
                pltpu.VMEM((bm, width), U32),
                pltpu.SemaphoreType.DMA((2,)),
                pltpu.SemaphoreType.DMA,
            ],
        ),
        out_shape=[
            jax.ShapeDtypeStruct((nblk * bm, width), U32),
            jax.ShapeDtypeStruct((N_EXPERTS,), I32),
            jax.ShapeDtypeStruct((nblk,), I32),
            jax.ShapeDtypeStruct((1,), I32),
        ],
        compiler_params=pltpu.CompilerParams(
            dimension_semantics=("arbitrary",), vmem_limit_bytes=VMEM_LIMIT),
        name="dispatch",
    )(tcnt_i, tbase_i, tot_i, f, lpos, gates)


def _expert_kernel(blk_ref, nact_ref, x_ref, wg_ref, wu_ref, wd_ref, y_ref, wg_bf, wu_bf, wd_bf):
    j = pl.program_id(0)
    active = j < nact_ref[0]

    @pl.when(active & ((j == 0) | (blk_ref[j] != blk_ref[jnp.maximum(j - 1, 0)])))
    def _():
        wg_bf[...] = wg_ref[...].astype(BF16)
        wu_bf[...] = wu_ref[...].astype(BF16)
        wd_bf[...] = wd_ref[...].astype(BF16)

    @pl.when(active)
    def _():
        half = y_ref.shape[1]
        x = _unpack_pairs(x_ref[:, 0:half])
        gate = lax.bitcast_convert_type(x_ref[:, half:half + 1], F32)
        g = jnp.dot(x, wg_bf[...], preferred_element_type=F32)
        u = jnp.dot(x, wu_bf[...], preferred_element_type=F32)
        hdn = (g * jax.nn.sigmoid(g) * u).astype(BF16)
        y = jnp.dot(hdn, wd_bf[...], preferred_element_type=F32) * gate
        y_ref[...] = _pack_pairs(y.astype(BF16).astype(F32))

    @pl.when(j >= nact_ref[0])
    def _():
        y_ref[...] = jnp.zeros(y_ref.shape, y_ref.dtype)


def _experts(blk_e, nact, xs, wg, wu, wd, bm):
    p, width = xs.shape
    d, dff = wg.shape[-2:]
    dh = d // 2
    nblk = p // bm
    rowmap = lambda j, blk, na: (jnp.minimum(j, na[0] - 1), 0)
    wmap = lambda j, blk, na: (blk[jnp.minimum(j, na[0] - 1)], 0, 0)
    return pl.pallas_call(
        _expert_kernel,
        grid_spec=pltpu.PrefetchScalarGridSpec(
            num_scalar_prefetch=2,
            grid=(nblk,),
            in_specs=[
                pl.BlockSpec((bm, width), rowmap),
                pl.BlockSpec((None, d, dff), wmap),
                pl.BlockSpec((None, d, dff), wmap),
                pl.BlockSpec((None, dff, d), wmap),
            ],
            out_specs=pl.BlockSpec((bm, dh), lambda j, blk, na: (j, 0)),
            scratch_shapes=[pltpu.VMEM((d, dff), BF16), pltpu.VMEM((d, dff), BF16), pltpu.VMEM((dff, d), BF16)],
        ),
        out_shape=jax.ShapeDtypeStruct((p, dh), U32),
        compiler_params=pltpu.CompilerParams(
            dimension_semantics=("arbitrary",), vmem_limit_bytes=VMEM_LIMIT),
        name="experts",
    )(blk_e, nact, xs, wg, wu, wd)


def _combine_kernel(tcnt_ref, tbase_ref, pstart_ref, h_ref, lpos_ref, lnf_ref, ys_hbm, o_ref,
                    loc_scr, sem, *, tm):
    i = pl.program_id(0)
    nt = pl.num_programs(0)
    slot = i % 2

    def copies(tile, sl, act):
        def make(lrow, grow, size):
            return pltpu.make_async_copy(ys_hbm.at[pl.ds(grow, size)], loc_scr.at[sl, pl.ds(lrow, size)], sem.at[sl])

        return _segment_copies(tile, tcnt_ref, tbase_ref, pstart_ref, make, act, _local_rows(tm))

    @pl.when(i == 0)
    def _():
        copies(0, 0, "start")

    @pl.when(i + 1 < nt)
    def _():
        copies(i + 1, 1 - slot, "start")

    used = copies(i, slot, "wait")
    row = lax.broadcasted_iota(I32, (_local_rows(tm), 1), 0)
    ys = _unpack_pairs(jnp.where(row < used, loc_scr[slot], U32(0)))

    r8 = lax.broadcasted_iota(I32, (8, tm), 0)
    lp = lpos_ref[...].astype(F32)
    top = jnp.where(r8 == 0, lp[0:1], jnp.where(r8 == 1, lp[1:2], 0.0))
    cols = jnp.concatenate([top, jnp.zeros((LANES - 8, tm), F32)], axis=0).T.astype(I32)
    pos = lax.broadcasted_iota(I32, (tm, _local_rows(tm)), 1)
    sel = jnp.where(pos == cols[:, 0:1], 1.0, jnp.where(pos == cols[:, 1:2], 1.0, 0.0)).astype(BF16)
    o_ref[...] = _rms(h_ref[...] + jnp.dot(sel, ys, preferred_element_type=F32), lnf_ref[...])


def _combine(tcnt_i, tbase_i, pstart, h2, lpos, lnf, ys, tm):
    n, d = h2.shape
    return pl.pallas_call(
        functools.partial(_combine_kernel, tm=tm),
        grid_spec=pltpu.PrefetchScalarGridSpec(
            num_scalar_prefetch=3,
            grid=(n // tm,),
            in_specs=[
                pl.BlockSpec((tm, d), lambda i, *_: (i, 0)),
                pl.BlockSpec((2, tm), lambda i, *_: (0, i)),
                pl.BlockSpec((1, d), lambda i, *_: (0, 0)),
                pl.BlockSpec(memory_space=pl.ANY),
            ],
            out_specs=pl.BlockSpec((tm, d), lambda i, *_: (i, 0)),
            scratch_shapes=[pltpu.VMEM((2, _local_rows(tm), d // 2), U32), pltpu.SemaphoreType.DMA((2,))],
        ),
        out_shape=jax.ShapeDtypeStruct((n, d), F32),
        compiler_params=pltpu.CompilerParams(
            dimension_semantics=("arbitrary",), vmem_limit_bytes=VMEM_LIMIT),
        name="combine",
    )(tcnt_i, tbase_i, pstart, h2, lpos, lnf, ys)


def _swa_head_perm():
    cols = []
    for p in range(4):
        for half in range(2):
            head = half * 4 + p
            cols.extend(range(head * D_HEAD, (head + 1) * D_HEAD))
    return jnp.asarray(cols, dtype=I32)


def kernel(x, mem, positions, ln_mix_w, w_in, lambda_q1, lambda_k1, lambda_q2, lambda_k2, subln_w, sinks,
           w_out, ln_cross_w, ln_mem_w, wq_cross, wkv_cross, wo_cross, ln_moe_w, w_group, b_group,
           w_expert, b_expert, w_gate, w_up, w_down, ln_final_w):
    b, s, d = x.shape
    m_len = mem.shape[1]
    n = b * s
    assert w_in.shape[0] == 1 and d == 1024 and n <= 65536
    lambda_init = 0.8 - 0.6 * math.exp(-0.3 * 0)

    tm = 512
    t_attn = 256
    tk_attn = 1024
    bm = 512

    x2 = x.reshape(n, d)
    half = D_HEAD // 2
    per_row = LANES // half
    inv_freq = jnp.exp(-math.log(ROPE_THETA) * jnp.arange(0, D_HEAD, 2, dtype=F32) / D_HEAD)
    inv128 = jnp.tile(inv_freq, per_row).reshape(1, LANES)
    pos_tiles = positions.reshape(n // tm, per_row, tm // per_row).astype(I32).transpose(0, 2, 1)
    pos_rep = jnp.repeat(pos_tiles.reshape(n // per_row, per_row), half, axis=1)
    cos_t, sin_t = _rope_tables(pos_rep, inv128, min(1024, n // per_row))

    perm = _swa_head_perm()
    sq0 = 3 * 512
    w_in_l = w_in[0]
    w_in_p = jnp.concatenate([w_in_l[:, :sq0], w_in_l[:, sq0:sq0 + 512][:, perm], w_in_l[:, sq0 + 512:]], axis=1)
    w_out_l = w_out[0]
    w_out_p = jnp.concatenate([w_out_l[:512], w_out_l[512:][perm]], axis=0)
    sinks_p = sinks[0].reshape(2, 4).reshape(-1)

    qkv = _inproj(x2, ln_mix_w[0].reshape(1, d), w_in_p.astype(BF16), cos_t, sin_t, tm)
    o_diff = _diff_attention(qkv, lambda_q1[0].reshape(1, -1), lambda_k1[0].reshape(1, -1),
                             lambda_q2[0].reshape(1, -1), lambda_k2[0].reshape(1, -1),
                             subln_w[0].reshape(-1, 1), b, s, t_attn, min(tk_attn, s), lambda_init)
    o_swa = _swa_attention(qkv, sinks_p.astype(F32), b, s, tm)
    kv = _memkv(mem.reshape(b * m_len, d), ln_mem_w[0].reshape(1, d), wkv_cross[0].astype(BF16), m_len)

    wr = jnp.concatenate([w_group[0].T, jnp.zeros((8 - N_GROUPS, d), F32), w_expert[0].T], axis=0)
    br = jnp.concatenate([b_group[0], jnp.zeros((8 - N_GROUPS,), F32), b_expert[0]]).reshape(-1, 1)
    h2, f, lpos, gates, tcnt, tbase = _post(
        x2, o_diff, o_swa, w_out_p.astype(BF16), ln_cross_w[0].reshape(1, d), wq_cross[0].astype(BF16), kv,
        wo_cross[0].astype(BF16), ln_moe_w[0].reshape(1, d), wr, br, b, s, tm, m_len)

    tcnt_i = tcnt[:, :, 0].astype(I32).reshape(-1)
    tbase_i = tbase[:, :, 0].astype(I32).reshape(-1)
    tot_i = tbase_i[-N_EXPERTS:] + tcnt_i[-N_EXPERTS:]
    rows_max = 2 * n + (n // tm) * N_EXPERTS * (SUBLANES - 1)
    nblk = (rows_max + N_EXPERTS * (bm - 1) + bm - 1) // bm
    xs, pstart, blk_e, nact = _dispatch(tcnt_i, tbase_i, tot_i, f, lpos, gates, tm, bm, nblk)
    ys = _experts(blk_e, nact, xs, w_gate[0], w_up[0], w_down[0], bm)
    out = _combine(tcnt_i, tbase_i, pstart, h2, lpos, ln_final_w.reshape(1, d), ys, tm)
    return out.reshape(b, s, d)
```

```python
import functools
import math

import jax
import jax.numpy as jnp
from jax import lax
from jax.experimental import pallas as pl
from jax.experimental.pallas import tpu as pltpu

F32 = jnp.float32
BF16 = jnp.bfloat16
I32 = jnp.int32

D_HEAD = 64
ROPE_THETA = 10000.0
H_DIFF = 4
N_Q_SWA = 8
N_KV_SWA = 2
WINDOW = 128
H_CROSS = 4
N_GROUPS = 4
E_PER_GROUP = 8
N_EXPERTS = N_GROUPS * E_PER_GROUP
EPS = 1e-6
LANES = 128
SUBLANES = 8
LONG_SEGMENT_ROWS = 128
IN_W = 2304
NEG = -1e30

VMEM_LIMIT = 56 * 1024 * 1024

TOKEN_TILE = 512
ATTN_QUERY_TILE = 256
ATTN_KEY_BLOCK = 1024
EXPERT_BLOCK_ROWS = 512

NT_DIMS = (((1,), (1,)), ((), ()))


U32 = jnp.uint32


def _pack_pairs(x):
    h = x.shape[1] // 2
    hi = lax.bitcast_convert_type(x[:, :h], U32)
    lo = lax.bitcast_convert_type(x[:, h:], U32)
    return hi | (lo >> 16)


def _unpack_pairs(p):
    a = lax.bitcast_convert_type(p & U32(0xFFFF0000), F32)
    b = lax.bitcast_convert_type(p << 16, F32)
    return jnp.concatenate([a, b], axis=1).astype(BF16)


def _rms(x, w):
    ms = jnp.mean(x * x, axis=-1, keepdims=True)
    return x * lax.rsqrt(ms + EPS) * w


def _rope_table_kernel(pos_ref, inv_ref, cos_ref, sin_ref):
    ang = pos_ref[...].astype(F32) * inv_ref[...]
    cos_ref[...] = jnp.cos(ang)
    sin_ref[...] = jnp.sin(ang)


def _rope_tables(pos_rep, inv128, rows):
    n4 = pos_rep.shape[0]
    spec = pl.BlockSpec((rows, LANES), lambda i: (i, 0))
    return pl.pallas_call(
        _rope_table_kernel,
        grid=(n4 // rows,),
        in_specs=[spec, pl.BlockSpec((1, LANES), lambda i: (0, 0))],
        out_specs=[spec, spec],
        out_shape=[jax.ShapeDtypeStruct((n4, LANES), F32)] * 2,
        compiler_params=pltpu.CompilerParams(dimension_semantics=("arbitrary",)),
        name="ropetab",
    )(pos_rep, inv128)


def _inproj_kernel(x_ref, lnw_ref, w_ref, cos_ref, sin_ref, o_ref):
    a = _rms(x_ref[...], lnw_ref[...]).astype(BF16)
    half = D_HEAD // 2
    reps = LANES // half

    def expand(tab_ref):
        parts = [jnp.concatenate([tab_ref[:, j * half:(j + 1) * half]] * reps, axis=1) for j in range(reps)]
        return jnp.concatenate(parts, axis=0)

    cos = expand(cos_ref)
    sin = expand(sin_ref)
    lane = lax.broadcasted_iota(I32, (1, LANES), 1)
    first = (lane % D_HEAD) < (D_HEAD // 2)
    sin_signed = jnp.where(first, -sin, sin)
    n_chunks = IN_W // 256
    value_chunks = (4, 5)
    for c in [c for c in range(n_chunks) if c not in value_chunks] + list(value_chunks):
        p = jnp.dot(a, w_ref[:, c * 256:(c + 1) * 256], preferred_element_type=F32)
        for hh in range(2):
            g = c * 2 + hh
            xg = p[:, hh * LANES:(hh + 1) * LANES]
            is_v = (8 <= g < 12) or g == 17
            if not is_v:
                partner = jnp.where(first, pltpu.roll(xg, 96, 1), pltpu.roll(xg, 32, 1))
                xg = xg * cos + partner * sin_signed
                if g < 4:
                    xg = xg * (D_HEAD ** -0.5 * math.log2(math.e))
                elif 12 <= g < 16:
                    xg = xg * (D_HEAD ** -0.5)
            o_ref[:, g * LANES:(g + 1) * LANES] = xg.astype(BF16)


def _inproj(x2, ln_w, w_in_bf, cos, sin, tm):
    n, d = x2.shape
    dense_rows = tm // (LANES // (D_HEAD // 2))
    return pl.pallas_call(
        _inproj_kernel,
        grid=(n // tm,),
        in_specs=[
            pl.BlockSpec((tm, d), lambda i: (i, 0)),
            pl.BlockSpec((1, d), lambda i: (0, 0)),
            pl.BlockSpec((d, IN_W), lambda i: (0, 0)),
            pl.BlockSpec((dense_rows, LANES), lambda i: (i, 0)),
            pl.BlockSpec((dense_rows, LANES), lambda i: (i, 0)),
        ],
        out_specs=pl.BlockSpec((tm, IN_W), lambda i: (i, 0)),
        out_shape=jax.ShapeDtypeStruct((n, IN_W), BF16),
        compiler_params=pltpu.CompilerParams(
            dimension_semantics=("arbitrary",), vmem_limit_bytes=VMEM_LIMIT),
        name="inproj",
    )(x2, ln_w, w_in_bf, cos, sin)


def _diff_group_kernel(q_ref, k_ref, v_ref, lq1_ref, lk1_ref, lq2_ref, lk2_ref, sw_ref, o_ref,
                       vt_scr, q2_scr, acc_scr, m_scr, l_scr, s_scr, *, t, tk, lambda_init):
    s = k_ref.shape[0]
    per = tk // t
    for c in range(s // tk):
        for r in range(per):
            rows = slice(c * tk + r * t, c * tk + (r + 1) * t)
            vt_scr[c, :, r * t:(r + 1) * t] = v_ref[rows, :].astype(F32).T.astype(BF16)

    lane = lax.broadcasted_iota(I32, (1, LANES), 1)
    lam = (jnp.exp(jnp.sum(lq1_ref[...] * lk1_ref[...], axis=1, keepdims=True))
           - jnp.exp(jnp.sum(lq2_ref[...] * lk2_ref[...], axis=1, keepdims=True))
           + lambda_init)

    def accumulate(r, st_parts, vt):
        m_prev = m_scr[r]
        m_new = m_prev
        for st in st_parts:
            m_new = jnp.maximum(m_new, jnp.max(st, axis=0, keepdims=True))
        ps = [jnp.exp2(st - m_new) for st in st_parts]
        alpha = jnp.exp2(m_prev - m_new)
        l_new = alpha * l_scr[r]
        for p in ps:
            l_new = l_new + jnp.sum(p, axis=0, keepdims=True)
        l_scr[r] = l_new
        pb = [p.astype(BF16) for p in ps]
        pb = pb[0] if len(pb) == 1 else jnp.concatenate(pb, axis=0)
        acc_scr[r] = alpha * acc_scr[r] + jnp.dot(vt, pb, preferred_element_type=F32)
        m_scr[r] = m_new

    def interleaved(score_fn, softmax_fn, next_first_scores):
        for r in range(per):
            if r + 1 < per:
                score_fn(r + 1)
            else:
                next_first_scores()
            softmax_fn(r)

    def group(gi, carry):
        row0 = pl.multiple_of(gi * tk, tk)
        for r in range(per):
            q = q_ref[pl.ds(row0 + r * t, t), :]
            zero = jnp.zeros_like(q)
            q2_scr[r] = jnp.concatenate(
                [jnp.where(lane < D_HEAD, q, zero), jnp.where(lane >= D_HEAD, q, zero)], axis=0)
        acc_scr[...] = jnp.zeros(acc_scr.shape, F32)
        m_scr[...] = jnp.full(m_scr.shape, NEG, F32)
        l_scr[...] = jnp.zeros(l_scr.shape, F32)

        def diag_scores(r):
            rows = (r + 1) * t
            k = k_ref[pl.ds(row0, rows), :]
            s_scr[r, 0:rows, :] = lax.dot_general(k, q2_scr[r], NT_DIMS, preferred_element_type=F32)

        def diag_softmax(r):
            rows = (r + 1) * t
            key = lax.broadcasted_iota(I32, (t, 2 * t), 0)
            col = lax.broadcasted_iota(I32, (t, 2 * t), 1)
            diag = jnp.where(key <= jnp.where(col >= t, col - t, col), s_scr[r, r * t:rows, :], NEG)
            parts = [diag] if r == 0 else [s_scr[r, 0:r * t, :], diag]
            accumulate(r, parts, vt_scr[gi, :, 0:rows])

        def full_scores(j, r):
            k = k_ref[pl.ds(pl.multiple_of(j * tk, tk), tk), :]
            s_scr[r] = lax.dot_general(k, q2_scr[r], NT_DIMS, preferred_element_type=F32)

        diag_scores(0)
        interleaved(diag_scores, diag_softmax, functools.partial(full_scores, 0, 0))

        def key_block(j, c2):
            interleaved(functools.partial(full_scores, j), lambda r: accumulate(r, [s_scr[r]], vt_scr[j]),
                        functools.partial(full_scores, j + 1, 0))
            return c2

        lax.fori_loop(0, gi, key_block, 0)

        for r in range(per):
            on = acc_scr[r] / l_scr[r]
            o = on[:, :t] - lam * on[:, t:]
            ms = jnp.mean(o * o, axis=0, keepdims=True)
            o = o * lax.rsqrt(ms + EPS) * sw_ref[...] * (1.0 - lambda_init)
            o_ref[pl.ds(row0 + r * t, t), :] = o.T.astype(BF16)
        return carry

    lax.fori_loop(0, s // tk, group, 0)


def _diff_attention(qkv, lq1, lk1, lq2, lk2, subln_col, b, s, t, tk, lambda_init):
    n = qkv.shape[0]
    per = tk // t
    small = pl.BlockSpec((1, D_HEAD), lambda bi, h: (0, 0))
    return pl.pallas_call(
        functools.partial(_diff_group_kernel, t=t, tk=tk, lambda_init=lambda_init),
        grid=(b, H_DIFF),
        in_specs=[
            pl.BlockSpec((s, LANES), lambda bi, h: (bi, h)),
            pl.BlockSpec((s, LANES), lambda bi, h: (bi, 4 + h)),
            pl.BlockSpec((s, LANES), lambda bi, h: (bi, 8 + h)),
            small, small, small, small,
            pl.BlockSpec((LANES, 1), lambda bi, h: (0, 0)),
        ],
        out_specs=pl.BlockSpec((s, LANES), lambda bi, h: (bi, h)),
        out_shape=jax.ShapeDtypeStruct((n, H_DIFF * LANES), BF16),
        scratch_shapes=[
            pltpu.VMEM((s // tk, LANES, tk), BF16),
            pltpu.VMEM((per, 2 * t, LANES), BF16),
            pltpu.VMEM((per, LANES, 2 * t), F32),
            pltpu.VMEM((per, 1, 2 * t), F32),
            pltpu.VMEM((per, 1, 2 * t), F32),
            pltpu.VMEM((per, tk, 2 * t), F32),
        ],
        compiler_params=pltpu.CompilerParams(
            dimension_semantics=("arbitrary", "arbitrary"), vmem_limit_bytes=VMEM_LIMIT),
        name="diffattn",
    )(qkv, qkv, qkv, lq1, lk1, lq2, lk2, subln_col)


def _swa_kernel(sink_ref, q_ref, kc_ref, kp_ref, vc_ref, vp_ref, o_ref, kbuf, vbuf, *, tq):
    i = pl.program_id(1)
    w = WINDOW
    kbuf[0:w, :] = kp_ref[...]
    kbuf[w:w + tq, :] = kc_ref[...]
    vbuf[0:w, :] = vp_ref[...]
    vbuf[w:w + tq, :] = vc_ref[...]
    lane = lax.broadcasted_iota(I32, (1, LANES), 1)
    lo = lane < D_HEAD
    qi = lax.broadcasted_iota(I32, (w, 2 * w), 0)
    ki = lax.broadcasted_iota(I32, (w, 2 * w), 1)
    band = (ki > qi) & (ki <= qi + w)
    band_first = band & (ki >= jnp.where(i > 0, 0, w))
    def scores(r):
        keys = kbuf[r * w:(r + 2) * w, :]
        qs = []
        for half in range(2):
            for p in range(4):
                qg = q_ref[r * w:(r + 1) * w, p * LANES:(p + 1) * LANES]
                qs.append(jnp.where(lo if half == 0 else ~lo, qg, jnp.zeros_like(qg)))
        qstack = jnp.concatenate(qs, axis=0)
        return lax.dot_general(qstack, keys, NT_DIMS, preferred_element_type=F32)

    s_next = scores(0)
    for r in range(tq // w):
        s_all = s_next
        if r + 1 < tq // w:
            s_next = scores(r + 1)
        vals = vbuf[r * w:(r + 2) * w, :]
        valid = band_first if r == 0 else band
        ps = []
        for hb in range(8):
            sink = sink_ref[hb]
            s = jnp.where(valid, s_all[hb * w:(hb + 1) * w], NEG)
            m = jnp.maximum(jnp.max(s, axis=1, keepdims=True), sink)
            e = jnp.exp(s - m)
            den = jnp.sum(e, axis=1, keepdims=True) + jnp.exp(sink - m)
            ps.append((e / den).astype(BF16))
        pv = jnp.dot(jnp.concatenate(ps, axis=0), vals, preferred_element_type=F32)
        for p in range(4):
            og = jnp.where(lo, pv[p * w:(p + 1) * w], pv[(4 + p) * w:(5 + p) * w])
            o_ref[r * w:(r + 1) * w, p * LANES:(p + 1) * LANES] = og.astype(BF16)


def _swa_attention(qkv, sinks, b, s, tq):
    n = qkv.shape[0]
    nq = s // tq
    per = tq // WINDOW
    prev_map = lambda col: (lambda bi, i, sk: (jnp.maximum(bi * (s // WINDOW) + i * per - 1, 0), col))
    cur_map = lambda col: (lambda bi, i, sk: (bi * nq + i, col))
    return pl.pallas_call(
        functools.partial(_swa_kernel, tq=tq),
        grid_spec=pltpu.PrefetchScalarGridSpec(
            num_scalar_prefetch=1,
            grid=(b, nq),
            in_specs=[
                pl.BlockSpec((tq, 4 * LANES), lambda bi, i, sk: (bi * nq + i, 3)),
                pl.BlockSpec((tq, LANES), cur_map(16)),
                pl.BlockSpec((WINDOW, LANES), prev_map(16)),
                pl.BlockSpec((tq, LANES), cur_map(17)),
                pl.BlockSpec((WINDOW, LANES), prev_map(17)),
            ],
            out_specs=pl.BlockSpec((tq, 4 * LANES), lambda bi, i, sk: (bi * nq + i, 0)),
            scratch_shapes=[pltpu.VMEM((WINDOW + tq, LANES), BF16)] * 2,
        ),
        out_shape=jax.ShapeDtypeStruct((n, 4 * LANES), BF16),
        compiler_params=pltpu.CompilerParams(
            dimension_semantics=("arbitrary", "arbitrary"), vmem_limit_bytes=VMEM_LIMIT),
        name="swa",
    )(sinks, qkv, qkv, qkv, qkv, qkv)


def _memkv_kernel(m_ref, lnw_ref, w_ref, o_ref):
    a = _rms(m_ref[...], lnw_ref[...]).astype(BF16)
    o_ref[...] = jnp.dot(a, w_ref[...], preferred_element_type=F32).astype(BF16)


def _memkv(mem2, ln_w, wkv_bf, m_len):
    n, d = mem2.shape
    return pl.pallas_call(
        _memkv_kernel,
        grid=(n // m_len,),
        in_specs=[
            pl.BlockSpec((m_len, d), lambda i: (i, 0)),
            pl.BlockSpec((1, d), lambda i: (0, 0)),
            pl.BlockSpec((d, 2 * d), lambda i: (0, 0)),
        ],
        out_specs=pl.BlockSpec((m_len, 2 * d), lambda i: (i, 0)),
        out_shape=jax.ShapeDtypeStruct((n, 2 * d), BF16),
        compiler_params=pltpu.CompilerParams(
            dimension_semantics=("arbitrary",), vmem_limit_bytes=VMEM_LIMIT),
        name="memkv",
    )(mem2, ln_w, wkv_bf)


def _split_bf16(x):
    hi = x.astype(BF16)
    lo = (x - hi.astype(F32)).astype(BF16)
    return hi, lo


def _post_kernel(x_ref, od_ref, os_ref, wout_ref, lnc_ref, wq_ref, kv_ref, wo_ref, lnm_ref, wr_ref, br_ref,
                 h_ref, f_ref, lpos_ref, gate_ref, tcnt_ref, tbase_ref, base_scr, fprev_scr, *, tm, d):
    g = pl.program_id(0)

    @pl.when(g == 0)
    def _():
        base_scr[...] = jnp.zeros(base_scr.shape, F32)
        fprev_scr[...] = jnp.zeros(fprev_scr.shape, F32)

    mix = jnp.concatenate([od_ref[...], os_ref[...]], axis=1)
    h1 = x_ref[...] + jnp.dot(mix, wout_ref[...], preferred_element_type=F32)
    logits = _route_logits(fprev_scr[...], wr_ref, br_ref)

    c = _rms(h1, lnc_ref[...]).astype(BF16)
    dc = d // H_CROSS
    q = (jnp.dot(c, wq_ref[...], preferred_element_type=F32) * (dc ** -0.5)).astype(BF16)
    eid1, eid2 = _route_select(logits, gate_ref, tm)
    outs = []
    for hd in range(H_CROSS):
        k = kv_ref[:, hd * dc:(hd + 1) * dc]
        v = kv_ref[:, d + hd * dc:d + (hd + 1) * dc]
        s = lax.dot_general(q[:, hd * dc:(hd + 1) * dc], k, NT_DIMS, preferred_element_type=F32)
        e = jnp.exp(s - jnp.max(s, axis=1, keepdims=True))
        den = jnp.sum(e, axis=1, keepdims=True)
        outs.append((jnp.dot(e.astype(BF16), v, preferred_element_type=F32) / den).astype(BF16))
    o = jnp.concatenate(outs, axis=1)
    h2 = h1 + jnp.dot(o, wo_ref[...], preferred_element_type=F32)
    _route_positions(eid1, eid2, jnp.where(g > 0, 1.0, 0.0), lpos_ref, tcnt_ref, tbase_ref, base_scr, tm)
    h_ref[...] = h2
    f = _rms(h2, lnm_ref[...])
    f_ref[...] = f.astype(BF16)
    fprev_scr[...] = f


def _route_logits(f, wr_ref, br_ref):
    f_hi, f_lo = _split_bf16(f)
    w_hi, w_lo = _split_bf16(wr_ref[...])
    return (lax.dot_general(w_hi, f_hi, NT_DIMS, preferred_element_type=F32)
            + lax.dot_general(w_hi, f_lo, NT_DIMS, preferred_element_type=F32)
            + lax.dot_general(w_lo, f_hi, NT_DIMS, preferred_element_type=F32)) + br_ref[...]


def _route_select(lg, gate_ref, tm):
    gl = lg[0:N_GROUPS]
    gmax = jnp.max(gl, axis=0, keepdims=True)
    gidx = lax.broadcasted_iota(I32, gl.shape, 0)
    g_sel = jnp.min(jnp.where(gl == gmax, gidx, N_GROUPS), axis=0, keepdims=True)
    g_p = 1.0 / jnp.sum(jnp.exp(gl - gmax), axis=0, keepdims=True)

    e8 = jnp.zeros((E_PER_GROUP, tm), F32)
    for g in range(N_GROUPS):
        e8 = e8 + jnp.where(g_sel == g, lg[8 + g * E_PER_GROUP:8 + (g + 1) * E_PER_GROUP], 0.0)
    ex = jnp.exp(e8 - jnp.max(e8, axis=0, keepdims=True))
    ep = ex / jnp.sum(ex, axis=0, keepdims=True)
    idx8 = lax.broadcasted_iota(I32, ep.shape, 0)
    p1 = jnp.max(ep, axis=0, keepdims=True)
    i1 = jnp.min(jnp.where(ep == p1, idx8, E_PER_GROUP), axis=0, keepdims=True)
    ep2 = jnp.where(idx8 == i1, -1.0, ep)
    p2 = jnp.max(ep2, axis=0, keepdims=True)
    i2 = jnp.min(jnp.where(ep2 == p2, idx8, E_PER_GROUP), axis=0, keepdims=True)
    psum = p1 + p2
    gate_ref[0:1, :] = g_p * (p1 / psum)
    gate_ref[1:2, :] = g_p * (p2 / psum)
    return g_sel * E_PER_GROUP + i1, g_sel * E_PER_GROUP + i2


def _route_positions(eid1, eid2, live, lpos_ref, tcnt_ref, tbase_ref, base_scr, tm):
    e32 = lax.broadcasted_iota(I32, (N_EXPERTS, tm), 0)
    oh1 = (e32 == eid1).astype(F32)
    oh2 = (e32 == eid2).astype(F32)
    cnt = oh1 + oh2
    tr = lax.broadcasted_iota(I32, (tm, tm), 0)
    tc = lax.broadcasted_iota(I32, (tm, tm), 1)
    upper = jnp.where(tr < tc, 1.0, 0.0).astype(BF16)
    earlier = jnp.dot(cnt.astype(BF16), upper, preferred_element_type=F32)
    seg = jnp.sum(cnt, axis=1, keepdims=True)
    seg = jnp.floor((seg + (SUBLANES - 1)) * (1.0 / SUBLANES)) * SUBLANES * live
    seg = jnp.broadcast_to(seg, (N_EXPERTS, LANES))
    er = lax.broadcasted_iota(I32, (N_EXPERTS, N_EXPERTS), 0)
    ec = lax.broadcasted_iota(I32, (N_EXPERTS, N_EXPERTS), 1)
    lower = jnp.where(ec < er, 1.0, 0.0).astype(BF16)
    start = jnp.dot(lower, seg.astype(BF16), preferred_element_type=F32)[:, 0:1]
    where = earlier + start
    lpos_ref[0:1, :] = jnp.sum(oh1 * where, axis=0, keepdims=True).astype(I32)
    lpos_ref[1:2, :] = jnp.sum(oh2 * where, axis=0, keepdims=True).astype(I32)
    tcnt_ref[...] = seg
    tbase_ref[...] = base_scr[...]
    base_scr[...] = base_scr[...] + seg


def _post(x2, od, osw, wout_bf, lnc, wq_bf, kv, wo_bf, lnm, wr, br, b, s, tm, m_len):
    n, d = x2.shape
    nt = s // tm
    tiles = n // tm
    main = lambda g: jnp.minimum(g, tiles - 1)
    routed = lambda g: jnp.maximum(g - 1, 0)
    row = lambda g: (main(g), 0)
    const = lambda g: (0, 0)
    return pl.pallas_call(
        functools.partial(_post_kernel, tm=tm, d=d),
        grid=(tiles + 1,),
        in_specs=[
            pl.BlockSpec((tm, d), row),
            pl.BlockSpec((tm, d // 2), row),
            pl.BlockSpec((tm, d // 2), row),
            pl.BlockSpec((d, d), const),
            pl.BlockSpec((1, d), const),
            pl.BlockSpec((d, d), const),
            pl.BlockSpec((m_len, 2 * d), lambda g: (main(g) // nt, 0)),
            pl.BlockSpec((d, d), const),
            pl.BlockSpec((1, d), const),
            pl.BlockSpec((8 + N_EXPERTS, d), const),
            pl.BlockSpec((8 + N_EXPERTS, 1), const),
        ],
        out_specs=[
            pl.BlockSpec((tm, d), row),
            pl.BlockSpec((tm, d), row),
            pl.BlockSpec((2, tm), lambda g: (0, routed(g))),
            pl.BlockSpec((2, tm), lambda g: (0, routed(g))),
            pl.BlockSpec((None, N_EXPERTS, LANES), lambda g: (routed(g), 0, 0)),
            pl.BlockSpec((None, N_EXPERTS, LANES), lambda g: (routed(g), 0, 0)),
        ],
        out_shape=[
            jax.ShapeDtypeStruct((n, d), F32),
            jax.ShapeDtypeStruct((n, d), BF16),
            jax.ShapeDtypeStruct((2, n), I32),
            jax.ShapeDtypeStruct((2, n), F32),
            jax.ShapeDtypeStruct((n // tm, N_EXPERTS, LANES), F32),
            jax.ShapeDtypeStruct((n // tm, N_EXPERTS, LANES), F32),
        ],
        scratch_shapes=[pltpu.VMEM((N_EXPERTS, LANES), F32), pltpu.VMEM((tm, d), F32)],
        compiler_params=pltpu.CompilerParams(
            dimension_semantics=("arbitrary",), vmem_limit_bytes=VMEM_LIMIT),
        name="post",
    )(x2, od, osw, wout_bf, lnc, wq_bf, kv, wo_bf, lnm, wr, br)


def _local_rows(tm):
    return 2 * tm + SUBLANES * N_EXPERTS


def _segment_copies(tile, tcnt_ref, tbase_ref, pstart_ref, make, act, max_len):
    sizes = [SUBLANES << k for k in reversed(range((max_len // SUBLANES).bit_length()))]

    if act == "wait":
        total = lax.fori_loop(0, N_EXPERTS, lambda e, acc: acc + tcnt_ref[tile * N_EXPERTS + e], 0)
        for size in sizes:
            @pl.when((total & size) != 0)
            def _():
                make(0, 0, size).wait()

        return total

    def pieces(ln, local0, glob0, some_sizes):
        for size in some_sizes:
            @pl.when((ln & size) != 0)
            def _():
                off = (ln // (2 * size)) * (2 * size)
                make(pl.multiple_of(local0 + off, SUBLANES), pl.multiple_of(glob0 + off, SUBLANES), size).start()

    big = [s for s in sizes if s >= LONG_SEGMENT_ROWS]
    small = [s for s in sizes if s < LONG_SEGMENT_ROWS]

    def per_expert(e, local0):
        ln = tcnt_ref[tile * N_EXPERTS + e]
        glob0 = pstart_ref[e] + tbase_ref[tile * N_EXPERTS + e]

        @pl.when(ln >= LONG_SEGMENT_ROWS)
        def _():
            pieces(ln, local0, glob0, big)

        pieces(ln, local0, glob0, small)
        return local0 + ln

    return lax.fori_loop(0, N_EXPERTS, per_expert, 0)


def _dispatch_kernel(tcnt_ref, tbase_ref, tot_ref, f_ref, lpos_ref, gate_ref, xs_hbm, pstart_ref, blk_ref, nact_ref,
                     loc_scr, zero_scr, sem, zsem, *, tm, bm, nblk):
    i = pl.program_id(0)
    nt = pl.num_programs(0)
    slot = i % 2

    @pl.when(i == 0)
    def _():
        def per_expert(e, blk0):
            nb = (tot_ref[e] + (bm - 1)) // bm
            pstart_ref[e] = blk0 * bm

            def fill(j, carry):
                blk_ref[j] = e
                return carry

            lax.fori_loop(blk0, blk0 + nb, fill, 0)
            return blk0 + nb

        nact = lax.fori_loop(0, N_EXPERTS, per_expert, 0)
        nact_ref[0] = nact

        def tail(j, carry):
            blk_ref[j] = N_EXPERTS - 1
            return carry

        lax.fori_loop(nact, nblk, tail, 0)

        zero_scr[...] = jnp.zeros(zero_scr.shape, U32)

        def zero_fill(act):
            def do(copy):
                if act == "start":
                    copy.start()
                else:
                    copy.wait()

            def per_expert_pad(e, carry):
                tot = tot_ref[e]
                pad = (tot + (bm - 1)) // bm * bm - tot
                row0 = pstart_ref[e] + tot
                for k in reversed(range((bm // SUBLANES).bit_length())):
                    size = SUBLANES << k

                    @pl.when((pad & size) != 0)
                    def _():
                        row = pl.multiple_of(row0 + (pad // (2 * size)) * (2 * size), SUBLANES)
                        do(pltpu.make_async_copy(zero_scr.at[pl.ds(0, size)], xs_hbm.at[pl.ds(row, size)], zsem))

                return carry

            lax.fori_loop(0, N_EXPERTS, per_expert_pad, 0)

            def per_unused_block(j, carry):
                do(pltpu.make_async_copy(zero_scr, xs_hbm.at[pl.ds(pl.multiple_of(j * bm, bm), bm)], zsem))
                return carry

            lax.fori_loop(nact, nblk, per_unused_block, 0)

        zero_fill("start")
        zero_fill("wait")

    def copies(tile, sl, act):
        def make(lrow, grow, size):
            return pltpu.make_async_copy(loc_scr.at[sl, pl.ds(lrow, size)], xs_hbm.at[pl.ds(grow, size)], sem.at[sl])

        _segment_copies(tile, tcnt_ref, tbase_ref, pstart_ref, make, act, _local_rows(tm))

    @pl.when(i >= 2)
    def _():
        copies(i - 2, slot, "wait")

    half = f_ref.shape[1] // 2
    chunk = math.gcd(_local_rows(tm), 256)
    for r0 in range(0, _local_rows(tm), chunk):
        pos = lax.broadcasted_iota(I32, (chunk, tm), 0) + r0
        first = pos == lpos_ref[0:1, :]
        second = pos == lpos_ref[1:2, :]
        onehot = jnp.where(first, 1.0, jnp.where(second, 1.0, 0.0)).astype(BF16)
        loc_scr[slot, r0:r0 + chunk, 0:half] = _pack_pairs(
            jnp.dot(onehot, f_ref[...], preferred_element_type=F32))
        gate = jnp.sum(jnp.where(first, gate_ref[0:1, :], jnp.where(second, gate_ref[1:2, :], 0.0)),
                       axis=1, keepdims=True)
        loc_scr[slot, r0:r0 + chunk, half:half + LANES] = lax.bitcast_convert_type(
            jnp.broadcast_to(gate, (chunk, LANES)), U32)
    copies(i, slot, "start")

    @pl.when(i == nt - 1)
    def _():
        @pl.when(i >= 1)
        def _():
            copies(i - 1, 1 - slot, "wait")

        copies(i, slot, "wait")


def _dispatch(tcnt_i, tbase_i, tot_i, f, lpos, gates, tm, bm, nblk):
    n, d = f.shape
    width = d // 2 + LANES
    smem = pl.BlockSpec(memory_space=pltpu.SMEM)
    hbm = pl.BlockSpec(memory_space=pl.ANY)
    return pl.pallas_call(
        functools.partial(_dispatch_kernel, tm=tm, bm=bm, nblk=nblk),
        grid_spec=pltpu.PrefetchScalarGridSpec(
            num_scalar_prefetch=3,
            grid=(n // tm,),
            in_specs=[
                pl.BlockSpec((tm, d), lambda i, *_: (i, 0)),
                pl.BlockSpec((2, tm), lambda i, *_: (0, i)),
                pl.BlockSpec((2, tm), lambda i, *_: (0, i)),
            ],
            out_specs=[hbm, smem, smem, smem],
            scratch_shapes=[
                pltpu.VMEM((2, _local_rows(tm), width), U32),
                pltpu.VMEM((bm, width), U32),
                pltpu.SemaphoreType.DMA((2,)),
                pltpu.SemaphoreType.DMA,
            ],
        ),
        out_shape=[
            jax.ShapeDtypeStruct((nblk * bm, width), U32),
            jax.ShapeDtypeStruct((N_EXPERTS,), I32),
            jax.ShapeDtypeStruct((nblk,), I32),
            jax.ShapeDtypeStruct((1,), I32),
        ],
        compiler_params=pltpu.CompilerParams(
            dimension_semantics=("arbitrary",), vmem_limit_bytes=VMEM_LIMIT),
        name="dispatch",
    )(tcnt_i, tbase_i, tot_i, f, lpos, gates)


def _expert_kernel(blk_ref, nact_ref, x_ref, wg_ref, wu_ref, wd_ref, y_ref, wg_bf, wu_bf, wd_bf):
    j = pl.program_id(0)
    active = j < nact_ref[0]

    @pl.when(active & ((j == 0) | (blk_ref[j] != blk_ref[jnp.maximum(j - 1, 0)])))
    def _():
        wg_bf[...] = wg_ref[...].astype(BF16)
        wu_bf[...] = wu_ref[...].astype(BF16)
        wd_bf[...] = wd_ref[...].astype(BF16)

    @pl.when(active)
    def _():
        half = y_ref.shape[1]
        x = _unpack_pairs(x_ref[:, 0:half])
        gate = lax.bitcast_convert_type(x_ref[:, half:half + 1], F32)
        g = jnp.dot(x, wg_bf[...], preferred_element_type=F32)
        u = jnp.dot(x, wu_bf[...], preferred_element_type=F32)
        hdn = (g * jax.nn.sigmoid(g) * u).astype(BF16)
        y = jnp.dot(hdn, wd_bf[...], preferred_element_type=F32) * gate
        y_ref[...] = _pack_pairs(y.astype(BF16).astype(F32))

    @pl.when(j >= nact_ref[0])
    def _():
        y_ref[...] = jnp.zeros(y_ref.shape, y_ref.dtype)


def _experts(blk_e, nact, xs, wg, wu, wd, bm):
    p, width = xs.shape
    d, dff = wg.shape[-2:]
    dh = d // 2
    nblk = p // bm
    rowmap = lambda j, blk, na: (jnp.minimum(j, na[0] - 1), 0)
    wmap = lambda j, blk, na: (blk[jnp.minimum(j, na[0] - 1)], 0, 0)
    return pl.pallas_call(
        _expert_kernel,
        grid_spec=pltpu.PrefetchScalarGridSpec(
            num_scalar_prefetch=2,
            grid=(nblk,),
            in_specs=[
                pl.BlockSpec((bm, width), rowmap),
                pl.BlockSpec((None, d, dff), wmap),
                pl.BlockSpec((None, d, dff), wmap),
                pl.BlockSpec((None, dff, d), wmap),
            ],
            out_specs=pl.BlockSpec((bm, dh), lambda j, blk, na: (j, 0)),
            scratch_shapes=[pltpu.VMEM((d, dff), BF16), pltpu.VMEM((d, dff), BF16), pltpu.VMEM((dff, d), BF16)],
        ),
        out_shape=jax.ShapeDtypeStruct((p, dh), U32),
        compiler_params=pltpu.CompilerParams(
            dimension_semantics=("arbitrary",), vmem_limit_bytes=VMEM_LIMIT),
        name="experts",
    )(blk_e, nact, xs, wg, wu, wd)


def _combine_kernel(tcnt_ref, tbase_ref, pstart_ref, h_ref, lpos_ref, lnf_ref, ys_hbm, o_ref,
                    loc_scr, sem, *, tm):
    i = pl.program_id(0)
    nt = pl.num_programs(0)
    slot = i % 2

    def copies(tile, sl, act):
        def make(lrow, grow, size):
            return pltpu.make_async_copy(ys_hbm.at[pl.ds(grow, size)], loc_scr.at[sl, pl.ds(lrow, size)], sem.at[sl])

        return _segment_copies(tile, tcnt_ref, tbase_ref, pstart_ref, make, act, _local_rows(tm))

    @pl.when(i == 0)
    def _():
        copies(0, 0, "start")

    @pl.when(i + 1 < nt)
    def _():
        copies(i + 1, 1 - slot, "start")

    used = copies(i, slot, "wait")
    row = lax.broadcasted_iota(I32, (_local_rows(tm), 1), 0)
    ys = _unpack_pairs(jnp.where(row < used, loc_scr[slot], U32(0)))

    r8 = lax.broadcasted_iota(I32, (8, tm), 0)
    lp = lpos_ref[...].astype(F32)
    top = jnp.where(r8 == 0, lp[0:1], jnp.where(r8 == 1, lp[1:2], 0.0))
    cols = jnp.concatenate([top, jnp.zeros((LANES - 8, tm), F32)], axis=0).T.astype(I32)
    pos = lax.broadcasted_iota(I32, (tm, _local_rows(tm)), 1)
    sel = jnp.where(pos == cols[:, 0:1], 1.0, jnp.where(pos == cols[:, 1:2], 1.0, 0.0)).astype(BF16)
    o_ref[...] = _rms(h_ref[...] + jnp.dot(sel, ys, preferred_element_type=F32), lnf_ref[...])


def _combine(tcnt_i, tbase_i, pstart, h2, lpos, lnf, ys, tm):
    n, d = h2.shape
    return pl.pallas_call(
        functools.partial(_combine_kernel, tm=tm),
        grid_spec=pltpu.PrefetchScalarGridSpec(
            num_scalar_prefetch=3,
            grid=(n // tm,),
            in_specs=[
                pl.BlockSpec((tm, d), lambda i, *_: (i, 0)),
                pl.BlockSpec((2, tm), lambda i, *_: (0, i)),
                pl.BlockSpec((1, d), lambda i, *_: (0, 0)),
                pl.BlockSpec(memory_space=pl.ANY),
            ],
            out_specs=pl.BlockSpec((tm, d), lambda i, *_: (i, 0)),
            scratch_shapes=[pltpu.VMEM((2, _local_rows(tm), d // 2), U32), pltpu.SemaphoreType.DMA((2,))],
        ),
        out_shape=jax.ShapeDtypeStruct((n, d), F32),
        compiler_params=pltpu.CompilerParams(
            dimension_semantics=("arbitrary",), vmem_limit_bytes=VMEM_LIMIT),
        name="combine",
    )(tcnt_i, tbase_i, pstart, h2, lpos, lnf, ys)


def _swa_head_perm():
    cols = []
    for p in range(4):
        for half in range(2):
            head = half * 4 + p
            cols.extend(range(head * D_HEAD, (head + 1) * D_HEAD))
    return jnp.asarray(cols, dtype=I32)


def kernel(x, mem, positions, ln_mix_w, w_in, lambda_q1, lambda_k1, lambda_q2, lambda_k2, subln_w, sinks,
           w_out, ln_cross_w, ln_mem_w, wq_cross, wkv_cross, wo_cross, ln_moe_w, w_group, b_group,
           w_expert, b_expert, w_gate, w_up, w_down, ln_final_w):
    b, s, d = x.shape
    m_len = mem.shape[1]
    n = b * s
    assert w_in.shape[0] == 1 and d == 1024 and n <= 65536
    lambda_init = 0.8 - 0.6 * math.exp(-0.3 * 0)

    tm = min(TOKEN_TILE, s)
    t_attn = ATTN_QUERY_TILE
    tk_attn = ATTN_KEY_BLOCK
    bm = EXPERT_BLOCK_ROWS

    x2 = x.reshape(n, d)
    half = D_HEAD // 2
    per_row = LANES // half
    inv_freq = jnp.exp(-math.log(ROPE_THETA) * jnp.arange(0, D_HEAD, 2, dtype=F32) / D_HEAD)
    inv128 = jnp.tile(inv_freq, per_row).reshape(1, LANES)
    pos_tiles = positions.reshape(n // tm, per_row, tm // per_row).astype(I32).transpose(0, 2, 1)
    pos_rep = jnp.repeat(pos_tiles.reshape(n // per_row, per_row), half, axis=1)
    cos_t, sin_t = _rope_tables(pos_rep, inv128, min(1024, n // per_row))

    perm = _swa_head_perm()
    sq0 = 3 * 512
    w_in_l = w_in[0]
    w_in_p = jnp.concatenate([w_in_l[:, :sq0], w_in_l[:, sq0:sq0 + 512][:, perm], w_in_l[:, sq0 + 512:]], axis=1)
    w_out_l = w_out[0]
    w_out_p = jnp.concatenate([w_out_l[:512], w_out_l[512:][perm]], axis=0)
    sinks_p = sinks[0].reshape(2, 4).reshape(-1)

    qkv = _inproj(x2, ln_mix_w[0].reshape(1, d), w_in_p.astype(BF16), cos_t, sin_t, tm)
    o_diff = _diff_attention(qkv, lambda_q1[0].reshape(1, -1), lambda_k1[0].reshape(1, -1),
                             lambda_q2[0].reshape(1, -1), lambda_k2[0].reshape(1, -1),
                             subln_w[0].reshape(-1, 1), b, s, t_attn, min(tk_attn, s), lambda_init)
    o_swa = _swa_attention(qkv, sinks_p.astype(F32), b, s, tm)
    kv = _memkv(mem.reshape(b * m_len, d), ln_mem_w[0].reshape(1, d), wkv_cross[0].astype(BF16), m_len)

    wr = jnp.concatenate([w_group[0].T, jnp.zeros((8 - N_GROUPS, d), F32), w_expert[0].T], axis=0)
    br = jnp.concatenate([b_group[0], jnp.zeros((8 - N_GROUPS,), F32), b_expert[0]]).reshape(-1, 1)
    h2, f, lpos, gates, tcnt, tbase = _post(
        x2, o_diff, o_swa, w_out_p.astype(BF16), ln_cross_w[0].reshape(1, d), wq_cross[0].astype(BF16), kv,
        wo_cross[0].astype(BF16), ln_moe_w[0].reshape(1, d), wr, br, b, s, tm, m_len)

    tcnt_i = tcnt[:, :, 0].astype(I32).reshape(-1)
    tbase_i = tbase[:, :, 0].astype(I32).reshape(-1)
    tot_i = tbase_i[-N_EXPERTS:] + tcnt_i[-N_EXPERTS:]
    rows_max = 2 * n + (n // tm) * N_EXPERTS * (SUBLANES - 1)
    nblk = (rows_max + N_EXPERTS * (bm - 1) + bm - 1) // bm
    xs, pstart, blk_e, nact = _dispatch(tcnt_i, tbase_i, tot_i, f, lpos, gates, tm, bm, nblk)
    ys = _experts(blk_e, nact, xs, w_gate[0], w_up[0], w_down[0], bm)
    out = _combine(tcnt_i, tbase_i, pstart, h2, lpos, ln_final_w.reshape(1, d), ys, tm)
    return out.reshape(b, s, d)
```

```python
import functools
import math

import jax
import jax.numpy as jnp
from jax import lax
from jax.experimental import pallas as pl
from jax.experimental.pallas import tpu as pltpu

F32 = jnp.float32
BF16 = jnp.bfloat16
I32 = jnp.int32

D_HEAD = 64
ROPE_THETA = 10000.0
H_DIFF = 4
N_Q_SWA = 8
N_KV_SWA = 2
WINDOW = 128
H_CROSS = 4
N_GROUPS = 4
E_PER_GROUP = 8
N_EXPERTS = N_GROUPS * E_PER_GROUP
EPS = 1e-6
LANES = 128
SUBLANES = 8
LONG_SEGMENT_ROWS = 128
IN_W = 2304
NEG = -1e30

VMEM_LIMIT = 56 * 1024 * 1024

TOKEN_TILE = 512
WIDE_TOKEN_TILE = 1024
ATTN_QUERY_TILE = 256
ATTN_KEY_BLOCK = 1024
EXPERT_BLOCK_ROWS = 512

NT_DIMS = (((1,), (1,)), ((), ()))


U32 = jnp.uint32


def _pack_pairs(x):
    h = x.shape[1] // 2
    hi = lax.bitcast_convert_type(x[:, :h], U32)
    lo = lax.bitcast_convert_type(x[:, h:], U32)
    return hi | (lo >> 16)


def _unpack_pairs(p):
    a = lax.bitcast_convert_type(p & U32(0xFFFF0000), F32)
    b = lax.bitcast_convert_type(p << 16, F32)
    return jnp.concatenate([a, b], axis=1).astype(BF16)


def _rms(x, w):
    ms = jnp.mean(x * x, axis=-1, keepdims=True)
    return x * lax.rsqrt(ms + EPS) * w


def _rope_table_kernel(pos_ref, inv_ref, cos_ref, sin_ref):
    ang = pos_ref[...].astype(F32) * inv_ref[...]
    cos_ref[...] = jnp.cos(ang)
    sin_ref[...] = jnp.sin(ang)


def _rope_tables(pos_rep, inv128, rows):
    n4 = pos_rep.shape[0]
    spec = pl.BlockSpec((rows, LANES), lambda i: (i, 0))
    return pl.pallas_call(
        _rope_table_kernel,
        grid=(n4 // rows,),
        in_specs=[spec, pl.BlockSpec((1, LANES), lambda i: (0, 0))],
        out_specs=[spec, spec],
        out_shape=[jax.ShapeDtypeStruct((n4, LANES), F32)] * 2,
        compiler_params=pltpu.CompilerParams(dimension_semantics=("arbitrary",)),
        name="ropetab",
    )(pos_rep, inv128)


def _inproj_kernel(x_ref, lnw_ref, w_ref, cos_ref, sin_ref, o_ref):
    a = _rms(x_ref[...], lnw_ref[...]).astype(BF16)
    half = D_HEAD // 2
    reps = LANES // half

    def expand(tab_ref):
        parts = [jnp.concatenate([tab_ref[:, j * half:(j + 1) * half]] * reps, axis=1) for j in range(reps)]
        return jnp.concatenate(parts, axis=0)

    cos = expand(cos_ref)
    sin = expand(sin_ref)
    lane = lax.broadcasted_iota(I32, (1, LANES), 1)
    first = (lane % D_HEAD) < (D_HEAD // 2)
    sin_signed = jnp.where(first, -sin, sin)
    n_chunks = IN_W // 256
    value_chunks = (4, 5)
    for c in [c for c in range(n_chunks) if c not in value_chunks] + list(value_chunks):
        p = jnp.dot(a, w_ref[:, c * 256:(c + 1) * 256], preferred_element_type=F32)
        for hh in range(2):
            g = c * 2 + hh
            xg = p[:, hh * LANES:(hh + 1) * LANES]
            is_v = (8 <= g < 12) or g == 17
            if not is_v:
                partner = jnp.where(first, pltpu.roll(xg, 96, 1), pltpu.roll(xg, 32, 1))
                xg = xg * cos + partner * sin_signed
                if g < 4 or 12 <= g < 16:
                    xg = xg * (D_HEAD ** -0.5 * math.log2(math.e))
            o_ref[:, g * LANES:(g + 1) * LANES] = xg.astype(BF16)


def _inproj(x2, ln_w, w_in_bf, cos, sin, tm):
    n, d = x2.shape
    dense_rows = tm // (LANES // (D_HEAD // 2))
    return pl.pallas_call(
        _inproj_kernel,
        grid=(n // tm,),
        in_specs=[
            pl.BlockSpec((tm, d), lambda i: (i, 0)),
            pl.BlockSpec((1, d), lambda i: (0, 0)),
            pl.BlockSpec((d, IN_W), lambda i: (0, 0)),
            pl.BlockSpec((dense_rows, LANES), lambda i: (i, 0)),
            pl.BlockSpec((dense_rows, LANES), lambda i: (i, 0)),
        ],
        out_specs=pl.BlockSpec((tm, IN_W), lambda i: (i, 0)),
        out_shape=jax.ShapeDtypeStruct((n, IN_W), BF16),
        compiler_params=pltpu.CompilerParams(
            dimension_semantics=("arbitrary",), vmem_limit_bytes=VMEM_LIMIT),
        name="inproj",
    )(x2, ln_w, w_in_bf, cos, sin)


def _diff_group_kernel(q_ref, k_ref, v_ref, lq1_ref, lk1_ref, lq2_ref, lk2_ref, sw_ref, o_ref,
                       vt_scr, q2_scr, acc_scr, m_scr, l_scr, s_scr, *, t, tk, lambda_init):
    s = k_ref.shape[0]
    per = tk // t
    for c in range(s // tk):
        for r in range(per):
            rows = slice(c * tk + r * t, c * tk + (r + 1) * t)
            vt_scr[c, :, r * t:(r + 1) * t] = v_ref[rows, :].astype(F32).T.astype(BF16)

    lane = lax.broadcasted_iota(I32, (1, LANES), 1)
    lam = (jnp.exp(jnp.sum(lq1_ref[...] * lk1_ref[...], axis=1, keepdims=True))
           - jnp.exp(jnp.sum(lq2_ref[...] * lk2_ref[...], axis=1, keepdims=True))
           + lambda_init)

    def accumulate(r, st_parts, vt):
        m_prev = m_scr[r]
        m_new = m_prev
        for st in st_parts:
            m_new = jnp.maximum(m_new, jnp.max(st, axis=0, keepdims=True))
        ps = [jnp.exp2(st - m_new) for st in st_parts]
        alpha = jnp.exp2(m_prev - m_new)
        l_new = alpha * l_scr[r]
        for p in ps:
            l_new = l_new + jnp.sum(p, axis=0, keepdims=True)
        l_scr[r] = l_new
        pb = [p.astype(BF16) for p in ps]
        pb = pb[0] if len(pb) == 1 else jnp.concatenate(pb, axis=0)
        acc_scr[r] = alpha * acc_scr[r] + jnp.dot(vt, pb, preferred_element_type=F32)
        m_scr[r] = m_new

    def interleaved(score_fn, softmax_fn, next_first_scores):
        for r in range(per):
            if r + 1 < per:
                score_fn(r + 1)
            else:
                next_first_scores()
            softmax_fn(r)

    def group(gi, carry):
        row0 = pl.multiple_of(gi * tk, tk)
        for r in range(per):
            q = q_ref[pl.ds(row0 + r * t, t), :]
            zero = jnp.zeros_like(q)
            q2_scr[r] = jnp.concatenate(
                [jnp.where(lane < D_HEAD, q, zero), jnp.where(lane >= D_HEAD, q, zero)], axis=0)
        acc_scr[...] = jnp.zeros(acc_scr.shape, F32)
        m_scr[...] = jnp.full(m_scr.shape, NEG, F32)
        l_scr[...] = jnp.zeros(l_scr.shape, F32)

        def diag_scores(r):
            rows = (r + 1) * t
            k = k_ref[pl.ds(row0, rows), :]
            s_scr[r, 0:rows, :] = lax.dot_general(k, q2_scr[r], NT_DIMS, preferred_element_type=F32)

        def diag_softmax(r):
            rows = (r + 1) * t
            key = lax.broadcasted_iota(I32, (t, 2 * t), 0)
            col = lax.broadcasted_iota(I32, (t, 2 * t), 1)
            diag = jnp.where(key <= jnp.where(col >= t, col - t, col), s_scr[r, r * t:rows, :], NEG)
            parts = [diag] if r == 0 else [s_scr[r, 0:r * t, :], diag]
            accumulate(r, parts, vt_scr[gi, :, 0:rows])

        def full_scores(j, r):
            k = k_ref[pl.ds(pl.multiple_of(j * tk, tk), tk), :]
            s_scr[r] = lax.dot_general(k, q2_scr[r], NT_DIMS, preferred_element_type=F32)

        diag_scores(0)
        interleaved(diag_scores, diag_softmax, functools.partial(full_scores, 0, 0))

        def key_block(j, c2):
            interleaved(functools.partial(full_scores, j), lambda r: accumulate(r, [s_scr[r]], vt_scr[j]),
                        functools.partial(full_scores, j + 1, 0))
            return c2

        lax.fori_loop(0, gi, key_block, 0)

        for r in range(per):
            on = acc_scr[r] / l_scr[r]
            o = on[:, :t] - lam * on[:, t:]
            ms = jnp.mean(o * o, axis=0, keepdims=True)
            o = o * lax.rsqrt(ms + EPS) * sw_ref[...] * (1.0 - lambda_init)
            o_ref[pl.ds(row0 + r * t, t), :] = o.T.astype(BF16)
        return carry

    lax.fori_loop(0, s // tk, group, 0)


def _diff_attention(qkv, lq1, lk1, lq2, lk2, subln_col, b, s, t, tk, lambda_init):
    n = qkv.shape[0]
    per = tk // t
    small = pl.BlockSpec((1, D_HEAD), lambda bi, h: (0, 0))
    return pl.pallas_call(
        functools.partial(_diff_group_kernel, t=t, tk=tk, lambda_init=lambda_init),
        grid=(b, H_DIFF),
        in_specs=[
            pl.BlockSpec((s, LANES), lambda bi, h: (bi, h)),
            pl.BlockSpec((s, LANES), lambda bi, h: (bi, 4 + h)),
            pl.BlockSpec((s, LANES), lambda bi, h: (bi, 8 + h)),
            small, small, small, small,
            pl.BlockSpec((LANES, 1), lambda bi, h: (0, 0)),
        ],
        out_specs=pl.BlockSpec((s, LANES), lambda bi, h: (bi, h)),
        out_shape=jax.ShapeDtypeStruct((n, H_DIFF * LANES), BF16),
        scratch_shapes=[
            pltpu.VMEM((s // tk, LANES, tk), BF16),
            pltpu.VMEM((per, 2 * t, LANES), BF16),
            pltpu.VMEM((per, LANES, 2 * t), F32),
            pltpu.VMEM((per, 1, 2 * t), F32),
            pltpu.VMEM((per, 1, 2 * t), F32),
            pltpu.VMEM((per, tk, 2 * t), F32),
        ],
        compiler_params=pltpu.CompilerParams(
            dimension_semantics=("arbitrary", "arbitrary"), vmem_limit_bytes=VMEM_LIMIT),
        name="diffattn",
    )(qkv, qkv, qkv, lq1, lk1, lq2, lk2, subln_col)


def _swa_kernel(sink_ref, q_ref, kc_ref, kp_ref, vc_ref, vp_ref, o_ref, kbuf, vbuf, *, tq):
    i = pl.program_id(1)
    w = WINDOW
    kbuf[0:w, :] = kp_ref[...]
    kbuf[w:w + tq, :] = kc_ref[...]
    vbuf[0:w, :] = vp_ref[...]
    vbuf[w:w + tq, :] = vc_ref[...]
    lane = lax.broadcasted_iota(I32, (1, LANES), 1)
    lo = lane < D_HEAD
    qi = lax.broadcasted_iota(I32, (w, 2 * w), 0)
    ki = lax.broadcasted_iota(I32, (w, 2 * w), 1)
    band = (ki > qi) & (ki <= qi + w)
    band_first = band & (ki >= jnp.where(i > 0, 0, w))
    def scores(r):
        keys = kbuf[r * w:(r + 2) * w, :]
        qs = []
        for half in range(2):
            for p in range(4):
                qg = q_ref[r * w:(r + 1) * w, p * LANES:(p + 1) * LANES]
                qs.append(jnp.where(lo if half == 0 else ~lo, qg, jnp.zeros_like(qg)))
        qstack = jnp.concatenate(qs, axis=0)
        return lax.dot_general(qstack, keys, NT_DIMS, preferred_element_type=F32)

    s_next = scores(0)
    for r in range(tq // w):
        s_all = s_next
        if r + 1 < tq // w:
            s_next = scores(r + 1)
        vals = vbuf[r * w:(r + 2) * w, :]
        valid = band_first if r == 0 else band
        ps, inv = [], []
        for hb in range(8):
            sink = sink_ref[hb] * math.log2(math.e)
            s = jnp.where(valid, s_all[hb * w:(hb + 1) * w], NEG)
            m = jnp.maximum(jnp.max(s, axis=1, keepdims=True), sink)
            e = jnp.exp2(s - m)
            inv.append(1.0 / (jnp.sum(e, axis=1, keepdims=True) + jnp.exp2(sink - m)))
            ps.append(e.astype(BF16))
        pv = jnp.dot(jnp.concatenate(ps, axis=0), vals, preferred_element_type=F32)
        for p in range(4):
            og = jnp.where(lo, pv[p * w:(p + 1) * w] * inv[p], pv[(4 + p) * w:(5 + p) * w] * inv[4 + p])
            o_ref[r * w:(r + 1) * w, p * LANES:(p + 1) * LANES] = og.astype(BF16)


def _swa_attention(qkv, sinks, b, s, tq):
    n = qkv.shape[0]
    nq = s // tq
    per = tq // WINDOW
    prev_map = lambda col: (lambda bi, i, sk: (jnp.maximum(bi * (s // WINDOW) + i * per - 1, 0), col))
    cur_map = lambda col: (lambda bi, i, sk: (bi * nq + i, col))
    return pl.pallas_call(
        functools.partial(_swa_kernel, tq=tq),
        grid_spec=pltpu.PrefetchScalarGridSpec(
            num_scalar_prefetch=1,
            grid=(b, nq),
            in_specs=[
                pl.BlockSpec((tq, 4 * LANES), lambda bi, i, sk: (bi * nq + i, 3)),
                pl.BlockSpec((tq, LANES), cur_map(16)),
                pl.BlockSpec((WINDOW, LANES), prev_map(16)),
                pl.BlockSpec((tq, LANES), cur_map(17)),
                pl.BlockSpec((WINDOW, LANES), prev_map(17)),
            ],
            out_specs=pl.BlockSpec((tq, 4 * LANES), lambda bi, i, sk: (bi * nq + i, 0)),
            scratch_shapes=[pltpu.VMEM((WINDOW + tq, LANES), BF16)] * 2,
        ),
        out_shape=jax.ShapeDtypeStruct((n, 4 * LANES), BF16),
        compiler_params=pltpu.CompilerParams(
            dimension_semantics=("arbitrary", "arbitrary"), vmem_limit_bytes=VMEM_LIMIT),
        name="swa",
    )(sinks, qkv, qkv, qkv, qkv, qkv)


def _memkv_kernel(m_ref, lnw_ref, w_ref, o_ref):
    a = _rms(m_ref[...], lnw_ref[...]).astype(BF16)
    o_ref[...] = jnp.dot(a, w_ref[...], preferred_element_type=F32).astype(BF16)


def _memkv(mem2, ln_w, wkv_bf, m_len):
    n, d = mem2.shape
    return pl.pallas_call(
        _memkv_kernel,
        grid=(n // m_len,),
        in_specs=[
            pl.BlockSpec((m_len, d), lambda i: (i, 0)),
            pl.BlockSpec((1, d), lambda i: (0, 0)),
            pl.BlockSpec((d, 2 * d), lambda i: (0, 0)),
        ],
        out_specs=pl.BlockSpec((m_len, 2 * d), lambda i: (i, 0)),
        out_shape=jax.ShapeDtypeStruct((n, 2 * d), BF16),
        compiler_params=pltpu.CompilerParams(
            dimension_semantics=("arbitrary",), vmem_limit_bytes=VMEM_LIMIT),
        name="memkv",
    )(mem2, ln_w, wkv_bf)


def _split_bf16(x):
    hi = x.astype(BF16)
    lo = (x - hi.astype(F32)).astype(BF16)
    return hi, lo


def _post_kernel(x_ref, od_ref, os_ref, wout_ref, lnc_ref, wq_ref, kv_ref, wo_ref, lnm_ref, wr_ref, br_ref,
                 h_ref, f_ref, lpos_ref, gate_ref, tcnt_ref, tbase_ref, base_scr, fprev_scr, *, tm, d):
    g = pl.program_id(0)

    @pl.when(g == 0)
    def _():
        base_scr[...] = jnp.zeros(base_scr.shape, F32)
        fprev_scr[...] = jnp.zeros(fprev_scr.shape, F32)

    mix = jnp.concatenate([od_ref[...], os_ref[...]], axis=1)
    h1 = x_ref[...] + jnp.dot(mix, wout_ref[...], preferred_element_type=F32)
    logits = _route_logits(fprev_scr[...], wr_ref, br_ref)

    c = _rms(h1, lnc_ref[...]).astype(BF16)
    dc = d // H_CROSS
    q = (jnp.dot(c, wq_ref[...], preferred_element_type=F32) * (dc ** -0.5)).astype(BF16)
    eid1, eid2 = _route_select(logits, gate_ref, tm)
    outs = []
    for hd in range(H_CROSS):
        k = kv_ref[:, hd * dc:(hd + 1) * dc]
        v = kv_ref[:, d + hd * dc:d + (hd + 1) * dc]
        s = lax.dot_general(q[:, hd * dc:(hd + 1) * dc], k, NT_DIMS, preferred_element_type=F32)
        e = jnp.exp(s - jnp.max(s, axis=1, keepdims=True))
        den = jnp.sum(e, axis=1, keepdims=True)
        outs.append((jnp.dot(e.astype(BF16), v, preferred_element_type=F32) / den).astype(BF16))
    o = jnp.concatenate(outs, axis=1)
    h2 = h1 + jnp.dot(o, wo_ref[...], preferred_element_type=F32)
    _route_positions(eid1, eid2, jnp.where(g > 0, 1.0, 0.0), lpos_ref, tcnt_ref, tbase_ref, base_scr, tm)
    h_ref[...] = h2
    f = _rms(h2, lnm_ref[...])
    f_ref[...] = f.astype(BF16)
    fprev_scr[...] = f


def _route_logits(f, wr_ref, br_ref):
    f_hi, f_lo = _split_bf16(f)
    w_hi, w_lo = _split_bf16(wr_ref[...])
    return (lax.dot_general(w_hi, f_hi, NT_DIMS, preferred_element_type=F32)
            + lax.dot_general(w_hi, f_lo, NT_DIMS, preferred_element_type=F32)
            + lax.dot_general(w_lo, f_hi, NT_DIMS, preferred_element_type=F32)) + br_ref[...]


def _route_select(lg, gate_ref, tm):
    gl = lg[0:N_GROUPS]
    gmax = jnp.max(gl, axis=0, keepdims=True)
    gidx = lax.broadcasted_iota(I32, gl.shape, 0)
    g_sel = jnp.min(jnp.where(gl == gmax, gidx, N_GROUPS), axis=0, keepdims=True)
    g_p = 1.0 / jnp.sum(jnp.exp(gl - gmax), axis=0, keepdims=True)

    e8 = jnp.zeros((E_PER_GROUP, tm), F32)
    for g in range(N_GROUPS):
        e8 = e8 + jnp.where(g_sel == g, lg[8 + g * E_PER_GROUP:8 + (g + 1) * E_PER_GROUP], 0.0)
    ex = jnp.exp(e8 - jnp.max(e8, axis=0, keepdims=True))
    ep = ex / jnp.sum(ex, axis=0, keepdims=True)
    idx8 = lax.broadcasted_iota(I32, ep.shape, 0)
    p1 = jnp.max(ep, axis=0, keepdims=True)
    i1 = jnp.min(jnp.where(ep == p1, idx8, E_PER_GROUP), axis=0, keepdims=True)
    ep2 = jnp.where(idx8 == i1, -1.0, ep)
    p2 = jnp.max(ep2, axis=0, keepdims=True)
    i2 = jnp.min(jnp.where(ep2 == p2, idx8, E_PER_GROUP), axis=0, keepdims=True)
    psum = p1 + p2
    gate_ref[0:1, :] = g_p * (p1 / psum)
    gate_ref[1:2, :] = g_p * (p2 / psum)
    return g_sel * E_PER_GROUP + i1, g_sel * E_PER_GROUP + i2


def _route_positions(eid1, eid2, live, lpos_ref, tcnt_ref, tbase_ref, base_scr, tm):
    e32 = lax.broadcasted_iota(I32, (N_EXPERTS, tm), 0)
    oh1 = (e32 == eid1).astype(F32)
    oh2 = (e32 == eid2).astype(F32)
    cnt = oh1 + oh2
    tr = lax.broadcasted_iota(I32, (tm, tm), 0)
    tc = lax.broadcasted_iota(I32, (tm, tm), 1)
    upper = jnp.where(tr < tc, 1.0, 0.0).astype(BF16)
    earlier = jnp.dot(cnt.astype(BF16), upper, preferred_element_type=F32)
    seg = jnp.sum(cnt, axis=1, keepdims=True)
    seg = jnp.floor((seg + (SUBLANES - 1)) * (1.0 / SUBLANES)) * SUBLANES * live
    seg = jnp.broadcast_to(seg, (N_EXPERTS, LANES))
    er = lax.broadcasted_iota(I32, (N_EXPERTS, N_EXPERTS), 0)
    ec = lax.broadcasted_iota(I32, (N_EXPERTS, N_EXPERTS), 1)
    lower = jnp.where(ec < er, 1.0, 0.0).astype(BF16)
    start = jnp.dot(lower, seg.astype(BF16), preferred_element_type=F32)[:, 0:1]
    where = earlier + start
    lpos_ref[0:1, :] = jnp.sum(oh1 * where, axis=0, keepdims=True).astype(I32)
    lpos_ref[1:2, :] = jnp.sum(oh2 * where, axis=0, keepdims=True).astype(I32)
    tcnt_ref[...] = seg
    tbase_ref[...] = base_scr[...]
    base_scr[...] = base_scr[...] + seg


def _post(x2, od, osw, wout_bf, lnc, wq_bf, kv, wo_bf, lnm, wr, br, b, s, tm, m_len):
    n, d = x2.shape
    nt = s // tm
    tiles = n // tm
    main = lambda g: jnp.minimum(g, tiles - 1)
    routed = lambda g: jnp.maximum(g - 1, 0)
    row = lambda g: (main(g), 0)
    const = lambda g: (0, 0)
    return pl.pallas_call(
        functools.partial(_post_kernel, tm=tm, d=d),
        grid=(tiles + 1,),
        in_specs=[
            pl.BlockSpec((tm, d), row),
            pl.BlockSpec((tm, d // 2), row),
            pl.BlockSpec((tm, d // 2), row),
            pl.BlockSpec((d, d), const),
            pl.BlockSpec((1, d), const),
            pl.BlockSpec((d, d), const),
            pl.BlockSpec((m_len, 2 * d), lambda g: (main(g) // nt, 0)),
            pl.BlockSpec((d, d), const),
            pl.BlockSpec((1, d), const),
            pl.BlockSpec((8 + N_EXPERTS, d), const),
            pl.BlockSpec((8 + N_EXPERTS, 1), const),
        ],
        out_specs=[
            pl.BlockSpec((tm, d), row),
            pl.BlockSpec((tm, d), row),
            pl.BlockSpec((2, tm), lambda g: (0, routed(g))),
            pl.BlockSpec((2, tm), lambda g: (0, routed(g))),
            pl.BlockSpec((None, N_EXPERTS, LANES), lambda g: (routed(g), 0, 0)),
            pl.BlockSpec((None, N_EXPERTS, LANES), lambda g: (routed(g), 0, 0)),
        ],
        out_shape=[
            jax.ShapeDtypeStruct((n, d), F32),
            jax.ShapeDtypeStruct((n, d), BF16),
            jax.ShapeDtypeStruct((2, n), I32),
            jax.ShapeDtypeStruct((2, n), F32),
            jax.ShapeDtypeStruct((n // tm, N_EXPERTS, LANES), F32),
            jax.ShapeDtypeStruct((n // tm, N_EXPERTS, LANES), F32),
        ],
        scratch_shapes=[pltpu.VMEM((N_EXPERTS, LANES), F32), pltpu.VMEM((tm, d), F32)],
        compiler_params=pltpu.CompilerParams(
            dimension_semantics=("arbitrary",), vmem_limit_bytes=VMEM_LIMIT),
        name="post",
    )(x2, od, osw, wout_bf, lnc, wq_bf, kv, wo_bf, lnm, wr, br)


def _local_rows(tm):
    return 2 * tm + SUBLANES * N_EXPERTS


def _segment_copies(tile, tcnt_ref, tbase_ref, pstart_ref, make, act, max_len):
    sizes = [SUBLANES << k for k in reversed(range((max_len // SUBLANES).bit_length()))]

    if act == "wait":
        total = lax.fori_loop(0, N_EXPERTS, lambda e, acc: acc + tcnt_ref[tile * N_EXPERTS + e], 0)
        for size in sizes:
            @pl.when((total & size) != 0)
            def _():
                make(0, 0, size).wait()

        return total

    def pieces(ln, local0, glob0, some_sizes):
        for size in some_sizes:
            @pl.when((ln & size) != 0)
            def _():
                off = (ln // (2 * size)) * (2 * size)
                make(pl.multiple_of(local0 + off, SUBLANES), pl.multiple_of(glob0 + off, SUBLANES), size).start()

    big = [s for s in sizes if s >= LONG_SEGMENT_ROWS]
    small = [s for s in sizes if s < LONG_SEGMENT_ROWS]

    def per_expert(e, local0):
        ln = tcnt_ref[tile * N_EXPERTS + e]
        glob0 = pstart_ref[e] + tbase_ref[tile * N_EXPERTS + e]

        @pl.when(ln >= LONG_SEGMENT_ROWS)
        def _():
            pieces(ln, local0, glob0, big)

        pieces(ln, local0, glob0, small)
        return local0 + ln

    return lax.fori_loop(0, N_EXPERTS, per_expert, 0)


def _dispatch_kernel(tcnt_ref, tbase_ref, tot_ref, f_ref, lpos_ref, gate_ref, xs_hbm, pstart_ref, blk_ref, nact_ref,
                     loc_scr, zero_scr, sem, zsem, *, tm, bm, nblk):
    i = pl.program_id(0)
    nt = pl.num_programs(0)
    slot = i % 2

    @pl.when(i == 0)
    def _():
        def per_expert(e, blk0):
            nb = (tot_ref[e] + (bm - 1)) // bm
            pstart_ref[e] = blk0 * bm

            def fill(j, carry):
                blk_ref[j] = e
                return carry

            lax.fori_loop(blk0, blk0 + nb, fill, 0)
            return blk0 + nb

        nact = lax.fori_loop(0, N_EXPERTS, per_expert, 0)
        nact_ref[0] = nact

        def tail(j, carry):
            blk_ref[j] = N_EXPERTS - 1
            return carry

        lax.fori_loop(nact, nblk, tail, 0)

        zero_scr[...] = jnp.zeros(zero_scr.shape, U32)

        def zero_fill(act):
            def do(copy):
                if act == "start":
                    copy.start()
                else:
                    copy.wait()

            def per_expert_pad(e, carry):
                tot = tot_ref[e]
                pad = (tot + (bm - 1)) // bm * bm - tot
                row0 = pstart_ref[e] + tot
                for k in reversed(range((bm // SUBLANES).bit_length())):
                    size = SUBLANES << k

                    @pl.when((pad & size) != 0)
                    def _():
                        row = pl.multiple_of(row0 + (pad // (2 * size)) * (2 * size), SUBLANES)
                        do(pltpu.make_async_copy(zero_scr.at[pl.ds(0, size)], xs_hbm.at[pl.ds(row, size)], zsem))

                return carry

            lax.fori_loop(0, N_EXPERTS, per_expert_pad, 0)

            def per_unused_block(j, carry):
                do(pltpu.make_async_copy(zero_scr, xs_hbm.at[pl.ds(pl.multiple_of(j * bm, bm), bm)], zsem))
                return carry

            lax.fori_loop(nact, nblk, per_unused_block, 0)

        zero_fill("start")
        zero_fill("wait")

    def copies(tile, sl, act):
        def make(lrow, grow, size):
            return pltpu.make_async_copy(loc_scr.at[sl, pl.ds(lrow, size)], xs_hbm.at[pl.ds(grow, size)], sem.at[sl])

        _segment_copies(tile, tcnt_ref, tbase_ref, pstart_ref, make, act, _local_rows(tm))

    @pl.when(i >= 2)
    def _():
        copies(i - 2, slot, "wait")

    half = f_ref.shape[1] // 2
    chunk = math.gcd(_local_rows(tm), 256)
    for r0 in range(0, _local_rows(tm), chunk):
        pos = lax.broadcasted_iota(I32, (chunk, tm), 0) + r0
        first = pos == lpos_ref[0:1, :]
        second = pos == lpos_ref[1:2, :]
        onehot = jnp.where(first, 1.0, jnp.where(second, 1.0, 0.0)).astype(BF16)
        loc_scr[slot, r0:r0 + chunk, 0:half] = _pack_pairs(
            jnp.dot(onehot, f_ref[...], preferred_element_type=F32))
        gate = jnp.sum(jnp.where(first, gate_ref[0:1, :], jnp.where(second, gate_ref[1:2, :], 0.0)),
                       axis=1, keepdims=True)
        loc_scr[slot, r0:r0 + chunk, half:half + LANES] = lax.bitcast_convert_type(
            jnp.broadcast_to(gate, (chunk, LANES)), U32)
    copies(i, slot, "start")

    @pl.when(i == nt - 1)
    def _():
        @pl.when(i >= 1)
        def _():
            copies(i - 1, 1 - slot, "wait")

        copies(i, slot, "wait")


def _dispatch(tcnt_i, tbase_i, tot_i, f, lpos, gates, tm, bm, nblk):
    n, d = f.shape
    width = d // 2 + LANES
    smem = pl.BlockSpec(memory_space=pltpu.SMEM)
    hbm = pl.BlockSpec(memory_space=pl.ANY)
    return pl.pallas_call(
        functools.partial(_dispatch_kernel, tm=tm, bm=bm, nblk=nblk),
        grid_spec=pltpu.PrefetchScalarGridSpec(
            num_scalar_prefetch=3,
            grid=(n // tm,),
            in_specs=[
                pl.BlockSpec((tm, d), lambda i, *_: (i, 0)),
                pl.BlockSpec((2, tm), lambda i, *_: (0, i)),
                pl.BlockSpec((2, tm), lambda i, *_: (0, i)),
            ],
            out_specs=[hbm, smem, smem, smem],
            scratch_shapes=[
                pltpu.VMEM((2, _local_rows(tm), width), U32),
                pltpu.VMEM((bm, width), U32),
                pltpu.SemaphoreType.DMA((2,)),
                pltpu.SemaphoreType.DMA,
            ],
        ),
        out_shape=[
            jax.ShapeDtypeStruct((nblk * bm, width), U32),
            jax.ShapeDtypeStruct((N_EXPERTS,), I32),
            jax.ShapeDtypeStruct((nblk,), I32),
            jax.ShapeDtypeStruct((1,), I32),
        ],
        compiler_params=pltpu.CompilerParams(
            dimension_semantics=("arbitrary",), vmem_limit_bytes=VMEM_LIMIT),
        name="dispatch",
    )(tcnt_i, tbase_i, tot_i, f, lpos, gates)


def _expert_kernel(blk_ref, nact_ref, x_ref, wg_ref, wu_ref, wd_ref, y_ref, wg_bf, wu_bf, wd_bf):
    j = pl.program_id(0)
    active = j < nact_ref[0]

    @pl.when(active & ((j == 0) | (blk_ref[j] != blk_ref[jnp.maximum(j - 1, 0)])))
    def _():
        wg_bf[...] = wg_ref[...].astype(BF16)
        wu_bf[...] = wu_ref[...].astype(BF16)
        wd_bf[...] = wd_ref[...].astype(BF16)

    @pl.when(active)
    def _():
        half = y_ref.shape[1]
        x = _unpack_pairs(x_ref[:, 0:half])
        gate = lax.bitcast_convert_type(x_ref[:, half:half + 1], F32)
        g = jnp.dot(x, wg_bf[...], preferred_element_type=F32)
        u = jnp.dot(x, wu_bf[...], preferred_element_type=F32)
        hdn = (g * jax.nn.sigmoid(g) * u).astype(BF16)
        y = jnp.dot(hdn, wd_bf[...], preferred_element_type=F32) * gate
        y_ref[...] = _pack_pairs(y.astype(BF16).astype(F32))

    @pl.when(j >= nact_ref[0])
    def _():
        y_ref[...] = jnp.zeros(y_ref.shape, y_ref.dtype)


def _experts(blk_e, nact, xs, wg, wu, wd, bm):
    p, width = xs.shape
    d, dff = wg.shape[-2:]
    dh = d // 2
    nblk = p // bm
    rowmap = lambda j, blk, na: (jnp.minimum(j, na[0] - 1), 0)
    wmap = lambda j, blk, na: (blk[jnp.minimum(j, na[0] - 1)], 0, 0)
    return pl.pallas_call(
        _expert_kernel,
        grid_spec=pltpu.PrefetchScalarGridSpec(
            num_scalar_prefetch=2,
            grid=(nblk,),
            in_specs=[
                pl.BlockSpec((bm, width), rowmap),
                pl.BlockSpec((None, d, dff), wmap),
                pl.BlockSpec((None, d, dff), wmap),
                pl.BlockSpec((None, dff, d), wmap),
            ],
            out_specs=pl.BlockSpec((bm, dh), lambda j, blk, na: (j, 0)),
            scratch_shapes=[pltpu.VMEM((d, dff), BF16), pltpu.VMEM((d, dff), BF16), pltpu.VMEM((dff, d), BF16)],
        ),
        out_shape=jax.ShapeDtypeStruct((p, dh), U32),
        compiler_params=pltpu.CompilerParams(
            dimension_semantics=("arbitrary",), vmem_limit_bytes=VMEM_LIMIT),
        name="experts",
    )(blk_e, nact, xs, wg, wu, wd)


def _combine_kernel(tcnt_ref, tbase_ref, pstart_ref, h_ref, lpos_ref, lnf_ref, ys_hbm, o_ref,
                    loc_scr, sem, *, tm):
    i = pl.program_id(0)
    nt = pl.num_programs(0)
    slot = i % 2

    def copies(tile, sl, act):
        def make(lrow, grow, size):
            return pltpu.make_async_copy(ys_hbm.at[pl.ds(grow, size)], loc_scr.at[sl, pl.ds(lrow, size)], sem.at[sl])

        return _segment_copies(tile, tcnt_ref, tbase_ref, pstart_ref, make, act, _local_rows(tm))

    @pl.when(i == 0)
    def _():
        copies(0, 0, "start")

    @pl.when(i + 1 < nt)
    def _():
        copies(i + 1, 1 - slot, "start")

    used = copies(i, slot, "wait")
    row = lax.broadcasted_iota(I32, (_local_rows(tm), 1), 0)
    ys = _unpack_pairs(jnp.where(row < used, loc_scr[slot], U32(0)))

    r8 = lax.broadcasted_iota(I32, (8, tm), 0)
    lp = lpos_ref[...].astype(F32)
    top = jnp.where(r8 == 0, lp[0:1], jnp.where(r8 == 1, lp[1:2], 0.0))
    cols = jnp.concatenate([top, jnp.zeros((LANES - 8, tm), F32)], axis=0).T.astype(I32)
    pos = lax.broadcasted_iota(I32, (tm, _local_rows(tm)), 1)
    sel = jnp.where(pos == cols[:, 0:1], 1.0, jnp.where(pos == cols[:, 1:2], 1.0, 0.0)).astype(BF16)
    o_ref[...] = _rms(h_ref[...] + jnp.dot(sel, ys, preferred_element_type=F32), lnf_ref[...])


def _combine(tcnt_i, tbase_i, pstart, h2, lpos, lnf, ys, tm):
    n, d = h2.shape
    return pl.pallas_call(
        functools.partial(_combine_kernel, tm=tm),
        grid_spec=pltpu.PrefetchScalarGridSpec(
            num_scalar_prefetch=3,
            grid=(n // tm,),
            in_specs=[
                pl.BlockSpec((tm, d), lambda i, *_: (i, 0)),
                pl.BlockSpec((2, tm), lambda i, *_: (0, i)),
                pl.BlockSpec((1, d), lambda i, *_: (0, 0)),
                pl.BlockSpec(memory_space=pl.ANY),
            ],
            out_specs=pl.BlockSpec((tm, d), lambda i, *_: (i, 0)),
            scratch_shapes=[pltpu.VMEM((2, _local_rows(tm), d // 2), U32), pltpu.SemaphoreType.DMA((2,))],
        ),
        out_shape=jax.ShapeDtypeStruct((n, d), F32),
        compiler_params=pltpu.CompilerParams(
            dimension_semantics=("arbitrary",), vmem_limit_bytes=VMEM_LIMIT),
        name="combine",
    )(tcnt_i, tbase_i, pstart, h2, lpos, lnf, ys)


def _swa_head_perm():
    cols = []
    for p in range(4):
        for half in range(2):
            head = half * 4 + p
            cols.extend(range(head * D_HEAD, (head + 1) * D_HEAD))
    return jnp.asarray(cols, dtype=I32)


def kernel(x, mem, positions, ln_mix_w, w_in, lambda_q1, lambda_k1, lambda_q2, lambda_k2, subln_w, sinks,
           w_out, ln_cross_w, ln_mem_w, wq_cross, wkv_cross, wo_cross, ln_moe_w, w_group, b_group,
           w_expert, b_expert, w_gate, w_up, w_down, ln_final_w):
    b, s, d = x.shape
    m_len = mem.shape[1]
    n = b * s
    assert w_in.shape[0] == 1 and d == 1024 and n <= 65536
    lambda_init = 0.8 - 0.6 * math.exp(-0.3 * 0)

    tm = min(TOKEN_TILE, s)
    tw = min(WIDE_TOKEN_TILE, s)
    t_attn = ATTN_QUERY_TILE
    tk_attn = ATTN_KEY_BLOCK
    bm = EXPERT_BLOCK_ROWS

    x2 = x.reshape(n, d)
    half = D_HEAD // 2
    per_row = LANES // half
    inv_freq = jnp.exp(-math.log(ROPE_THETA) * jnp.arange(0, D_HEAD, 2, dtype=F32) / D_HEAD)
    inv128 = jnp.tile(inv_freq, per_row).reshape(1, LANES)
    pos_tiles = positions.reshape(n // tw, per_row, tw // per_row).astype(I32).transpose(0, 2, 1)
    pos_rep = jnp.repeat(pos_tiles.reshape(n // per_row, per_row), half, axis=1)
    cos_t, sin_t = _rope_tables(pos_rep, inv128, min(1024, n // per_row))

    perm = _swa_head_perm()
    sq0 = 3 * 512
    w_in_l = w_in[0]
    w_in_p = jnp.concatenate([w_in_l[:, :sq0], w_in_l[:, sq0:sq0 + 512][:, perm], w_in_l[:, sq0 + 512:]], axis=1)
    w_out_l = w_out[0]
    w_out_p = jnp.concatenate([w_out_l[:512], w_out_l[512:][perm]], axis=0)
    sinks_p = sinks[0].reshape(2, 4).reshape(-1)

    qkv = _inproj(x2, ln_mix_w[0].reshape(1, d), w_in_p.astype(BF16), cos_t, sin_t, tw)
    o_diff = _diff_attention(qkv, lambda_q1[0].reshape(1, -1), lambda_k1[0].reshape(1, -1),
                             lambda_q2[0].reshape(1, -1), lambda_k2[0].reshape(1, -1),
                             subln_w[0].reshape(-1, 1), b, s, t_attn, min(tk_attn, s), lambda_init)
    o_swa = _swa_attention(qkv, sinks_p.astype(F32), b, s, tm)
    kv = _memkv(mem.reshape(b * m_len, d), ln_mem_w[0].reshape(1, d), wkv_cross[0].astype(BF16), m_len)

    wr = jnp.concatenate([w_group[0].T, jnp.zeros((8 - N_GROUPS, d), F32), w_expert[0].T], axis=0)
    br = jnp.concatenate([b_group[0], jnp.zeros((8 - N_GROUPS,), F32), b_expert[0]]).reshape(-1, 1)
    h2, f, lpos, gates, tcnt, tbase = _post(
        x2, o_diff, o_swa, w_out_p.astype(BF16), ln_cross_w[0].reshape(1, d), wq_cross[0].astype(BF16), kv,
        wo_cross[0].astype(BF16), ln_moe_w[0].reshape(1, d), wr, br, b, s, tm, m_len)

    tcnt_i = tcnt[:, :, 0].astype(I32).reshape(-1)
    tbase_i = tbase[:, :, 0].astype(I32).reshape(-1)
    tot_i = tbase_i[-N_EXPERTS:] + tcnt_i[-N_EXPERTS:]
    rows_max = 2 * n + (n // tm) * N_EXPERTS * (SUBLANES - 1)
    nblk = (rows_max + N_EXPERTS * (bm - 1) + bm - 1) // bm
    xs, pstart, blk_e, nact = _dispatch(tcnt_i, tbase_i, tot_i, f, lpos, gates, tm, bm, nblk)
    ys = _experts(blk_e, nact, xs, w_gate[0], w_up[0], w_down[0], bm)
    out = _combine(tcnt_i, tbase_i, pstart, h2, lpos, ln_final_w.reshape(1, d), ys, tm)
    return out.reshape(b, s, d)
```

```python
import functools
import math

import jax
import jax.numpy as jnp
from jax import lax
from jax.experimental import pallas as pl
from jax.experimental.pallas import tpu as pltpu

F32 = jnp.float32
BF16 = jnp.bfloat16
I32 = jnp.int32

D_HEAD = 64
ROPE_THETA = 10000.0
H_DIFF = 4
N_Q_SWA = 8
N_KV_SWA = 2
WINDOW = 128
H_CROSS = 4
N_GROUPS = 4
E_PER_GROUP = 8
N_EXPERTS = N_GROUPS * E_PER_GROUP
EPS = 1e-6
LANES = 128
SUBLANES = 8
LONG_SEGMENT_ROWS = 128
IN_W = 2304
NEG = -1e30

VMEM_LIMIT = 56 * 1024 * 1024

TOKEN_TILE = 512
WIDE_TOKEN_TILE = 1024
ATTN_QUERY_TILE = 256
ATTN_KEY_BLOCK = 1024
EXPERT_BLOCK_ROWS = 512

NT_DIMS = (((1,), (1,)), ((), ()))


U32 = jnp.uint32


def _pack_pairs(x):
    h = x.shape[1] // 2
    hi = lax.bitcast_convert_type(x[:, :h], U32)
    lo = lax.bitcast_convert_type(x[:, h:], U32)
    return hi | (lo >> 16)


def _unpack_pairs(p):
    a = lax.bitcast_convert_type(p & U32(0xFFFF0000), F32)
    b = lax.bitcast_convert_type(p << 16, F32)
    return jnp.concatenate([a, b], axis=1).astype(BF16)


def _rms(x, w):
    ms = jnp.mean(x * x, axis=-1, keepdims=True)
    return x * lax.rsqrt(ms + EPS) * w


def _rope_table_kernel(pos_ref, inv_ref, cos_ref, sin_ref):
    ang = pos_ref[...].astype(F32) * inv_ref[...]
    cos_ref[...] = jnp.cos(ang)
    sin_ref[...] = jnp.sin(ang)


def _rope_tables(pos_rep, inv128, rows):
    n4 = pos_rep.shape[0]
    spec = pl.BlockSpec((rows, LANES), lambda i: (i, 0))
    return pl.pallas_call(
        _rope_table_kernel,
        grid=(n4 // rows,),
        in_specs=[spec, pl.BlockSpec((1, LANES), lambda i: (0, 0))],
        out_specs=[spec, spec],
        out_shape=[jax.ShapeDtypeStruct((n4, LANES), F32)] * 2,
        compiler_params=pltpu.CompilerParams(dimension_semantics=("arbitrary",)),
        name="ropetab",
    )(pos_rep, inv128)


def _inproj_kernel(x_ref, lnw_ref, w_ref, cos_ref, sin_ref, o_ref):
    a = _rms(x_ref[...], lnw_ref[...]).astype(BF16)
    half = D_HEAD // 2
    reps = LANES // half

    def expand(tab_ref):
        parts = [jnp.concatenate([tab_ref[:, j * half:(j + 1) * half]] * reps, axis=1) for j in range(reps)]
        return jnp.concatenate(parts, axis=0)

    cos = expand(cos_ref)
    sin = expand(sin_ref)
    lane = lax.broadcasted_iota(I32, (1, LANES), 1)
    first = (lane % D_HEAD) < (D_HEAD // 2)
    sin_signed = jnp.where(first, -sin, sin)
    n_chunks = IN_W // 256
    value_chunks = (4, 5)
    for c in [c for c in range(n_chunks) if c not in value_chunks] + list(value_chunks):
        p = jnp.dot(a, w_ref[:, c * 256:(c + 1) * 256], preferred_element_type=F32)
        for hh in range(2):
            g = c * 2 + hh
            xg = p[:, hh * LANES:(hh + 1) * LANES]
            is_v = (8 <= g < 12) or g == 17
            if not is_v:
                partner = jnp.where(first, pltpu.roll(xg, 96, 1), pltpu.roll(xg, 32, 1))
                xg = xg * cos + partner * sin_signed
                if g < 4 or 12 <= g < 16:
                    xg = xg * (D_HEAD ** -0.5 * math.log2(math.e))
            o_ref[:, g * LANES:(g + 1) * LANES] = xg.astype(BF16)


def _inproj(x2, ln_w, w_in_bf, cos, sin, tm):
    n, d = x2.shape
    dense_rows = tm // (LANES // (D_HEAD // 2))
    return pl.pallas_call(
        _inproj_kernel,
        grid=(n // tm,),
        in_specs=[
            pl.BlockSpec((tm, d), lambda i: (i, 0)),
            pl.BlockSpec((1, d), lambda i: (0, 0)),
            pl.BlockSpec((d, IN_W), lambda i: (0, 0)),
            pl.BlockSpec((dense_rows, LANES), lambda i: (i, 0)),
            pl.BlockSpec((dense_rows, LANES), lambda i: (i, 0)),
        ],
        out_specs=pl.BlockSpec((tm, IN_W), lambda i: (i, 0)),
        out_shape=jax.ShapeDtypeStruct((n, IN_W), BF16),
        compiler_params=pltpu.CompilerParams(
            dimension_semantics=("arbitrary",), vmem_limit_bytes=VMEM_LIMIT),
        name="inproj",
    )(x2, ln_w, w_in_bf, cos, sin)


def _diff_group_kernel(q_ref, k_ref, v_ref, lq1_ref, lk1_ref, lq2_ref, lk2_ref, sw_ref, o_ref,
                       vt_scr, q2_scr, acc_scr, m_scr, l_scr, s_scr, *, t, tk, lambda_init):
    s = k_ref.shape[0]
    per = tk // t
    for c in range(s // tk):
        for r in range(per):
            rows = slice(c * tk + r * t, c * tk + (r + 1) * t)
            vt_scr[c, :, r * t:(r + 1) * t] = v_ref[rows, :].astype(F32).T.astype(BF16)

    lane = lax.broadcasted_iota(I32, (1, LANES), 1)
    lam = (jnp.exp(jnp.sum(lq1_ref[...] * lk1_ref[...], axis=1, keepdims=True))
           - jnp.exp(jnp.sum(lq2_ref[...] * lk2_ref[...], axis=1, keepdims=True))
           + lambda_init)

    def accumulate(r, st_parts, vt):
        m_prev = m_scr[r]
        m_new = m_prev
        for st in st_parts:
            m_new = jnp.maximum(m_new, jnp.max(st, axis=0, keepdims=True))
        ps = [jnp.exp2(st - m_new) for st in st_parts]
        alpha = jnp.exp2(m_prev - m_new)
        l_new = alpha * l_scr[r]
        for p in ps:
            l_new = l_new + jnp.sum(p, axis=0, keepdims=True)
        l_scr[r] = l_new
        pb = [p.astype(BF16) for p in ps]
        pb = pb[0] if len(pb) == 1 else jnp.concatenate(pb, axis=0)
        acc_scr[r] = alpha * acc_scr[r] + jnp.dot(vt, pb, preferred_element_type=F32)
        m_scr[r] = m_new

    def interleaved(score_fn, softmax_fn, next_first_scores):
        for r in range(per):
            if r + 1 < per:
                score_fn(r + 1)
            else:
                next_first_scores()
            softmax_fn(r)

    def group(gi, carry):
        row0 = pl.multiple_of(gi * tk, tk)
        for r in range(per):
            q = q_ref[pl.ds(row0 + r * t, t), :]
            zero = jnp.zeros_like(q)
            q2_scr[r] = jnp.concatenate(
                [jnp.where(lane < D_HEAD, q, zero), jnp.where(lane >= D_HEAD, q, zero)], axis=0)
        acc_scr[...] = jnp.zeros(acc_scr.shape, F32)
        m_scr[...] = jnp.full(m_scr.shape, NEG, F32)
        l_scr[...] = jnp.zeros(l_scr.shape, F32)

        def diag_scores(r):
            rows = (r + 1) * t
            k = k_ref[pl.ds(row0, rows), :]
            s_scr[r, 0:rows, :] = lax.dot_general(k, q2_scr[r], NT_DIMS, preferred_element_type=F32)

        def diag_softmax(r):
            rows = (r + 1) * t
            key = lax.broadcasted_iota(I32, (t, 2 * t), 0)
            col = lax.broadcasted_iota(I32, (t, 2 * t), 1)
            diag = jnp.where(key <= jnp.where(col >= t, col - t, col), s_scr[r, r * t:rows, :], NEG)
            parts = [diag] if r == 0 else [s_scr[r, 0:r * t, :], diag]
            accumulate(r, parts, vt_scr[gi, :, 0:rows])

        def full_scores(j, r):
            k = k_ref[pl.ds(pl.multiple_of(j * tk, tk), tk), :]
            s_scr[r] = lax.dot_general(k, q2_scr[r], NT_DIMS, preferred_element_type=F32)

        diag_scores(0)
        interleaved(diag_scores, diag_softmax, functools.partial(full_scores, 0, 0))

        def key_block(j, c2):
            interleaved(functools.partial(full_scores, j), lambda r: accumulate(r, [s_scr[r]], vt_scr[j]),
                        functools.partial(full_scores, j + 1, 0))
            return c2

        lax.fori_loop(0, gi, key_block, 0)

        for r in range(per):
            on = acc_scr[r] / l_scr[r]
            o = on[:, :t] - lam * on[:, t:]
            ms = jnp.mean(o * o, axis=0, keepdims=True)
            o = o * lax.rsqrt(ms + EPS) * sw_ref[...] * (1.0 - lambda_init)
            o_ref[pl.ds(row0 + r * t, t), :] = o.T.astype(BF16)
        return carry

    lax.fori_loop(0, s // tk, group, 0)


def _diff_attention(qkv, lq1, lk1, lq2, lk2, subln_col, b, s, t, tk, lambda_init):
    n = qkv.shape[0]
    per = tk // t
    small = pl.BlockSpec((1, D_HEAD), lambda bi, h: (0, 0))
    return pl.pallas_call(
        functools.partial(_diff_group_kernel, t=t, tk=tk, lambda_init=lambda_init),
        grid=(b, H_DIFF),
        in_specs=[
            pl.BlockSpec((s, LANES), lambda bi, h: (bi, h)),
            pl.BlockSpec((s, LANES), lambda bi, h: (bi, 4 + h)),
            pl.BlockSpec((s, LANES), lambda bi, h: (bi, 8 + h)),
            small, small, small, small,
            pl.BlockSpec((LANES, 1), lambda bi, h: (0, 0)),
        ],
        out_specs=pl.BlockSpec((s, LANES), lambda bi, h: (bi, h)),
        out_shape=jax.ShapeDtypeStruct((n, H_DIFF * LANES), BF16),
        scratch_shapes=[
            pltpu.VMEM((s // tk, LANES, tk), BF16),
            pltpu.VMEM((per, 2 * t, LANES), BF16),
            pltpu.VMEM((per, LANES, 2 * t), F32),
            pltpu.VMEM((per, 1, 2 * t), F32),
            pltpu.VMEM((per, 1, 2 * t), F32),
            pltpu.VMEM((per, tk, 2 * t), F32),
        ],
        compiler_params=pltpu.CompilerParams(
            dimension_semantics=("arbitrary", "arbitrary"), vmem_limit_bytes=VMEM_LIMIT),
        name="diffattn",
    )(qkv, qkv, qkv, lq1, lk1, lq2, lk2, subln_col)


def _swa_kernel(sink_ref, q_ref, kc_ref, kp_ref, vc_ref, vp_ref, o_ref, kbuf, vbuf, *, tq):
    i = pl.program_id(1)
    w = WINDOW
    kbuf[0:w, :] = kp_ref[...]
    kbuf[w:w + tq, :] = kc_ref[...]
    vbuf[0:w, :] = vp_ref[...]
    vbuf[w:w + tq, :] = vc_ref[...]
    lane = lax.broadcasted_iota(I32, (1, LANES), 1)
    lo = lane < D_HEAD
    qi = lax.broadcasted_iota(I32, (w, 2 * w), 0)
    ki = lax.broadcasted_iota(I32, (w, 2 * w), 1)
    band = (ki > qi) & (ki <= qi + w)
    band_first = band & (ki >= jnp.where(i > 0, 0, w))
    def scores(r):
        keys = kbuf[r * w:(r + 2) * w, :]
        qs = []
        for half in range(2):
            for p in range(4):
                qg = q_ref[r * w:(r + 1) * w, p * LANES:(p + 1) * LANES]
                qs.append(jnp.where(lo if half == 0 else ~lo, qg, jnp.zeros_like(qg)))
        qstack = jnp.concatenate(qs, axis=0)
        return lax.dot_general(qstack, keys, NT_DIMS, preferred_element_type=F32)

    s_next = scores(0)
    for r in range(tq // w):
        s_all = s_next
        if r + 1 < tq // w:
            s_next = scores(r + 1)
        vals = vbuf[r * w:(r + 2) * w, :]
        valid = band_first if r == 0 else band
        ps, inv = [], []
        for hb in range(8):
            sink = sink_ref[hb] * math.log2(math.e)
            s = jnp.where(valid, s_all[hb * w:(hb + 1) * w], NEG)
            m = jnp.maximum(jnp.max(s, axis=1, keepdims=True), sink)
            e = jnp.exp2(s - m)
            inv.append(1.0 / (jnp.sum(e, axis=1, keepdims=True) + jnp.exp2(sink - m)))
            ps.append(e.astype(BF16))
        pv = jnp.dot(jnp.concatenate(ps, axis=0), vals, preferred_element_type=F32)
        for p in range(4):
            og = jnp.where(lo, pv[p * w:(p + 1) * w] * inv[p], pv[(4 + p) * w:(5 + p) * w] * inv[4 + p])
            o_ref[r * w:(r + 1) * w, p * LANES:(p + 1) * LANES] = og.astype(BF16)


def _swa_attention(qkv, sinks, b, s, tq):
    n = qkv.shape[0]
    nq = s // tq
    per = tq // WINDOW
    prev_map = lambda col: (lambda bi, i, sk: (jnp.maximum(bi * (s // WINDOW) + i * per - 1, 0), col))
    cur_map = lambda col: (lambda bi, i, sk: (bi * nq + i, col))
    return pl.pallas_call(
        functools.partial(_swa_kernel, tq=tq),
        grid_spec=pltpu.PrefetchScalarGridSpec(
            num_scalar_prefetch=1,
            grid=(b, nq),
            in_specs=[
                pl.BlockSpec((tq, 4 * LANES), lambda bi, i, sk: (bi * nq + i, 3)),
                pl.BlockSpec((tq, LANES), cur_map(16)),
                pl.BlockSpec((WINDOW, LANES), prev_map(16)),
                pl.BlockSpec((tq, LANES), cur_map(17)),
                pl.BlockSpec((WINDOW, LANES), prev_map(17)),
            ],
            out_specs=pl.BlockSpec((tq, 4 * LANES), lambda bi, i, sk: (bi * nq + i, 0)),
            scratch_shapes=[pltpu.VMEM((WINDOW + tq, LANES), BF16)] * 2,
        ),
        out_shape=jax.ShapeDtypeStruct((n, 4 * LANES), BF16),
        compiler_params=pltpu.CompilerParams(
            dimension_semantics=("arbitrary", "arbitrary"), vmem_limit_bytes=VMEM_LIMIT),
        name="swa",
    )(sinks, qkv, qkv, qkv, qkv, qkv)


def _memkv_kernel(m_ref, lnw_ref, w_ref, o_ref):
    a = _rms(m_ref[...], lnw_ref[...]).astype(BF16)
    o_ref[...] = jnp.dot(a, w_ref[...], preferred_element_type=F32).astype(BF16)


def _memkv(mem2, ln_w, wkv_bf, m_len):
    n, d = mem2.shape
    return pl.pallas_call(
        _memkv_kernel,
        grid=(n // m_len,),
        in_specs=[
            pl.BlockSpec((m_len, d), lambda i: (i, 0)),
            pl.BlockSpec((1, d), lambda i: (0, 0)),
            pl.BlockSpec((d, 2 * d), lambda i: (0, 0)),
        ],
        out_specs=pl.BlockSpec((m_len, 2 * d), lambda i: (i, 0)),
        out_shape=jax.ShapeDtypeStruct((n, 2 * d), BF16),
        compiler_params=pltpu.CompilerParams(
            dimension_semantics=("arbitrary",), vmem_limit_bytes=VMEM_LIMIT),
        name="memkv",
    )(mem2, ln_w, wkv_bf)


def _split_bf16(x):
    hi = x.astype(BF16)
    lo = (x - hi.astype(F32)).astype(BF16)
    return hi, lo


def _post_kernel(x_ref, od_ref, os_ref, wout_ref, lnc_ref, wq_ref, kv_ref, wo_ref, lnm_ref, wr_ref, br_ref,
                 h_ref, f_ref, lpos_ref, gate_ref, tcnt_ref, tbase_ref, base_scr, fprev_scr, upper_scr, *, tm, d):
    g = pl.program_id(0)

    @pl.when(g == 0)
    def _():
        base_scr[...] = jnp.zeros(base_scr.shape, F32)
        fprev_scr[...] = jnp.zeros(fprev_scr.shape, F32)
        tr = lax.broadcasted_iota(I32, (tm, tm), 0)
        tc = lax.broadcasted_iota(I32, (tm, tm), 1)
        upper_scr[...] = jnp.where(tr < tc, 1.0, 0.0).astype(BF16)

    mix = jnp.concatenate([od_ref[...], os_ref[...]], axis=1)
    h1 = x_ref[...] + jnp.dot(mix, wout_ref[...], preferred_element_type=F32)
    logits = _route_logits(fprev_scr[...], wr_ref, br_ref)

    c = _rms(h1, lnc_ref[...]).astype(BF16)
    dc = d // H_CROSS
    q = (jnp.dot(c, wq_ref[...], preferred_element_type=F32) * (dc ** -0.5)).astype(BF16)
    eid1, eid2 = _route_select(logits, gate_ref, tm)
    outs = []
    for hd in range(H_CROSS):
        k = kv_ref[:, hd * dc:(hd + 1) * dc]
        v = kv_ref[:, d + hd * dc:d + (hd + 1) * dc]
        s = lax.dot_general(q[:, hd * dc:(hd + 1) * dc], k, NT_DIMS, preferred_element_type=F32)
        e = jnp.exp(s - jnp.max(s, axis=1, keepdims=True))
        den = jnp.sum(e, axis=1, keepdims=True)
        outs.append((jnp.dot(e.astype(BF16), v, preferred_element_type=F32) / den).astype(BF16))
    o = jnp.concatenate(outs, axis=1)
    h2 = h1 + jnp.dot(o, wo_ref[...], preferred_element_type=F32)
    _route_positions(eid1, eid2, jnp.where(g > 0, 1.0, 0.0), lpos_ref, tcnt_ref, tbase_ref, base_scr,
                     upper_scr, tm)
    h_ref[...] = h2
    f = _rms(h2, lnm_ref[...])
    f_ref[...] = f.astype(BF16)
    fprev_scr[...] = f


def _route_logits(f, wr_ref, br_ref):
    f_hi, f_lo = _split_bf16(f)
    w_hi, w_lo = _split_bf16(wr_ref[...])
    return (lax.dot_general(w_hi, f_hi, NT_DIMS, preferred_element_type=F32)
            + lax.dot_general(w_hi, f_lo, NT_DIMS, preferred_element_type=F32)
            + lax.dot_general(w_lo, f_hi, NT_DIMS, preferred_element_type=F32)) + br_ref[...]


def _route_select(lg, gate_ref, tm):
    gl = lg[0:N_GROUPS]
    gmax = jnp.max(gl, axis=0, keepdims=True)
    gidx = lax.broadcasted_iota(I32, gl.shape, 0)
    g_sel = jnp.min(jnp.where(gl == gmax, gidx, N_GROUPS), axis=0, keepdims=True)
    g_p = 1.0 / jnp.sum(jnp.exp(gl - gmax), axis=0, keepdims=True)

    e8 = jnp.zeros((E_PER_GROUP, tm), F32)
    for g in range(N_GROUPS):
        e8 = e8 + jnp.where(g_sel == g, lg[8 + g * E_PER_GROUP:8 + (g + 1) * E_PER_GROUP], 0.0)
    ex = jnp.exp(e8 - jnp.max(e8, axis=0, keepdims=True))
    ep = ex / jnp.sum(ex, axis=0, keepdims=True)
    idx8 = lax.broadcasted_iota(I32, ep.shape, 0)
    p1 = jnp.max(ep, axis=0, keepdims=True)
    i1 = jnp.min(jnp.where(ep == p1, idx8, E_PER_GROUP), axis=0, keepdims=True)
    ep2 = jnp.where(idx8 == i1, -1.0, ep)
    p2 = jnp.max(ep2, axis=0, keepdims=True)
    i2 = jnp.min(jnp.where(ep2 == p2, idx8, E_PER_GROUP), axis=0, keepdims=True)
    psum = p1 + p2
    gate_ref[0:1, :] = g_p * (p1 / psum)
    gate_ref[1:2, :] = g_p * (p2 / psum)
    return g_sel * E_PER_GROUP + i1, g_sel * E_PER_GROUP + i2


def _route_positions(eid1, eid2, live, lpos_ref, tcnt_ref, tbase_ref, base_scr, upper_scr, tm):
    e32 = lax.broadcasted_iota(I32, (N_EXPERTS, tm), 0)
    oh1 = (e32 == eid1).astype(F32)
    oh2 = (e32 == eid2).astype(F32)
    cnt = oh1 + oh2
    earlier = jnp.dot(cnt.astype(BF16), upper_scr[...], preferred_element_type=F32)
    seg = jnp.sum(cnt, axis=1, keepdims=True)
    seg = jnp.floor((seg + (SUBLANES - 1)) * (1.0 / SUBLANES)) * SUBLANES * live
    seg = jnp.broadcast_to(seg, (N_EXPERTS, LANES))
    er = lax.broadcasted_iota(I32, (N_EXPERTS, N_EXPERTS), 0)
    ec = lax.broadcasted_iota(I32, (N_EXPERTS, N_EXPERTS), 1)
    lower = jnp.where(ec < er, 1.0, 0.0).astype(BF16)
    start = jnp.dot(lower, seg.astype(BF16), preferred_element_type=F32)[:, 0:1]
    where = earlier + start
    lpos_ref[0:1, :] = jnp.sum(oh1 * where, axis=0, keepdims=True).astype(I32)
    lpos_ref[1:2, :] = jnp.sum(oh2 * where, axis=0, keepdims=True).astype(I32)
    tcnt_ref[...] = seg
    tbase_ref[...] = base_scr[...]
    base_scr[...] = base_scr[...] + seg


def _post(x2, od, osw, wout_bf, lnc, wq_bf, kv, wo_bf, lnm, wr, br, b, s, tm, m_len):
    n, d = x2.shape
    nt = s // tm
    tiles = n // tm
    main = lambda g: jnp.minimum(g, tiles - 1)
    routed = lambda g: jnp.maximum(g - 1, 0)
    row = lambda g: (main(g), 0)
    const = lambda g: (0, 0)
    return pl.pallas_call(
        functools.partial(_post_kernel, tm=tm, d=d),
        grid=(tiles + 1,),
        in_specs=[
            pl.BlockSpec((tm, d), row),
            pl.BlockSpec((tm, d // 2), row),
            pl.BlockSpec((tm, d // 2), row),
            pl.BlockSpec((d, d), const),
            pl.BlockSpec((1, d), const),
            pl.BlockSpec((d, d), const),
            pl.BlockSpec((m_len, 2 * d), lambda g: (main(g) // nt, 0)),
            pl.BlockSpec((d, d), const),
            pl.BlockSpec((1, d), const),
            pl.BlockSpec((8 + N_EXPERTS, d), const),
            pl.BlockSpec((8 + N_EXPERTS, 1), const),
        ],
        out_specs=[
            pl.BlockSpec((tm, d), row),
            pl.BlockSpec((tm, d), row),
            pl.BlockSpec((2, tm), lambda g: (0, routed(g))),
            pl.BlockSpec((2, tm), lambda g: (0, routed(g))),
            pl.BlockSpec((None, N_EXPERTS, LANES), lambda g: (routed(g), 0, 0)),
            pl.BlockSpec((None, N_EXPERTS, LANES), lambda g: (routed(g), 0, 0)),
        ],
        out_shape=[
            jax.ShapeDtypeStruct((n, d), F32),
            jax.ShapeDtypeStruct((n, d), BF16),
            jax.ShapeDtypeStruct((2, n), I32),
            jax.ShapeDtypeStruct((2, n), F32),
            jax.ShapeDtypeStruct((n // tm, N_EXPERTS, LANES), F32),
            jax.ShapeDtypeStruct((n // tm, N_EXPERTS, LANES), F32),
        ],
        scratch_shapes=[pltpu.VMEM((N_EXPERTS, LANES), F32), pltpu.VMEM((tm, d), F32), pltpu.VMEM((tm, tm), BF16)],
        compiler_params=pltpu.CompilerParams(
            dimension_semantics=("arbitrary",), vmem_limit_bytes=VMEM_LIMIT),
        name="post",
    )(x2, od, osw, wout_bf, lnc, wq_bf, kv, wo_bf, lnm, wr, br)


def _local_rows(tm):
    return 2 * tm + SUBLANES * N_EXPERTS


def _segment_copies(tile, tcnt_ref, tbase_ref, pstart_ref, make, act, max_len):
    sizes = [SUBLANES << k for k in reversed(range((max_len // SUBLANES).bit_length()))]

    if act == "wait":
        total = lax.fori_loop(0, N_EXPERTS, lambda e, acc: acc + tcnt_ref[tile * N_EXPERTS + e], 0)
        for size in sizes:
            @pl.when((total & size) != 0)
            def _():
                make(0, 0, size).wait()

        return total

    def pieces(ln, local0, glob0, some_sizes):
        for size in some_sizes:
            @pl.when((ln & size) != 0)
            def _():
                off = (ln // (2 * size)) * (2 * size)
                make(pl.multiple_of(local0 + off, SUBLANES), pl.multiple_of(glob0 + off, SUBLANES), size).start()

    big = [s for s in sizes if s >= LONG_SEGMENT_ROWS]
    small = [s for s in sizes if s < LONG_SEGMENT_ROWS]

    def per_expert(e, local0):
        ln = tcnt_ref[tile * N_EXPERTS + e]
        glob0 = pstart_ref[e] + tbase_ref[tile * N_EXPERTS + e]

        @pl.when(ln >= LONG_SEGMENT_ROWS)
        def _():
            pieces(ln, local0, glob0, big)

        pieces(ln, local0, glob0, small)
        return local0 + ln

    return lax.fori_loop(0, N_EXPERTS, per_expert, 0)


def _dispatch_kernel(tcnt_ref, tbase_ref, tot_ref, f_ref, lpos_ref, gate_ref, xs_hbm, pstart_ref, blk_ref, nact_ref,
                     loc_scr, zero_scr, sem, zsem, *, tm, bm, nblk):
    i = pl.program_id(0)
    nt = pl.num_programs(0)
    slot = i % 2

    @pl.when(i == 0)
    def _():
        def per_expert(e, blk0):
            nb = (tot_ref[e] + (bm - 1)) // bm
            pstart_ref[e] = blk0 * bm

            def fill(j, carry):
                blk_ref[j] = e
                return carry

            lax.fori_loop(blk0, blk0 + nb, fill, 0)
            return blk0 + nb

        nact = lax.fori_loop(0, N_EXPERTS, per_expert, 0)
        nact_ref[0] = nact

        def tail(j, carry):
            blk_ref[j] = N_EXPERTS - 1
            return carry

        lax.fori_loop(nact, nblk, tail, 0)

        zero_scr[...] = jnp.zeros(zero_scr.shape, U32)

        def zero_fill(act):
            def do(copy):
                if act == "start":
                    copy.start()
                else:
                    copy.wait()

            def per_expert_pad(e, carry):
                tot = tot_ref[e]
                pad = (tot + (bm - 1)) // bm * bm - tot
                row0 = pstart_ref[e] + tot
                for k in reversed(range((bm // SUBLANES).bit_length())):
                    size = SUBLANES << k

                    @pl.when((pad & size) != 0)
                    def _():
                        row = pl.multiple_of(row0 + (pad // (2 * size)) * (2 * size), SUBLANES)
                        do(pltpu.make_async_copy(zero_scr.at[pl.ds(0, size)], xs_hbm.at[pl.ds(row, size)], zsem))

                return carry

            lax.fori_loop(0, N_EXPERTS, per_expert_pad, 0)

            def per_unused_block(j, carry):
                do(pltpu.make_async_copy(zero_scr, xs_hbm.at[pl.ds(pl.multiple_of(j * bm, bm), bm)], zsem))
                return carry

            lax.fori_loop(nact, nblk, per_unused_block, 0)

        zero_fill("start")
        zero_fill("wait")

    def copies(tile, sl, act):
        def make(lrow, grow, size):
            return pltpu.make_async_copy(loc_scr.at[sl, pl.ds(lrow, size)], xs_hbm.at[pl.ds(grow, size)], sem.at[sl])

        _segment_copies(tile, tcnt_ref, tbase_ref, pstart_ref, make, act, _local_rows(tm))

    @pl.when(i >= 2)
    def _():
        copies(i - 2, slot, "wait")

    half = f_ref.shape[1] // 2
    chunk = math.gcd(_local_rows(tm), 256)
    for r0 in range(0, _local_rows(tm), chunk):
        pos = lax.broadcasted_iota(I32, (chunk, tm), 0) + r0
        first = pos == lpos_ref[0:1, :]
        second = pos == lpos_ref[1:2, :]
        onehot = jnp.where(first, 1.0, jnp.where(second, 1.0, 0.0)).astype(BF16)
        loc_scr[slot, r0:r0 + chunk, 0:half] = _pack_pairs(
            jnp.dot(onehot, f_ref[...], preferred_element_type=F32))
        gate = jnp.sum(jnp.where(first, gate_ref[0:1, :], jnp.where(second, gate_ref[1:2, :], 0.0)),
                       axis=1, keepdims=True)
        loc_scr[slot, r0:r0 + chunk, half:half + LANES] = lax.bitcast_convert_type(
            jnp.broadcast_to(gate, (chunk, LANES)), U32)
    copies(i, slot, "start")

    @pl.when(i == nt - 1)
    def _():
        @pl.when(i >= 1)
        def _():
            copies(i - 1, 1 - slot, "wait")

        copies(i, slot, "wait")


def _dispatch(tcnt_i, tbase_i, tot_i, f, lpos, gates, tm, bm, nblk):
    n, d = f.shape
    width = d // 2 + LANES
    smem = pl.BlockSpec(memory_space=pltpu.SMEM)
    hbm = pl.BlockSpec(memory_space=pl.ANY)
    return pl.pallas_call(
        functools.partial(_dispatch_kernel, tm=tm, bm=bm, nblk=nblk),
        grid_spec=pltpu.PrefetchScalarGridSpec(
            num_scalar_prefetch=3,
            grid=(n // tm,),
            in_specs=[
                pl.BlockSpec((tm, d), lambda i, *_: (i, 0)),
                pl.BlockSpec((2, tm), lambda i, *_: (0, i)),
                pl.BlockSpec((2, tm), lambda i, *_: (0, i)),
            ],
            out_specs=[hbm, smem, smem, smem],
            scratch_shapes=[
                pltpu.VMEM((2, _local_rows(tm), width), U32),
                pltpu.VMEM((bm, width), U32),
                pltpu.SemaphoreType.DMA((2,)),
                pltpu.SemaphoreType.DMA,
            ],
        ),
        out_shape=[
            jax.ShapeDtypeStruct((nblk * bm, width), U32),
            jax.ShapeDtypeStruct((N_EXPERTS,), I32),
            jax.ShapeDtypeStruct((nblk,), I32),
            jax.ShapeDtypeStruct((1,), I32),
        ],
        compiler_params=pltpu.CompilerParams(
            dimension_semantics=("arbitrary",), vmem_limit_bytes=VMEM_LIMIT),
        name="dispatch",
    )(tcnt_i, tbase_i, tot_i, f, lpos, gates)


def _expert_kernel(blk_ref, nact_ref, x_ref, wg_ref, wu_ref, wd_ref, y_ref, wg_bf, wu_bf, wd_bf):
    j = pl.program_id(0)
    active = j < nact_ref[0]

    @pl.when(active & ((j == 0) | (blk_ref[j] != blk_ref[jnp.maximum(j - 1, 0)])))
    def _():
        wg_bf[...] = wg_ref[...].astype(BF16)
        wu_bf[...] = wu_ref[...].astype(BF16)
        wd_bf[...] = wd_ref[...].astype(BF16)

    @pl.when(active)
    def _():
        half = y_ref.shape[1]
        x = _unpack_pairs(x_ref[:, 0:half])
        gate = lax.bitcast_convert_type(x_ref[:, half:half + 1], F32)
        g = jnp.dot(x, wg_bf[...], preferred_element_type=F32)
        u = jnp.dot(x, wu_bf[...], preferred_element_type=F32)
        hdn = (g * jax.nn.sigmoid(g) * u).astype(BF16)
        y = jnp.dot(hdn, wd_bf[...], preferred_element_type=F32) * gate
        y_ref[...] = _pack_pairs(y.astype(BF16).astype(F32))

    @pl.when(j >= nact_ref[0])
    def _():
        y_ref[...] = jnp.zeros(y_ref.shape, y_ref.dtype)


def _experts(blk_e, nact, xs, wg, wu, wd, bm):
    p, width = xs.shape
    d, dff = wg.shape[-2:]
    dh = d // 2
    nblk = p // bm
    rowmap = lambda j, blk, na: (jnp.minimum(j, na[0] - 1), 0)
    wmap = lambda j, blk, na: (blk[jnp.minimum(j, na[0] - 1)], 0, 0)
    return pl.pallas_call(
        _expert_kernel,
        grid_spec=pltpu.PrefetchScalarGridSpec(
            num_scalar_prefetch=2,
            grid=(nblk,),
            in_specs=[
                pl.BlockSpec((bm, width), rowmap),
                pl.BlockSpec((None, d, dff), wmap),
                pl.BlockSpec((None, d, dff), wmap),
                pl.BlockSpec((None, dff, d), wmap),
            ],
            out_specs=pl.BlockSpec((bm, dh), lambda j, blk, na: (j, 0)),
            scratch_shapes=[pltpu.VMEM((d, dff), BF16), pltpu.VMEM((d, dff), BF16), pltpu.VMEM((dff, d), BF16)],
        ),
        out_shape=jax.ShapeDtypeStruct((p, dh), U32),
        compiler_params=pltpu.CompilerParams(
            dimension_semantics=("arbitrary",), vmem_limit_bytes=VMEM_LIMIT),
        name="experts",
    )(blk_e, nact, xs, wg, wu, wd)


def _combine_kernel(tcnt_ref, tbase_ref, pstart_ref, h_ref, lpos_ref, lnf_ref, ys_hbm, o_ref,
                    loc_scr, sem, *, tm):
    i = pl.program_id(0)
    nt = pl.num_programs(0)
    slot = i % 2

    def copies(tile, sl, act):
        def make(lrow, grow, size):
            return pltpu.make_async_copy(ys_hbm.at[pl.ds(grow, size)], loc_scr.at[sl, pl.ds(lrow, size)], sem.at[sl])

        return _segment_copies(tile, tcnt_ref, tbase_ref, pstart_ref, make, act, _local_rows(tm))

    @pl.when(i == 0)
    def _():
        copies(0, 0, "start")

    @pl.when(i + 1 < nt)
    def _():
        copies(i + 1, 1 - slot, "start")

    used = copies(i, slot, "wait")
    row = lax.broadcasted_iota(I32, (_local_rows(tm), 1), 0)
    ys = _unpack_pairs(jnp.where(row < used, loc_scr[slot], U32(0)))

    r8 = lax.broadcasted_iota(I32, (8, tm), 0)
    lp = lpos_ref[...].astype(F32)
    top = jnp.where(r8 == 0, lp[0:1], jnp.where(r8 == 1, lp[1:2], 0.0))
    cols = jnp.concatenate([top, jnp.zeros((LANES - 8, tm), F32)], axis=0).T.astype(I32)
    pos = lax.broadcasted_iota(I32, (tm, _local_rows(tm)), 1)
    sel = jnp.where(pos == cols[:, 0:1], 1.0, jnp.where(pos == cols[:, 1:2], 1.0, 0.0)).astype(BF16)
    o_ref[...] = _rms(h_ref[...] + jnp.dot(sel, ys, preferred_element_type=F32), lnf_ref[...])


def _combine(tcnt_i, tbase_i, pstart, h2, lpos, lnf, ys, tm):
    n, d = h2.shape
    return pl.pallas_call(
        functools.partial(_combine_kernel, tm=tm),
        grid_spec=pltpu.PrefetchScalarGridSpec(
            num_scalar_prefetch=3,
            grid=(n // tm,),
            in_specs=[
                pl.BlockSpec((tm, d), lambda i, *_: (i, 0)),
                pl.BlockSpec((2, tm), lambda i, *_: (0, i)),
                pl.BlockSpec((1, d), lambda i, *_: (0, 0)),
                pl.BlockSpec(memory_space=pl.ANY),
            ],
            out_specs=pl.BlockSpec((tm, d), lambda i, *_: (i, 0)),
            scratch_shapes=[pltpu.VMEM((2, _local_rows(tm), d // 2), U32), pltpu.SemaphoreType.DMA((2,))],
        ),
        out_shape=jax.ShapeDtypeStruct((n, d), F32),
        compiler_params=pltpu.CompilerParams(
            dimension_semantics=("arbitrary",), vmem_limit_bytes=VMEM_LIMIT),
        name="combine",
    )(tcnt_i, tbase_i, pstart, h2, lpos, lnf, ys)


def _swa_head_perm():
    cols = []
    for p in range(4):
        for half in range(2):
            head = half * 4 + p
            cols.extend(range(head * D_HEAD, (head + 1) * D_HEAD))
    return jnp.asarray(cols, dtype=I32)


def kernel(x, mem, positions, ln_mix_w, w_in, lambda_q1, lambda_k1, lambda_q2, lambda_k2, subln_w, sinks,
           w_out, ln_cross_w, ln_mem_w, wq_cross, wkv_cross, wo_cross, ln_moe_w, w_group, b_group,
           w_expert, b_expert, w_gate, w_up, w_down, ln_final_w):
    b, s, d = x.shape
    m_len = mem.shape[1]
    n = b * s
    assert w_in.shape[0] == 1 and d == 1024, "one layer, model width 1024 (head layout of the input projection)"
    lambda_init = 0.8 - 0.6 * math.exp(-0.3 * 0)

    tm = min(TOKEN_TILE, s)
    tw = min(WIDE_TOKEN_TILE, s)
    t_attn = ATTN_QUERY_TILE
    tk_attn = ATTN_KEY_BLOCK
    bm = EXPERT_BLOCK_ROWS

    x2 = x.reshape(n, d)
    half = D_HEAD // 2
    per_row = LANES // half
    inv_freq = jnp.exp(-math.log(ROPE_THETA) * jnp.arange(0, D_HEAD, 2, dtype=F32) / D_HEAD)
    inv128 = jnp.tile(inv_freq, per_row).reshape(1, LANES)
    pos_tiles = positions.reshape(n // tw, per_row, tw // per_row).astype(I32).transpose(0, 2, 1)
    pos_rep = jnp.repeat(pos_tiles.reshape(n // per_row, per_row), half, axis=1)
    cos_t, sin_t = _rope_tables(pos_rep, inv128, min(1024, n // per_row))

    perm = _swa_head_perm()
    sq0 = 3 * 512
    w_in_l = w_in[0]
    w_in_p = jnp.concatenate([w_in_l[:, :sq0], w_in_l[:, sq0:sq0 + 512][:, perm], w_in_l[:, sq0 + 512:]], axis=1)
    w_out_l = w_out[0]
    w_out_p = jnp.concatenate([w_out_l[:512], w_out_l[512:][perm]], axis=0)

    qkv = _inproj(x2, ln_mix_w[0].reshape(1, d), w_in_p.astype(BF16), cos_t, sin_t, tw)
    o_diff = _diff_attention(qkv, lambda_q1[0].reshape(1, -1), lambda_k1[0].reshape(1, -1),
                             lambda_q2[0].reshape(1, -1), lambda_k2[0].reshape(1, -1),
                             subln_w[0].reshape(-1, 1), b, s, t_attn, min(tk_attn, s), lambda_init)
    o_swa = _swa_attention(qkv, sinks[0].astype(F32), b, s, tm)
    kv = _memkv(mem.reshape(b * m_len, d), ln_mem_w[0].reshape(1, d), wkv_cross[0].astype(BF16), m_len)

    wr = jnp.concatenate([w_group[0].T, jnp.zeros((8 - N_GROUPS, d), F32), w_expert[0].T], axis=0)
    br = jnp.concatenate([b_group[0], jnp.zeros((8 - N_GROUPS,), F32), b_expert[0]]).reshape(-1, 1)
    h2, f, lpos, gates, tcnt, tbase = _post(
        x2, o_diff, o_swa, w_out_p.astype(BF16), ln_cross_w[0].reshape(1, d), wq_cross[0].astype(BF16), kv,
        wo_cross[0].astype(BF16), ln_moe_w[0].reshape(1, d), wr, br, b, s, tm, m_len)

    tcnt_i = tcnt[:, :, 0].astype(I32).reshape(-1)
    tbase_i = tbase[:, :, 0].astype(I32).reshape(-1)
    tot_i = tbase_i[-N_EXPERTS:] + tcnt_i[-N_EXPERTS:]
    rows_max = 2 * n + (n // tm) * N_EXPERTS * (SUBLANES - 1)
    nblk = (rows_max + N_EXPERTS * (bm - 1) + bm - 1) // bm
    xs, pstart, blk_e, nact = _dispatch(tcnt_i, tbase_i, tot_i, f, lpos, gates, tm, bm, nblk)
    ys = _experts(blk_e, nact, xs, w_gate[0], w_up[0], w_down[0], bm)
    out = _combine(tcnt_i, tbase_i, pstart, h2, lpos, ln_final_w.reshape(1, d), ys, tm)
    return out.reshape(b, s, d)
```

```python
import functools
import math

import jax
import jax.numpy as jnp
from jax import lax
from jax.experimental import pallas as pl
from jax.experimental.pallas import tpu as pltpu

F32 = jnp.float32
BF16 = jnp.bfloat16
I32 = jnp.int32

D_HEAD = 64
ROPE_THETA = 10000.0
H_DIFF = 4
N_Q_SWA = 8
N_KV_SWA = 2
WINDOW = 128
H_CROSS = 4
N_GROUPS = 4
E_PER_GROUP = 8
N_EXPERTS = N_GROUPS * E_PER_GROUP
EPS = 1e-6
LANES = 128
SUBLANES = 8
LONG_SEGMENT_ROWS = 128
IN_W = 2304
NEG = -1e30

VMEM_LIMIT = 56 * 1024 * 1024

TOKEN_TILE = 512
WIDE_TOKEN_TILE = 1024
ATTN_QUERY_TILE = 256
ATTN_KEY_BLOCK = 1024
EXPERT_BLOCK_ROWS = 512

NT_DIMS = (((1,), (1,)), ((), ()))


U32 = jnp.uint32


def _pack_pairs(x):
    h = x.shape[1] // 2
    hi = lax.bitcast_convert_type(x[:, :h], U32)
    lo = lax.bitcast_convert_type(x[:, h:], U32)
    return hi | (lo >> 16)


def _unpack_pairs(p):
    a = lax.bitcast_convert_type(p & U32(0xFFFF0000), F32)
    b = lax.bitcast_convert_type(p << 16, F32)
    return jnp.concatenate([a, b], axis=1).astype(BF16)


def _rms(x, w):
    ms = jnp.mean(x * x, axis=-1, keepdims=True)
    return x * lax.rsqrt(ms + EPS) * w


def _rope_table_kernel(pos_ref, inv_ref, cos_ref, sin_ref):
    ang = pos_ref[...].astype(F32) * inv_ref[...]
    cos_ref[...] = jnp.cos(ang)
    sin_ref[...] = jnp.sin(ang)


def _rope_tables(pos_rep, inv128, rows):
    n4 = pos_rep.shape[0]
    spec = pl.BlockSpec((rows, LANES), lambda i: (i, 0))
    return pl.pallas_call(
        _rope_table_kernel,
        grid=(n4 // rows,),
        in_specs=[spec, pl.BlockSpec((1, LANES), lambda i: (0, 0))],
        out_specs=[spec, spec],
        out_shape=[jax.ShapeDtypeStruct((n4, LANES), F32)] * 2,
        compiler_params=pltpu.CompilerParams(dimension_semantics=("arbitrary",)),
        name="ropetab",
    )(pos_rep, inv128)


def _inproj_kernel(x_ref, lnw_ref, w_ref, cos_ref, sin_ref, o_ref):
    a = _rms(x_ref[...], lnw_ref[...]).astype(BF16)
    half = D_HEAD // 2
    reps = LANES // half

    def expand(tab_ref):
        parts = [jnp.concatenate([tab_ref[:, j * half:(j + 1) * half]] * reps, axis=1) for j in range(reps)]
        return jnp.concatenate(parts, axis=0)

    cos = expand(cos_ref)
    sin = expand(sin_ref)
    lane = lax.broadcasted_iota(I32, (1, LANES), 1)
    first = (lane % D_HEAD) < (D_HEAD // 2)
    sin_signed = jnp.where(first, -sin, sin)
    n_chunks = IN_W // 256
    value_chunks = (4, 5)
    for c in [c for c in range(n_chunks) if c not in value_chunks] + list(value_chunks):
        p = jnp.dot(a, w_ref[:, c * 256:(c + 1) * 256], preferred_element_type=F32)
        for hh in range(2):
            g = c * 2 + hh
            xg = p[:, hh * LANES:(hh + 1) * LANES]
            is_v = (8 <= g < 12) or g == 17
            if not is_v:
                partner = jnp.where(first, pltpu.roll(xg, 96, 1), pltpu.roll(xg, 32, 1))
                xg = xg * cos + partner * sin_signed
                if g < 4 or 12 <= g < 16:
                    xg = xg * (D_HEAD ** -0.5 * math.log2(math.e))
            o_ref[:, g * LANES:(g + 1) * LANES] = xg.astype(BF16)


def _inproj(x2, ln_w, w_in_bf, cos, sin, tm):
    n, d = x2.shape
    dense_rows = tm // (LANES // (D_HEAD // 2))
    return pl.pallas_call(
        _inproj_kernel,
        grid=(n // tm,),
        in_specs=[
            pl.BlockSpec((tm, d), lambda i: (i, 0)),
            pl.BlockSpec((1, d), lambda i: (0, 0)),
            pl.BlockSpec((d, IN_W), lambda i: (0, 0)),
            pl.BlockSpec((dense_rows, LANES), lambda i: (i, 0)),
            pl.BlockSpec((dense_rows, LANES), lambda i: (i, 0)),
        ],
        out_specs=pl.BlockSpec((tm, IN_W), lambda i: (i, 0)),
        out_shape=jax.ShapeDtypeStruct((n, IN_W), BF16),
        compiler_params=pltpu.CompilerParams(
            dimension_semantics=("arbitrary",), vmem_limit_bytes=VMEM_LIMIT),
        name="inproj",
    )(x2, ln_w, w_in_bf, cos, sin)


def _diff_group_kernel(q_ref, k_ref, v_ref, lq1_ref, lk1_ref, lq2_ref, lk2_ref, sw_ref, o_ref,
                       vt_scr, q2_scr, acc_scr, m_scr, l_scr, s_scr, *, t, tk, lambda_init):
    s = k_ref.shape[0]
    per = tk // t
    for c in range(s // tk):
        for r in range(per):
            rows = slice(c * tk + r * t, c * tk + (r + 1) * t)
            vt_scr[c, :, r * t:(r + 1) * t] = v_ref[rows, :].astype(F32).T.astype(BF16)

    lane = lax.broadcasted_iota(I32, (1, LANES), 1)
    lam = (jnp.exp(jnp.sum(lq1_ref[...] * lk1_ref[...], axis=1, keepdims=True))
           - jnp.exp(jnp.sum(lq2_ref[...] * lk2_ref[...], axis=1, keepdims=True))
           + lambda_init)

    def accumulate(r, st_parts, vt):
        m_prev = m_scr[r]
        m_new = m_prev
        for st in st_parts:
            m_new = jnp.maximum(m_new, jnp.max(st, axis=0, keepdims=True))
        ps = [jnp.exp2(st - m_new) for st in st_parts]
        alpha = jnp.exp2(m_prev - m_new)
        l_new = alpha * l_scr[r]
        for p in ps:
            l_new = l_new + jnp.sum(p, axis=0, keepdims=True)
        l_scr[r] = l_new
        pb = [p.astype(BF16) for p in ps]
        pb = pb[0] if len(pb) == 1 else jnp.concatenate(pb, axis=0)
        acc_scr[r] = alpha * acc_scr[r] + jnp.dot(vt, pb, preferred_element_type=F32)
        m_scr[r] = m_new

    def interleaved(score_fn, softmax_fn, next_first_scores):
        for r in range(per):
            if r + 1 < per:
                score_fn(r + 1)
            else:
                next_first_scores()
            softmax_fn(r)

    def group(gi, carry):
        row0 = pl.multiple_of(gi * tk, tk)
        for r in range(per):
            q = q_ref[pl.ds(row0 + r * t, t), :]
            zero = jnp.zeros_like(q)
            q2_scr[r] = jnp.concatenate(
                [jnp.where(lane < D_HEAD, q, zero), jnp.where(lane >= D_HEAD, q, zero)], axis=0)
        acc_scr[...] = jnp.zeros(acc_scr.shape, F32)
        m_scr[...] = jnp.full(m_scr.shape, NEG, F32)
        l_scr[...] = jnp.zeros(l_scr.shape, F32)

        def diag_scores(r):
            rows = (r + 1) * t
            k = k_ref[pl.ds(row0, rows), :]
            s_scr[r, 0:rows, :] = lax.dot_general(k, q2_scr[r], NT_DIMS, preferred_element_type=F32)

        def diag_softmax(r):
            rows = (r + 1) * t
            key = lax.broadcasted_iota(I32, (t, 2 * t), 0)
            col = lax.broadcasted_iota(I32, (t, 2 * t), 1)
            diag = jnp.where(key <= jnp.where(col >= t, col - t, col), s_scr[r, r * t:rows, :], NEG)
            parts = [diag] if r == 0 else [s_scr[r, 0:r * t, :], diag]
            accumulate(r, parts, vt_scr[gi, :, 0:rows])

        def full_scores(j, r):
            k = k_ref[pl.ds(pl.multiple_of(j * tk, tk), tk), :]
            s_scr[r] = lax.dot_general(k, q2_scr[r], NT_DIMS, preferred_element_type=F32)

        diag_scores(0)
        interleaved(diag_scores, diag_softmax, functools.partial(full_scores, 0, 0))

        def key_block(j, c2):
            interleaved(functools.partial(full_scores, j), lambda r: accumulate(r, [s_scr[r]], vt_scr[j]),
                        functools.partial(full_scores, j + 1, 0))
            return c2

        lax.fori_loop(0, gi, key_block, 0)

        for r in range(per):
            on = acc_scr[r] / l_scr[r]
            o = on[:, :t] - lam * on[:, t:]
            ms = jnp.mean(o * o, axis=0, keepdims=True)
            o = o * lax.rsqrt(ms + EPS) * sw_ref[...] * (1.0 - lambda_init)
            o_ref[pl.ds(row0 + r * t, t), :] = o.T.astype(BF16)
        return carry

    lax.fori_loop(0, s // tk, group, 0)


def _diff_attention(qkv, lq1, lk1, lq2, lk2, subln_col, b, s, t, tk, lambda_init):
    n = qkv.shape[0]
    per = tk // t
    small = pl.BlockSpec((1, D_HEAD), lambda bi, h: (0, 0))
    return pl.pallas_call(
        functools.partial(_diff_group_kernel, t=t, tk=tk, lambda_init=lambda_init),
        grid=(b, H_DIFF),
        in_specs=[
            pl.BlockSpec((s, LANES), lambda bi, h: (bi, h)),
            pl.BlockSpec((s, LANES), lambda bi, h: (bi, 4 + h)),
            pl.BlockSpec((s, LANES), lambda bi, h: (bi, 8 + h)),
            small, small, small, small,
            pl.BlockSpec((LANES, 1), lambda bi, h: (0, 0)),
        ],
        out_specs=pl.BlockSpec((s, LANES), lambda bi, h: (bi, h)),
        out_shape=jax.ShapeDtypeStruct((n, H_DIFF * LANES), BF16),
        scratch_shapes=[
            pltpu.VMEM((s // tk, LANES, tk), BF16),
            pltpu.VMEM((per, 2 * t, LANES), BF16),
            pltpu.VMEM((per, LANES, 2 * t), F32),
            pltpu.VMEM((per, 1, 2 * t), F32),
            pltpu.VMEM((per, 1, 2 * t), F32),
            pltpu.VMEM((per, tk, 2 * t), F32),
        ],
        compiler_params=pltpu.CompilerParams(
            dimension_semantics=("arbitrary", "arbitrary"), vmem_limit_bytes=VMEM_LIMIT),
        name="diffattn",
    )(qkv, qkv, qkv, lq1, lk1, lq2, lk2, subln_col)


def _swa_kernel(sink_ref, q_ref, kc_ref, kp_ref, vc_ref, vp_ref, o_ref, kbuf, vbuf, *, tq):
    i = pl.program_id(1)
    w = WINDOW
    kbuf[0:w, :] = kp_ref[...]
    kbuf[w:w + tq, :] = kc_ref[...]
    vbuf[0:w, :] = vp_ref[...]
    vbuf[w:w + tq, :] = vc_ref[...]
    lane = lax.broadcasted_iota(I32, (1, LANES), 1)
    lo = lane < D_HEAD
    qi = lax.broadcasted_iota(I32, (w, 2 * w), 0)
    ki = lax.broadcasted_iota(I32, (w, 2 * w), 1)
    band = (ki > qi) & (ki <= qi + w)
    band_first = band & (ki >= jnp.where(i > 0, 0, w))
    def scores(r):
        keys = kbuf[r * w:(r + 2) * w, :]
        qs = []
        for half in range(2):
            for p in range(4):
                qg = q_ref[r * w:(r + 1) * w, p * LANES:(p + 1) * LANES]
                qs.append(jnp.where(lo if half == 0 else ~lo, qg, jnp.zeros_like(qg)))
        qstack = jnp.concatenate(qs, axis=0)
        return lax.dot_general(qstack, keys, NT_DIMS, preferred_element_type=F32)

    s_next = scores(0)
    for r in range(tq // w):
        s_all = s_next
        if r + 1 < tq // w:
            s_next = scores(r + 1)
        vals = vbuf[r * w:(r + 2) * w, :]
        valid = band_first if r == 0 else band
        ps, inv = [], []
        for hb in range(8):
            sink = sink_ref[hb] * math.log2(math.e)
            s = jnp.where(valid, s_all[hb * w:(hb + 1) * w], NEG)
            m = jnp.maximum(jnp.max(s, axis=1, keepdims=True), sink)
            e = jnp.exp2(s - m)
            inv.append(1.0 / (jnp.sum(e, axis=1, keepdims=True) + jnp.exp2(sink - m)))
            ps.append(e.astype(BF16))
        pv = jnp.dot(jnp.concatenate(ps, axis=0), vals, preferred_element_type=F32)
        for p in range(4):
            og = jnp.where(lo, pv[p * w:(p + 1) * w] * inv[p], pv[(4 + p) * w:(5 + p) * w] * inv[4 + p])
            o_ref[r * w:(r + 1) * w, p * LANES:(p + 1) * LANES] = og.astype(BF16)


def _swa_attention(qkv, sinks, b, s, tq):
    n = qkv.shape[0]
    nq = s // tq
    per = tq // WINDOW
    prev_map = lambda col: (lambda bi, i, sk: (jnp.maximum(bi * (s // WINDOW) + i * per - 1, 0), col))
    cur_map = lambda col: (lambda bi, i, sk: (bi * nq + i, col))
    return pl.pallas_call(
        functools.partial(_swa_kernel, tq=tq),
        grid_spec=pltpu.PrefetchScalarGridSpec(
            num_scalar_prefetch=1,
            grid=(b, nq),
            in_specs=[
                pl.BlockSpec((tq, 4 * LANES), lambda bi, i, sk: (bi * nq + i, 3)),
                pl.BlockSpec((tq, LANES), cur_map(16)),
                pl.BlockSpec((WINDOW, LANES), prev_map(16)),
                pl.BlockSpec((tq, LANES), cur_map(17)),
                pl.BlockSpec((WINDOW, LANES), prev_map(17)),
            ],
            out_specs=pl.BlockSpec((tq, 4 * LANES), lambda bi, i, sk: (bi * nq + i, 0)),
            scratch_shapes=[pltpu.VMEM((WINDOW + tq, LANES), BF16)] * 2,
        ),
        out_shape=jax.ShapeDtypeStruct((n, 4 * LANES), BF16),
        compiler_params=pltpu.CompilerParams(
            dimension_semantics=("arbitrary", "arbitrary"), vmem_limit_bytes=VMEM_LIMIT),
        name="swa",
    )(sinks, qkv, qkv, qkv, qkv, qkv)


def _memkv_kernel(m_ref, lnw_ref, w_ref, o_ref):
    a = _rms(m_ref[...], lnw_ref[...]).astype(BF16)
    o_ref[...] = jnp.dot(a, w_ref[...], preferred_element_type=F32).astype(BF16)


def _memkv(mem2, ln_w, wkv_bf, m_len):
    n, d = mem2.shape
    return pl.pallas_call(
        _memkv_kernel,
        grid=(n // m_len,),
        in_specs=[
            pl.BlockSpec((m_len, d), lambda i: (i, 0)),
            pl.BlockSpec((1, d), lambda i: (0, 0)),
            pl.BlockSpec((d, 2 * d), lambda i: (0, 0)),
        ],
        out_specs=pl.BlockSpec((m_len, 2 * d), lambda i: (i, 0)),
        out_shape=jax.ShapeDtypeStruct((n, 2 * d), BF16),
        compiler_params=pltpu.CompilerParams(
            dimension_semantics=("arbitrary",), vmem_limit_bytes=VMEM_LIMIT),
        name="memkv",
    )(mem2, ln_w, wkv_bf)


def _split_bf16(x):
    hi = x.astype(BF16)
    lo = (x - hi.astype(F32)).astype(BF16)
    return hi, lo


def _post_kernel(x_ref, od_ref, os_ref, wout_ref, lnc_ref, wq_ref, kv_ref, wo_ref, lnm_ref, wr_ref, br_ref,
                 h_ref, f_ref, lpos_ref, gate_ref, tcnt_ref, tbase_ref, base_scr, fprev_scr, *, tm, d):
    g = pl.program_id(0)

    @pl.when(g == 0)
    def _():
        base_scr[...] = jnp.zeros(base_scr.shape, F32)
        fprev_scr[...] = jnp.zeros(fprev_scr.shape, F32)

    mix = jnp.concatenate([od_ref[...], os_ref[...]], axis=1)
    h1 = x_ref[...] + jnp.dot(mix, wout_ref[...], preferred_element_type=F32)
    logits = _route_logits(fprev_scr[...], wr_ref, br_ref)

    c = _rms(h1, lnc_ref[...]).astype(BF16)
    dc = d // H_CROSS
    q = (jnp.dot(c, wq_ref[...], preferred_element_type=F32) * (dc ** -0.5)).astype(BF16)
    eid1, eid2 = _route_select(logits, gate_ref, tm)
    outs = []
    for hd in range(H_CROSS):
        k = kv_ref[:, hd * dc:(hd + 1) * dc]
        v = kv_ref[:, d + hd * dc:d + (hd + 1) * dc]
        s = lax.dot_general(q[:, hd * dc:(hd + 1) * dc], k, NT_DIMS, preferred_element_type=F32)
        e = jnp.exp(s - jnp.max(s, axis=1, keepdims=True))
        den = jnp.sum(e, axis=1, keepdims=True)
        outs.append((jnp.dot(e.astype(BF16), v, preferred_element_type=F32) / den).astype(BF16))
    o = jnp.concatenate(outs, axis=1)
    h2 = h1 + jnp.dot(o, wo_ref[...], preferred_element_type=F32)
    _route_positions(eid1, eid2, jnp.where(g > 0, 1.0, 0.0), lpos_ref, tcnt_ref, tbase_ref, base_scr, tm)
    h_ref[...] = h2
    f = _rms(h2, lnm_ref[...])
    f_ref[...] = f.astype(BF16)
    fprev_scr[...] = f


def _route_logits(f, wr_ref, br_ref):
    f_hi, f_lo = _split_bf16(f)
    w_hi, w_lo = _split_bf16(wr_ref[...])
    return (lax.dot_general(w_hi, f_hi, NT_DIMS, preferred_element_type=F32)
            + lax.dot_general(w_hi, f_lo, NT_DIMS, preferred_element_type=F32)
            + lax.dot_general(w_lo, f_hi, NT_DIMS, preferred_element_type=F32)) + br_ref[...]


def _route_select(lg, gate_ref, tm):
    gl = lg[0:N_GROUPS]
    gmax = jnp.max(gl, axis=0, keepdims=True)
    gidx = lax.broadcasted_iota(I32, gl.shape, 0)
    g_sel = jnp.min(jnp.where(gl == gmax, gidx, N_GROUPS), axis=0, keepdims=True)
    g_p = 1.0 / jnp.sum(jnp.exp(gl - gmax), axis=0, keepdims=True)

    e8 = jnp.zeros((E_PER_GROUP, tm), F32)
    for g in range(N_GROUPS):
        e8 = e8 + jnp.where(g_sel == g, lg[8 + g * E_PER_GROUP:8 + (g + 1) * E_PER_GROUP], 0.0)
    ex = jnp.exp(e8 - jnp.max(e8, axis=0, keepdims=True))
    ep = ex / jnp.sum(ex, axis=0, keepdims=True)
    idx8 = lax.broadcasted_iota(I32, ep.shape, 0)
    p1 = jnp.max(ep, axis=0, keepdims=True)
    i1 = jnp.min(jnp.where(ep == p1, idx8, E_PER_GROUP), axis=0, keepdims=True)
    ep2 = jnp.where(idx8 == i1, -1.0, ep)
    p2 = jnp.max(ep2, axis=0, keepdims=True)
    i2 = jnp.min(jnp.where(ep2 == p2, idx8, E_PER_GROUP), axis=0, keepdims=True)
    psum = p1 + p2
    gate_ref[0:1, :] = g_p * (p1 / psum)
    gate_ref[1:2, :] = g_p * (p2 / psum)
    return g_sel * E_PER_GROUP + i1, g_sel * E_PER_GROUP + i2


def _route_positions(eid1, eid2, live, lpos_ref, tcnt_ref, tbase_ref, base_scr, tm):
    e32 = lax.broadcasted_iota(I32, (N_EXPERTS, tm), 0)
    oh1 = (e32 == eid1).astype(F32)
    oh2 = (e32 == eid2).astype(F32)
    cnt = oh1 + oh2
    tr = lax.broadcasted_iota(I32, (tm, tm), 0)
    tc = lax.broadcasted_iota(I32, (tm, tm), 1)
    upper = jnp.where(tr < tc, 1.0, 0.0).astype(BF16)
    earlier = jnp.dot(cnt.astype(BF16), upper, preferred_element_type=F32)
    seg = jnp.sum(cnt, axis=1, keepdims=True)
    seg = jnp.floor((seg + (SUBLANES - 1)) * (1.0 / SUBLANES)) * SUBLANES * live
    seg = jnp.broadcast_to(seg, (N_EXPERTS, LANES))
    er = lax.broadcasted_iota(I32, (N_EXPERTS, N_EXPERTS), 0)
    ec = lax.broadcasted_iota(I32, (N_EXPERTS, N_EXPERTS), 1)
    lower = jnp.where(ec < er, 1.0, 0.0).astype(BF16)
    start = jnp.dot(lower, seg.astype(BF16), preferred_element_type=F32)[:, 0:1]
    where = earlier + start
    lpos_ref[0:1, :] = jnp.sum(oh1 * where, axis=0, keepdims=True).astype(I32)
    lpos_ref[1:2, :] = jnp.sum(oh2 * where, axis=0, keepdims=True).astype(I32)
    tcnt_ref[...] = seg
    tbase_ref[...] = base_scr[...]
    base_scr[...] = base_scr[...] + seg


def _post(x2, od, osw, wout_bf, lnc, wq_bf, kv, wo_bf, lnm, wr, br, b, s, tm, m_len):
    n, d = x2.shape
    nt = s // tm
    tiles = n // tm
    main = lambda g: jnp.minimum(g, tiles - 1)
    routed = lambda g: jnp.maximum(g - 1, 0)
    row = lambda g: (main(g), 0)
    const = lambda g: (0, 0)
    return pl.pallas_call(
        functools.partial(_post_kernel, tm=tm, d=d),
        grid=(tiles + 1,),
        in_specs=[
            pl.BlockSpec((tm, d), row),
            pl.BlockSpec((tm, d // 2), row),
            pl.BlockSpec((tm, d // 2), row),
            pl.BlockSpec((d, d), const),
            pl.BlockSpec((1, d), const),
            pl.BlockSpec((d, d), const),
            pl.BlockSpec((m_len, 2 * d), lambda g: (main(g) // nt, 0)),
            pl.BlockSpec((d, d), const),
            pl.BlockSpec((1, d), const),
            pl.BlockSpec((8 + N_EXPERTS, d), const),
            pl.BlockSpec((8 + N_EXPERTS, 1), const),
        ],
        out_specs=[
            pl.BlockSpec((tm, d), row),
            pl.BlockSpec((tm, d), row),
            pl.BlockSpec((2, tm), lambda g: (0, routed(g))),
            pl.BlockSpec((2, tm), lambda g: (0, routed(g))),
            pl.BlockSpec((None, N_EXPERTS, LANES), lambda g: (routed(g), 0, 0)),
            pl.BlockSpec((None, N_EXPERTS, LANES), lambda g: (routed(g), 0, 0)),
        ],
        out_shape=[
            jax.ShapeDtypeStruct((n, d), F32),
            jax.ShapeDtypeStruct((n, d), BF16),
            jax.ShapeDtypeStruct((2, n), I32),
            jax.ShapeDtypeStruct((2, n), F32),
            jax.ShapeDtypeStruct((n // tm, N_EXPERTS, LANES), F32),
            jax.ShapeDtypeStruct((n // tm, N_EXPERTS, LANES), F32),
        ],
        scratch_shapes=[pltpu.VMEM((N_EXPERTS, LANES), F32), pltpu.VMEM((tm, d), F32)],
        compiler_params=pltpu.CompilerParams(
            dimension_semantics=("arbitrary",), vmem_limit_bytes=VMEM_LIMIT),
        name="post",
    )(x2, od, osw, wout_bf, lnc, wq_bf, kv, wo_bf, lnm, wr, br)


def _local_rows(tm):
    return 2 * tm + SUBLANES * N_EXPERTS


def _segment_copies(tile, tcnt_ref, tbase_ref, pstart_ref, make, act, max_len):
    sizes = [SUBLANES << k for k in reversed(range((max_len // SUBLANES).bit_length()))]

    if act == "wait":
        total = lax.fori_loop(0, N_EXPERTS, lambda e, acc: acc + tcnt_ref[tile * N_EXPERTS + e], 0)
        for size in sizes:
            @pl.when((total & size) != 0)
            def _():
                make(0, 0, size).wait()

        return total

    def pieces(ln, local0, glob0, some_sizes):
        for size in some_sizes:
            @pl.when((ln & size) != 0)
            def _():
                off = (ln // (2 * size)) * (2 * size)
                make(pl.multiple_of(local0 + off, SUBLANES), pl.multiple_of(glob0 + off, SUBLANES), size).start()

    big = [s for s in sizes if s >= LONG_SEGMENT_ROWS]
    small = [s for s in sizes if s < LONG_SEGMENT_ROWS]

    def per_expert(e, local0):
        ln = tcnt_ref[tile * N_EXPERTS + e]
        glob0 = pstart_ref[e] + tbase_ref[tile * N_EXPERTS + e]

        @pl.when(ln >= LONG_SEGMENT_ROWS)
        def _():
            pieces(ln, local0, glob0, big)

        pieces(ln, local0, glob0, small)
        return local0 + ln

    return lax.fori_loop(0, N_EXPERTS, per_expert, 0)


def _dispatch_kernel(tcnt_ref, tbase_ref, tot_ref, f_ref, lpos_ref, gate_ref, xs_hbm, pstart_ref, blk_ref, nact_ref,
                     loc_scr, zero_scr, sem, zsem, *, tm, bm, nblk):
    i = pl.program_id(0)
    nt = pl.num_programs(0)
    slot = i % 2

    @pl.when(i == 0)
    def _():
        def per_expert(e, blk0):
            nb = (tot_ref[e] + (bm - 1)) // bm
            pstart_ref[e] = blk0 * bm

            def fill(j, carry):
                blk_ref[j] = e
                return carry

            lax.fori_loop(blk0, blk0 + nb, fill, 0)
            return blk0 + nb

        nact = lax.fori_loop(0, N_EXPERTS, per_expert, 0)
        nact_ref[0] = nact

        def tail(j, carry):
            blk_ref[j] = N_EXPERTS - 1
            return carry

        lax.fori_loop(nact, nblk, tail, 0)

        zero_scr[...] = jnp.zeros(zero_scr.shape, U32)

        def zero_fill(act):
            def do(copy):
                if act == "start":
                    copy.start()
                else:
                    copy.wait()

            def per_expert_pad(e, carry):
                tot = tot_ref[e]
                pad = (tot + (bm - 1)) // bm * bm - tot
                row0 = pstart_ref[e] + tot
                for k in reversed(range((bm // SUBLANES).bit_length())):
                    size = SUBLANES << k

                    @pl.when((pad & size) != 0)
                    def _():
                        row = pl.multiple_of(row0 + (pad // (2 * size)) * (2 * size), SUBLANES)
                        do(pltpu.make_async_copy(zero_scr.at[pl.ds(0, size)], xs_hbm.at[pl.ds(row, size)], zsem))

                return carry

            lax.fori_loop(0, N_EXPERTS, per_expert_pad, 0)

            def per_unused_block(j, carry):
                do(pltpu.make_async_copy(zero_scr, xs_hbm.at[pl.ds(pl.multiple_of(j * bm, bm), bm)], zsem))
                return carry

            lax.fori_loop(nact, nblk, per_unused_block, 0)

        zero_fill("start")
        zero_fill("wait")

    def copies(tile, sl, act):
        def make(lrow, grow, size):
            return pltpu.make_async_copy(loc_scr.at[sl, pl.ds(lrow, size)], xs_hbm.at[pl.ds(grow, size)], sem.at[sl])

        _segment_copies(tile, tcnt_ref, tbase_ref, pstart_ref, make, act, _local_rows(tm))

    @pl.when(i >= 2)
    def _():
        copies(i - 2, slot, "wait")

    half = f_ref.shape[1] // 2
    chunk = math.gcd(_local_rows(tm), 256)
    for r0 in range(0, _local_rows(tm), chunk):
        pos = lax.broadcasted_iota(I32, (chunk, tm), 0) + r0
        first = pos == lpos_ref[0:1, :]
        second = pos == lpos_ref[1:2, :]
        onehot = jnp.where(first, 1.0, jnp.where(second, 1.0, 0.0)).astype(BF16)
        loc_scr[slot, r0:r0 + chunk, 0:half] = _pack_pairs(
            jnp.dot(onehot, f_ref[...], preferred_element_type=F32))
        gate = jnp.sum(jnp.where(first, gate_ref[0:1, :], jnp.where(second, gate_ref[1:2, :], 0.0)),
                       axis=1, keepdims=True)
        loc_scr[slot, r0:r0 + chunk, half:half + LANES] = lax.bitcast_convert_type(
            jnp.broadcast_to(gate, (chunk, LANES)), U32)
    copies(i, slot, "start")

    @pl.when(i == nt - 1)
    def _():
        @pl.when(i >= 1)
        def _():
            copies(i - 1, 1 - slot, "wait")

        copies(i, slot, "wait")


def _dispatch(tcnt_i, tbase_i, tot_i, f, lpos, gates, tm, bm, nblk):
    n, d = f.shape
    width = d // 2 + LANES
    smem = pl.BlockSpec(memory_space=pltpu.SMEM)
    hbm = pl.BlockSpec(memory_space=pl.ANY)
    return pl.pallas_call(
        functools.partial(_dispatch_kernel, tm=tm, bm=bm, nblk=nblk),
        grid_spec=pltpu.PrefetchScalarGridSpec(
            num_scalar_prefetch=3,
            grid=(n // tm,),
            in_specs=[
                pl.BlockSpec((tm, d), lambda i, *_: (i, 0)),
                pl.BlockSpec((2, tm), lambda i, *_: (0, i)),
                pl.BlockSpec((2, tm), lambda i, *_: (0, i)),
            ],
            out_specs=[hbm, smem, smem, smem],
            scratch_shapes=[
                pltpu.VMEM((2, _local_rows(tm), width), U32),
                pltpu.VMEM((bm, width), U32),
                pltpu.SemaphoreType.DMA((2,)),
                pltpu.SemaphoreType.DMA,
            ],
        ),
        out_shape=[
            jax.ShapeDtypeStruct((nblk * bm, width), U32),
            jax.ShapeDtypeStruct((N_EXPERTS,), I32),
            jax.ShapeDtypeStruct((nblk,), I32),
            jax.ShapeDtypeStruct((1,), I32),
        ],
        compiler_params=pltpu.CompilerParams(
            dimension_semantics=("arbitrary",), vmem_limit_bytes=VMEM_LIMIT),
        name="dispatch",
    )(tcnt_i, tbase_i, tot_i, f, lpos, gates)


def _expert_kernel(blk_ref, nact_ref, x_ref, wg_ref, wu_ref, wd_ref, y_ref, wg_bf, wu_bf, wd_bf):
    j = pl.program_id(0)
    active = j < nact_ref[0]

    @pl.when(active & ((j == 0) | (blk_ref[j] != blk_ref[jnp.maximum(j - 1, 0)])))
    def _():
        wg_bf[...] = wg_ref[...].astype(BF16)
        wu_bf[...] = wu_ref[...].astype(BF16)
        wd_bf[...] = wd_ref[...].astype(BF16)

    @pl.when(active)
    def _():
        half = y_ref.shape[1]
        x = _unpack_pairs(x_ref[:, 0:half])
        gate = lax.bitcast_convert_type(x_ref[:, half:half + 1], F32)
        g = jnp.dot(x, wg_bf[...], preferred_element_type=F32)
        u = jnp.dot(x, wu_bf[...], preferred_element_type=F32)
        hdn = (g * jax.nn.sigmoid(g) * u).astype(BF16)
        y = jnp.dot(hdn, wd_bf[...], preferred_element_type=F32) * gate
        y_ref[...] = _pack_pairs(y.astype(BF16).astype(F32))

    @pl.when(j >= nact_ref[0])
    def _():
        y_ref[...] = jnp.zeros(y_ref.shape, y_ref.dtype)


def _experts(blk_e, nact, xs, wg, wu, wd, bm):
    p, width = xs.shape
    d, dff = wg.shape[-2:]
    dh = d // 2
    nblk = p // bm
    rowmap = lambda j, blk, na: (jnp.minimum(j, na[0] - 1), 0)
    wmap = lambda j, blk, na: (blk[jnp.minimum(j, na[0] - 1)], 0, 0)
    return pl.pallas_call(
        _expert_kernel,
        grid_spec=pltpu.PrefetchScalarGridSpec(
            num_scalar_prefetch=2,
            grid=(nblk,),
            in_specs=[
                pl.BlockSpec((bm, width), rowmap),
                pl.BlockSpec((None, d, dff), wmap),
                pl.BlockSpec((None, d, dff), wmap),
                pl.BlockSpec((None, dff, d), wmap),
            ],
            out_specs=pl.BlockSpec((bm, dh), lambda j, blk, na: (j, 0)),
            scratch_shapes=[pltpu.VMEM((d, dff), BF16), pltpu.VMEM((d, dff), BF16), pltpu.VMEM((dff, d), BF16)],
        ),
        out_shape=jax.ShapeDtypeStruct((p, dh), U32),
        compiler_params=pltpu.CompilerParams(
            dimension_semantics=("arbitrary",), vmem_limit_bytes=VMEM_LIMIT),
        name="experts",
    )(blk_e, nact, xs, wg, wu, wd)


def _combine_kernel(tcnt_ref, tbase_ref, pstart_ref, h_ref, lpos_ref, lnf_ref, ys_hbm, o_ref,
                    loc_scr, sem, *, tm):
    i = pl.program_id(0)
    nt = pl.num_programs(0)
    slot = i % 2

    def copies(tile, sl, act):
        def make(lrow, grow, size):
            return pltpu.make_async_copy(ys_hbm.at[pl.ds(grow, size)], loc_scr.at[sl, pl.ds(lrow, size)], sem.at[sl])

        return _segment_copies(tile, tcnt_ref, tbase_ref, pstart_ref, make, act, _local_rows(tm))

    @pl.when(i == 0)
    def _():
        copies(0, 0, "start")

    @pl.when(i + 1 < nt)
    def _():
        copies(i + 1, 1 - slot, "start")

    used = copies(i, slot, "wait")
    row = lax.broadcasted_iota(I32, (_local_rows(tm), 1), 0)
    ys = _unpack_pairs(jnp.where(row < used, loc_scr[slot], U32(0)))

    r8 = lax.broadcasted_iota(I32, (8, tm), 0)
    lp = lpos_ref[...].astype(F32)
    top = jnp.where(r8 == 0, lp[0:1], jnp.where(r8 == 1, lp[1:2], 0.0))
    cols = jnp.concatenate([top, jnp.zeros((LANES - 8, tm), F32)], axis=0).T.astype(I32)
    pos = lax.broadcasted_iota(I32, (tm, _local_rows(tm)), 1)
    sel = jnp.where(pos == cols[:, 0:1], 1.0, jnp.where(pos == cols[:, 1:2], 1.0, 0.0)).astype(BF16)
    o_ref[...] = _rms(h_ref[...] + jnp.dot(sel, ys, preferred_element_type=F32), lnf_ref[...])


def _combine(tcnt_i, tbase_i, pstart, h2, lpos, lnf, ys, tm):
    n, d = h2.shape
    return pl.pallas_call(
        functools.partial(_combine_kernel, tm=tm),
        grid_spec=pltpu.PrefetchScalarGridSpec(
            num_scalar_prefetch=3,
            grid=(n // tm,),
            in_specs=[
                pl.BlockSpec((tm, d), lambda i, *_: (i, 0)),
                pl.BlockSpec((2, tm), lambda i, *_: (0, i)),
                pl.BlockSpec((1, d), lambda i, *_: (0, 0)),
                pl.BlockSpec(memory_space=pl.ANY),
            ],
            out_specs=pl.BlockSpec((tm, d), lambda i, *_: (i, 0)),
            scratch_shapes=[pltpu.VMEM((2, _local_rows(tm), d // 2), U32), pltpu.SemaphoreType.DMA((2,))],
        ),
        out_shape=jax.ShapeDtypeStruct((n, d), F32),
        compiler_params=pltpu.CompilerParams(
            dimension_semantics=("arbitrary",), vmem_limit_bytes=VMEM_LIMIT),
        name="combine",
    )(tcnt_i, tbase_i, pstart, h2, lpos, lnf, ys)


def _swa_head_perm():
    cols = []
    for p in range(4):
        for half in range(2):
            head = half * 4 + p
            cols.extend(range(head * D_HEAD, (head + 1) * D_HEAD))
    return jnp.asarray(cols, dtype=I32)


def kernel(x, mem, positions, ln_mix_w, w_in, lambda_q1, lambda_k1, lambda_q2, lambda_k2, subln_w, sinks,
           w_out, ln_cross_w, ln_mem_w, wq_cross, wkv_cross, wo_cross, ln_moe_w, w_group, b_group,
           w_expert, b_expert, w_gate, w_up, w_down, ln_final_w):
    b, s, d = x.shape
    m_len = mem.shape[1]
    n = b * s
    assert w_in.shape[0] == 1 and d == 1024, "one layer, model width 1024 (head layout of the input projection)"
    lambda_init = 0.8 - 0.6 * math.exp(-0.3 * 0)

    tm = min(TOKEN_TILE, s)
    tw = min(WIDE_TOKEN_TILE, s)
    t_attn = ATTN_QUERY_TILE
    tk_attn = ATTN_KEY_BLOCK
    bm = EXPERT_BLOCK_ROWS

    x2 = x.reshape(n, d)
    half = D_HEAD // 2
    per_row = LANES // half
    inv_freq = jnp.exp(-math.log(ROPE_THETA) * jnp.arange(0, D_HEAD, 2, dtype=F32) / D_HEAD)
    inv128 = jnp.tile(inv_freq, per_row).reshape(1, LANES)
    pos_tiles = positions.reshape(n // tw, per_row, tw // per_row).astype(I32).transpose(0, 2, 1)
    pos_rep = jnp.repeat(pos_tiles.reshape(n // per_row, per_row), half, axis=1)
    cos_t, sin_t = _rope_tables(pos_rep, inv128, min(1024, n // per_row))

    perm = _swa_head_perm()
    sq0 = 3 * 512
    w_in_l = w_in[0]
    w_in_p = jnp.concatenate([w_in_l[:, :sq0], w_in_l[:, sq0:sq0 + 512][:, perm], w_in_l[:, sq0 + 512:]], axis=1)
    w_out_l = w_out[0]
    w_out_p = jnp.concatenate([w_out_l[:512], w_out_l[512:][perm]], axis=0)

    qkv = _inproj(x2, ln_mix_w[0].reshape(1, d), w_in_p.astype(BF16), cos_t, sin_t, tw)
    o_diff = _diff_attention(qkv, lambda_q1[0].reshape(1, -1), lambda_k1[0].reshape(1, -1),
                             lambda_q2[0].reshape(1, -1), lambda_k2[0].reshape(1, -1),
                             subln_w[0].reshape(-1, 1), b, s, t_attn, min(tk_attn, s), lambda_init)
    o_swa = _swa_attention(qkv, sinks[0].astype(F32), b, s, tm)
    kv = _memkv(mem.reshape(b * m_len, d), ln_mem_w[0].reshape(1, d), wkv_cross[0].astype(BF16), m_len)

    wr = jnp.concatenate([w_group[0].T, jnp.zeros((8 - N_GROUPS, d), F32), w_expert[0].T], axis=0)
    br = jnp.concatenate([b_group[0], jnp.zeros((8 - N_GROUPS,), F32), b_expert[0]]).reshape(-1, 1)
    h2, f, lpos, gates, tcnt, tbase = _post(
        x2, o_diff, o_swa, w_out_p.astype(BF16), ln_cross_w[0].reshape(1, d), wq_cross[0].astype(BF16), kv,
        wo_cross[0].astype(BF16), ln_moe_w[0].reshape(1, d), wr, br, b, s, tm, m_len)

    tcnt_i = tcnt[:, :, 0].astype(I32).reshape(-1)
    tbase_i = tbase[:, :, 0].astype(I32).reshape(-1)
    tot_i = tbase_i[-N_EXPERTS:] + tcnt_i[-N_EXPERTS:]
    rows_max = 2 * n + (n // tm) * N_EXPERTS * (SUBLANES - 1)
    nblk = (rows_max + N_EXPERTS * (bm - 1) + bm - 1) // bm
    xs, pstart, blk_e, nact = _dispatch(tcnt_i, tbase_i, tot_i, f, lpos, gates, tm, bm, nblk)
    ys = _experts(blk_e, nact, xs, w_gate[0], w_up[0], w_down[0], bm)
    out = _combine(tcnt_i, tbase_i, pstart, h2, lpos, ln_final_w.reshape(1, d), ys, tm)
    return out.reshape(b, s, d)
```
